```python
import functools
import jax, jax.numpy as jnp
from jax import lax
import numpy as np

D_MODEL = 1024
BATCH = 16
SEQ = 256
DEPTH = 1
DEC_BATCH = 8
DEC_SEQ = 1024
PAST_LEN = 256

GRID_W = 64
N_HEADS = 8
HEAD_DIM = 64
D_ATTN = N_HEADS * HEAD_DIM
WIN_H = 8
WIN_W = 16
D_LRU = 1024
LRU_BLOCKS = 16
LRU_BLOCK = D_LRU // LRU_BLOCKS
CONV_W = 4
LRU_C = 8.0
D_FF = 2816
N_MOD = 9
IN_WIDTH = 3 * D_ATTN + 2 * D_LRU + 2 * D_MODEL
EPS = 1e-6
NEG_INF = -1e30

kernel_name = 'hybrid_na_rglru_diffusion_step'


def rmsnorm(x, g):
    x32 = x.astype(jnp.float32)
    y = x32 * lax.rsqrt(jnp.mean(x32 * x32, axis=-1, keepdims=True) + EPS)
    return (y * g.astype(jnp.float32)).astype(x.dtype)


def modulate(xn, shift, scale):
    return xn * (1 + scale[:, None, :]) + shift[:, None, :]


def swiglu(x, w_in, w_out):
    g, u = jnp.split(x @ w_in, 2, axis=-1)
    return (jax.nn.silu(g) * u) @ w_out


def in_projection(h, w_in):
    b, l, _ = h.shape
    z = h @ w_in
    cuts = [D_ATTN, 2 * D_ATTN, 3 * D_ATTN, 3 * D_ATTN + D_LRU, 3 * D_ATTN + 2 * D_LRU]
    q, k, v, xl, gl, gates = jnp.split(z, cuts, axis=-1)
    q = q.reshape(b, l, N_HEADS, HEAD_DIM)
    k = k.reshape(b, l, N_HEADS, HEAD_DIM)
    v = v.reshape(b, l, N_HEADS, HEAD_DIM)
    return q, k, v, xl, gl, gates


def context_attention(q, k, v):
    s = jnp.einsum('bqhd,bkhd->bhqk', q, k).astype(jnp.float32) * (HEAD_DIM ** -0.5)
    p = jax.nn.softmax(s, axis=-1).astype(v.dtype)
    o = jnp.einsum('bhqk,bkhd->bqhd', p, v)
    return o.reshape(q.shape[0], q.shape[1], D_ATTN)


def neighbourhood_attention(q, k, v, kc, vc, rpb):
    b, l, nh, dh = q.shape
    rows = l // GRID_W
    kh = min(WIN_H, rows)
    scale = HEAD_DIM ** -0.5
    qg = q.reshape(b, rows, GRID_W, nh, dh)
    kg = k.reshape(b, rows, GRID_W, nh, dh)
    vg = v.reshape(b, rows, GRID_W, nh, dh)
    r = jnp.arange(rows)
    row_start = jnp.clip(r - kh // 2, 0, rows - kh)
    key_rows = row_start[:, None] + jnp.arange(kh)[None, :]
    kb = kg[:, key_rows].reshape(b, rows, kh * GRID_W, nh, dh)
    vb = vg[:, key_rows].reshape(b, rows, kh * GRID_W, nh, dh)
    cols = jnp.arange(GRID_W)
    col_start = jnp.clip(cols - WIN_W // 2, 0, GRID_W - WIN_W)
    in_win = (cols[None, :] >= col_start[:, None]) & (cols[None, :] < col_start[:, None] + WIN_W)
    mask = jnp.broadcast_to(in_win[:, None, :], (GRID_W, kh, GRID_W)).reshape(GRID_W, kh * GRID_W)
    dr = key_rows - r[:, None] + (WIN_H - 1)
    dc = jnp.clip(cols[None, :] - cols[:, None], -(WIN_W - 1), WIN_W - 1) + (WIN_W - 1)
    bias = rpb[:, dr[:, None, :, None], dc[None, :, None, :]]
    bias = bias.reshape(nh, rows, GRID_W, kh * GRID_W).astype(jnp.float32)
    s_win = jnp.einsum('brqhd,brkhd->bhrqk', qg, kb).astype(jnp.float32) * scale + bias
    s_win = jnp.where(mask, s_win, NEG_INF)
    s_ctx = jnp.einsum('brqhd,bhkd->bhrqk', qg, kc).astype(jnp.float32) * scale
    p = jax.nn.softmax(jnp.concatenate([s_win, s_ctx], axis=-1), axis=-1).astype(v.dtype)
    nw = kh * GRID_W
    o = (jnp.einsum('bhrqk,brkhd->brqhd', p[..., :nw], vb)
         + jnp.einsum('bhrqk,bhkd->brqhd', p[..., nw:], vc))
    return o.reshape(b, l, D_ATTN)


def dwconv_centred(x, w, bias):
    l = x.shape[1]
    left = CONV_W // 2
    xp = jnp.pad(x, ((0, 0), (left, CONV_W - 1 - left), (0, 0)))
    out = bias
    for j in range(CONV_W):
        out = out + xp[:, j:j + l] * w[j]
    return out


def blockdiag(x, w):
    b, l, _ = x.shape
    xb = x.reshape(b, l, LRU_BLOCKS, LRU_BLOCK)
    return jnp.einsum('blnc,ncd->blnd', xb, w.astype(jnp.float32)).reshape(b, l, D_LRU)


def linear_scan(a, bterm, h0):
    def comb(e1, e2):
        a1, b1 = e1
        a2, b2 = e2
        return a1 * a2, a2 * b1 + b2
    a_cum, b_cum = lax.associative_scan(comb, (a, bterm), axis=1)
    return b_cum + a_cum * h0[:, None, :]


def rglru_direction(xc, wa, ba, wi, bi, lam, h0, reverse):
    xs = xc[:, ::-1] if reverse else xc
    r = jax.nn.sigmoid(blockdiag(xs, wa) + ba.astype(jnp.float32))
    i = jax.nn.sigmoid(blockdiag(xs, wi) + bi.astype(jnp.float32))
    log_a = -LRU_C * r * jax.nn.softplus(-lam.astype(jnp.float32))
    a = jnp.exp(log_a)
    bterm = jnp.sqrt(-jnp.expm1(2.0 * log_a)) * (i * xs)
    h = linear_scan(a, bterm, h0.astype(jnp.float32))
    h_final = h[:, -1]
    if reverse:
        h = h[:, ::-1]
    return h, h_final


def rglru_branch(xl, gl, p, h0_f, h0_b):
    xc = dwconv_centred(xl, p['conv_w'], p['conv_b']).astype(jnp.float32)
    hf, hf_T = rglru_direction(xc, p['lru_wa'][0], p['lru_ba'][0], p['lru_wi'][0], p['lru_bi'][0],
                               p['lru_lambda'][0], h0_f, False)
    hb, hb_T = rglru_direction(xc, p['lru_wa'][1], p['lru_ba'][1], p['lru_wi'][1], p['lru_bi'][1],
                               p['lru_lambda'][1], h0_b, True)
    y = ((hf + hb) * jax.nn.gelu(gl.astype(jnp.float32))).astype(xl.dtype)
    h_final = jnp.stack([hf_T, hb_T], axis=1).astype(xl.dtype)
    return y, h_final


def merge_branches(attn, lru, gates, p):
    g_attn, g_lru = jnp.split(gates, 2, axis=-1)
    m = (jax.nn.sigmoid(g_attn) * (attn @ p['w_br_attn'])
         + jax.nn.sigmoid(g_lru) * (lru @ p['w_br_lru']))
    return m @ p['w_out']


def context_mixer(h, p):
    q, k, v, xl, gl, gates = in_projection(h, p['w_in'])
    attn = context_attention(q, k, v)
    zero = jnp.zeros((h.shape[0], D_LRU), jnp.float32)
    lru, h_final = rglru_branch(xl, gl, p, zero, zero)
    out = merge_branches(attn, lru, gates, p)
    return out, (k.transpose(0, 2, 1, 3), v.transpose(0, 2, 1, 3), h_final)


def latent_mixer(h, p, kc, vc, h0):
    q, k, v, xl, gl, gates = in_projection(h, p['w_in'])
    attn = neighbourhood_attention(q, k, v, kc, vc, p['rpb'])
    lru, _ = rglru_branch(xl, gl, p, h0[:, 0], h0[:, 1])
    return merge_branches(attn, lru, gates, p), None


def apply_layer(x, cond, p, mixer):
    mods = jax.nn.silu(cond) @ p['w_mod'] + p['b_mod']
    s1, sc1, g1, s2, sc2, g2, s3, sc3, g3 = jnp.split(mods, N_MOD, axis=-1)
    h = modulate(rmsnorm(x, p['norm_g'][0]), s1, sc1)
    x = x + 0.5 * g1[:, None, :] * swiglu(h, p['ffn1_w_in'], p['ffn1_w_out'])
    h = modulate(rmsnorm(x, p['norm_g'][1]), s2, sc2)
    out, extras = mixer(h, p)
    x = x + g2[:, None, :] * out
    h = modulate(rmsnorm(x, p['norm_g'][2]), s3, sc3)
    x = x + 0.5 * g3[:, None, :] * swiglu(h, p['ffn2_w_in'], p['ffn2_w_out'])
    return x, extras


def setup_inputs(seed: int = 0) -> dict:
    key = jax.random.key(seed)
    ks = jax.random.split(key, 32)
    f32 = jnp.float32
    nrm = lambda k, shape, s: jax.random.normal(k, shape, f32) * s
    a0 = jax.random.uniform(ks[20], (DEPTH, 2, D_LRU), f32, 0.9, 0.999)
    return {
        'x_prompt': nrm(ks[0], (BATCH, SEQ, D_MODEL), 1.0),
        'x_sample': nrm(ks[1], (DEC_BATCH, DEC_SEQ, D_MODEL), 1.0),
        'cache_k': nrm(ks[2], (DEC_BATCH, DEPTH, N_HEADS, PAST_LEN, HEAD_DIM), 1.0),
        'cache_v': nrm(ks[3], (DEC_BATCH, DEPTH, N_HEADS, PAST_LEN, HEAD_DIM), 1.0),
        'state_lru': nrm(ks[4], (DEC_BATCH, DEPTH, 2, D_LRU), 0.5),
        'c': nrm(ks[5], (DEC_BATCH, D_MODEL), 1.0),
        'c_ctx': nrm(ks[6], (D_MODEL,), 1.0),
        'w_mod': nrm(ks[7], (DEPTH, D_MODEL, N_MOD * D_MODEL), 0.5 * D_MODEL ** -0.5),
        'b_mod': nrm(ks[8], (DEPTH, N_MOD * D_MODEL), 0.02),
        'norm_g': 1.0 + nrm(ks[9], (DEPTH, 3, D_MODEL), 0.02),
        'ffn1_w_in': nrm(ks[10], (DEPTH, D_MODEL, 2 * D_FF), D_MODEL ** -0.5),
        'ffn1_w_out': nrm(ks[11], (DEPTH, D_FF, D_MODEL), D_FF ** -0.5),
        'w_in': nrm(ks[12], (DEPTH, D_MODEL, IN_WIDTH), D_MODEL ** -0.5),
        'rpb': nrm(ks[13], (DEPTH, N_HEADS, 2 * WIN_H - 1, 2 * WIN_W - 1), 0.1),
        'conv_w': nrm(ks[14], (DEPTH, CONV_W, D_LRU), CONV_W ** -0.5),
        'conv_b': nrm(ks[15], (DEPTH, D_LRU), 0.02),
        'lru_wa': nrm(ks[16], (DEPTH, 2, LRU_BLOCKS, LRU_BLOCK, LRU_BLOCK), LRU_BLOCK ** -0.5),
        'lru_ba': nrm(ks[17], (DEPTH, 2, D_LRU), 0.02),
        'lru_wi': nrm(ks[18], (DEPTH, 2, LRU_BLOCKS, LRU_BLOCK, LRU_BLOCK), LRU_BLOCK ** -0.5),
        'lru_bi': nrm(ks[19], (DEPTH, 2, D_LRU), 0.02),
        'lru_lambda': jnp.log(a0) - jnp.log1p(-a0),
        'w_br_attn': nrm(ks[21], (DEPTH, D_ATTN, D_MODEL), D_ATTN ** -0.5),
        'w_br_lru': nrm(ks[22], (DEPTH, D_LRU, D_MODEL), D_LRU ** -0.5),
        'w_out': nrm(ks[23], (DEPTH, D_MODEL, D_MODEL), D_MODEL ** -0.5),
        'ffn2_w_in': nrm(ks[24], (DEPTH, D_MODEL, 2 * D_FF), D_MODEL ** -0.5),
        'ffn2_w_out': nrm(ks[25], (DEPTH, D_FF, D_MODEL), D_FF ** -0.5),
        'final_g': 1.0 + nrm(ks[26], (D_MODEL,), 0.02),
    }


def reference(x_prompt, x_sample, cache_k, cache_v, state_lru, c, c_ctx, w_mod, b_mod, norm_g,
              ffn1_w_in, ffn1_w_out, w_in, rpb, conv_w, conv_b, lru_wa, lru_ba, lru_wi, lru_bi,
              lru_lambda, w_br_attn, w_br_lru, w_out, ffn2_w_in, ffn2_w_out, final_g):
    yp = x_prompt
    ys = x_sample
    ks_, vs_, hs_ = [], [], []
    for l in range(DEPTH):
        p = {
            'w_mod': w_mod[l], 'b_mod': b_mod[l], 'norm_g': norm_g[l],
            'ffn1_w_in': ffn1_w_in[l], 'ffn1_w_out': ffn1_w_out[l], 'w_in': w_in[l], 'rpb': rpb[l],
            'conv_w': conv_w[l], 'conv_b': conv_b[l], 'lru_wa': lru_wa[l], 'lru_ba': lru_ba[l],
            'lru_wi': lru_wi[l], 'lru_bi': lru_bi[l], 'lru_lambda': lru_lambda[l],
            'w_br_attn': w_br_attn[l], 'w_br_lru': w_br_lru[l], 'w_out': w_out[l],
            'ffn2_w_in': ffn2_w_in[l], 'ffn2_w_out': ffn2_w_out[l],
        }
        yp, (k_l, v_l, h_l) = apply_layer(yp, c_ctx[None, :], p, context_mixer)
        ks_.append(k_l)
        vs_.append(v_l)
        hs_.append(h_l)
        mixer = functools.partial(latent_mixer, kc=cache_k[:, l], vc=cache_v[:, l], h0=state_lru[:, l])
        ys, _ = apply_layer(ys, c, p, mixer)
    y_prompt = rmsnorm(yp, final_g)
    y_sample = rmsnorm(ys, final_g)
    new_cache_k = jnp.stack(ks_, axis=1)
    new_cache_v = jnp.stack(vs_, axis=1)
    new_state_lru = jnp.stack(hs_, axis=1)
    return (y_prompt, y_sample, new_cache_k, new_cache_v, new_state_lru)
```

```python
import functools

import numpy as np
import jax
import jax.numpy as jnp
from jax import lax
from jax.experimental import pallas as pl
from jax.experimental.pallas import tpu as pltpu

F32 = jnp.float32
BF16 = jnp.bfloat16

D_MODEL = 1024
N_HEADS = 8
HEAD_DIM = 64
D_ATTN = N_HEADS * HEAD_DIM
GRID_W = 64
WIN_H = 8
WIN_W = 16
D_LRU = 1024
LRU_BLOCKS = 16
LRU_BLOCK = D_LRU // LRU_BLOCKS
CONV_W = 4
LRU_C = 8.0
D_FF = 2816
N_MOD = 9
EPS = 1e-6
NEG_INF = -1e30
ATTN_SCALE = HEAD_DIM ** -0.5

LANES = 128
SUBLANES = 8
MXU_DIM = 256
VMEM_LIMIT_BYTES = 56 * 1024 * 1024

FF_CHUNK = MXU_DIM
N_FF_CHUNKS = D_FF // FF_CHUNK
LRU_TILE = MXU_DIM
N_LRU_TILES = D_LRU // LRU_TILE
N_SLABS = D_LRU // LANES
N_SEG = SUBLANES
SEG_PAD = 4
ROWS_PER_GROUP = 4


def _sigmoid(x):
    return jax.nn.sigmoid(x)


def _rms_mod(x, g, shift, scale):
    y = x * lax.rsqrt(jnp.mean(x * x, axis=-1, keepdims=True) + EPS)
    y = y * g
    return y * (1 + scale) + shift


def _dot(a, b):
    return jnp.dot(a, b, preferred_element_type=F32)


def _dot_nt(a, b):
    return lax.dot_general(a, b, (((1,), (1,)), ((), ())), preferred_element_type=F32)


def _params(n_axes=1):
    return pltpu.CompilerParams(dimension_semantics=("arbitrary",) * n_axes,
                                vmem_limit_bytes=VMEM_LIMIT_BYTES)


def _const_spec(shape):
    nd = len(shape)
    return pl.BlockSpec(shape, lambda *_: (0,) * nd, pipeline_mode=pl.Buffered(1))


def _mods_kernel(cond_ref, w_ref, b_ref, o_ref):
    c = cond_ref[...]
    s = (c * _sigmoid(c)).astype(BF16)
    o_ref[...] = _dot(s, w_ref[...].astype(BF16)) + b_ref[...]


def _mods(cond, w_mod, b_mod):
    n = w_mod.shape[1]
    bn = D_MODEL
    return pl.pallas_call(
        _mods_kernel,
        grid=(n // bn,),
        in_specs=[pl.BlockSpec(cond.shape, lambda j: (0, 0)),
                  pl.BlockSpec((D_MODEL, bn), lambda j: (0, j)),
                  pl.BlockSpec((1, bn), lambda j: (0, j))],
        out_specs=pl.BlockSpec((cond.shape[0], bn), lambda j: (0, j)),
        out_shape=jax.ShapeDtypeStruct((cond.shape[0], n), F32),
        compiler_params=_params(),
        name="mods",
    )(cond, w_mod, b_mod.reshape(1, n))


def _swiglu_into(acc_ref, h_ref, wgu_ref, wo_ref):
    acc_ref[...] = jnp.zeros_like(acc_ref)

    def body(j, carry):
        gu = _dot(h_ref[...], wgu_ref[j])
        g = gu[:, :FF_CHUNK]
        u = gu[:, FF_CHUNK:]
        a = ((g * _sigmoid(g)) * u).astype(BF16)
        acc_ref[...] += _dot(a, wo_ref[j])
        return carry

    lax.fori_loop(0, N_FF_CHUNKS, body, 0)


def _ffn1_kernel(x_ref, mod_ref, ng_ref, wgu_ref, wo_ref, win_ref,
                 x1_ref, qkv_ref, kv_ref, xl_ref, gl_ref, gates_ref, acc_ref, h_ref):
    x = x_ref[...]
    m = mod_ref[0]
    h_ref[...] = _rms_mod(x, ng_ref[0:1], m[0:1], m[1:2]).astype(BF16)
    _swiglu_into(acc_ref, h_ref, wgu_ref, wo_ref)
    x1 = x + (0.5 * m[2:3]) * acc_ref[...]
    x1_ref[...] = x1
    h2 = _rms_mod(x1, ng_ref[1:2], m[3:4], m[4:5]).astype(BF16)
    c0, c1, c2, c3 = 3 * D_ATTN, 3 * D_ATTN + D_LRU, 3 * D_ATTN + 2 * D_LRU, 3 * D_ATTN + 2 * D_LRU + 2 * D_MODEL
    qkv = _dot(h2, win_ref[:, 0:c0])
    qkv_ref[...] = qkv.astype(BF16)
    kv_ref[...] = qkv[:, D_ATTN:]
    xl_ref[...] = _dot(h2, win_ref[:, c0:c1])
    gl_ref[...] = _dot(h2, win_ref[:, c1:c2])
    gates_ref[...] = _dot(h2, win_ref[:, c2:c3])


def _ffn1(x, mods3, mod_row, norm_g, wgu, wo, win, tm):
    t = x.shape[0]
    row = lambda w: pl.BlockSpec((tm, w), lambda i: (i, 0))
    return pl.pallas_call(
        _ffn1_kernel,
        grid=(t // tm,),
        in_specs=[row(D_MODEL),
                  pl.BlockSpec((1, N_MOD, D_MODEL), lambda i: (mod_row(i * tm), 0, 0)),
                  _const_spec(norm_g.shape), _const_spec(wgu.shape), _const_spec(wo.shape),
                  _const_spec(win.shape)],
        out_specs=[row(D_MODEL), row(3 * D_ATTN), row(2 * D_ATTN), row(D_LRU), row(D_LRU), row(2 * D_MODEL)],
        out_shape=[jax.ShapeDtypeStruct((t, D_MODEL), F32),
                   jax.ShapeDtypeStruct((t, 3 * D_ATTN), BF16),
                   jax.ShapeDtypeStruct((t, 2 * D_ATTN), F32),
                   jax.ShapeDtypeStruct((t, D_LRU), F32),
                   jax.ShapeDtypeStruct((t, D_LRU), F32),
                   jax.ShapeDtypeStruct((t, 2 * D_MODEL), F32)],
        scratch_shapes=[pltpu.VMEM((tm, D_MODEL), F32), pltpu.VMEM((tm, D_MODEL), BF16)],
        compiler_params=_params(),
        name="ffn1_inproj",
    )(x, mods3, norm_g, wgu, wo, win)


def _ffn2_kernel(x1_ref, attn_ref, lru_ref, gates_ref, mod_ref, ng_ref, fg_ref,
                 wba_ref, wbl_ref, wout_ref, wgu_ref, wo_ref, y_ref, acc_ref, h_ref):
    m = mod_ref[0]
    gt = gates_ref[...]
    mm = (_sigmoid(gt[:, :D_MODEL]) * _dot(attn_ref[...], wba_ref[...])
          + _sigmoid(gt[:, D_MODEL:]) * _dot(lru_ref[...], wbl_ref[...]))
    x2 = x1_ref[...] + m[5:6] * _dot(mm.astype(BF16), wout_ref[...])
    h_ref[...] = _rms_mod(x2, ng_ref[2:3], m[6:7], m[7:8]).astype(BF16)
    _swiglu_into(acc_ref, h_ref, wgu_ref, wo_ref)
    x3 = x2 + (0.5 * m[8:9]) * acc_ref[...]
    y = x3 * lax.rsqrt(jnp.mean(x3 * x3, axis=-1, keepdims=True) + EPS)
    y_ref[...] = y * fg_ref[...]


def _ffn2(x1, attn, lru, gates, mods3, mod_row, norm_g, final_g, wba, wbl, wout, wgu, wo, tm):
    t = x1.shape[0]
    row = lambda w: pl.BlockSpec((tm, w), lambda i: (i, 0))
    return pl.pallas_call(
        _ffn2_kernel,
        grid=(t // tm,),
        in_specs=[row(D_MODEL), row(D_ATTN), row(D_LRU), row(2 * D_MODEL),
                  pl.BlockSpec((1, N_MOD, D_MODEL), lambda i: (mod_row(i * tm), 0, 0)),
                  _const_spec(norm_g.shape), _const_spec(final_g.shape),
                  _const_spec(wba.shape), _const_spec(wbl.shape), _const_spec(wout.shape),
                  _const_spec(wgu.shape), _const_spec(wo.shape)],
        out_specs=row(D_MODEL),
        out_shape=jax.ShapeDtypeStruct((t, D_MODEL), F32),
        scratch_shapes=[pltpu.VMEM((tm, D_MODEL), F32), pltpu.VMEM((tm, D_MODEL), BF16)],
        compiler_params=_params(),
        name="merge_ffn2",
    )(x1, attn, lru, gates, mods3, norm_g, final_g, wba, wbl, wout, wgu, wo)


def _softmax_pv(scores, values):
    m = scores[0].max(axis=-1, keepdims=True)
    for s in scores[1:]:
        m = jnp.maximum(m, s.max(axis=-1, keepdims=True))
    l = None
    o = None
    for s, v in zip(scores, values):
        p = jnp.exp(s - m)
        ls = p.sum(axis=-1, keepdims=True)
        os_ = _dot(p.astype(BF16), v)
        l = ls if l is None else l + ls
        o = os_ if o is None else o + os_
    return o / l


def _ctx_attn_kernel(qkv_ref, o_ref):
    seq = qkv_ref.shape[0]
    lane = lax.broadcasted_iota(jnp.int32, (seq, LANES), 1)
    first = lane < HEAD_DIM
    for hp in range(D_ATTN // LANES):
        q = qkv_ref[:, hp * LANES:(hp + 1) * LANES]
        k = qkv_ref[:, D_ATTN + hp * LANES:D_ATTN + (hp + 1) * LANES]
        v = qkv_ref[:, 2 * D_ATTN + hp * LANES:2 * D_ATTN + (hp + 1) * LANES]
        outs = []
        for sel in (first, jnp.logical_not(first)):
            qm = jnp.where(sel, q, jnp.zeros_like(q))
            s = _dot_nt(qm, k) * ATTN_SCALE
            outs.append(_softmax_pv([s], [v]))
        o_ref[:, hp * LANES:(hp + 1) * LANES] = jnp.where(first, outs[0], outs[1]).astype(BF16)


def _ctx_attention(qkv, seq):
    t = qkv.shape[0]
    return pl.pallas_call(
        _ctx_attn_kernel,
        grid=(t // seq,),
        in_specs=[pl.BlockSpec((seq, 3 * D_ATTN), lambda b: (b, 0))],
        out_specs=pl.BlockSpec((seq, D_ATTN), lambda b: (b, 0)),
        out_shape=jax.ShapeDtypeStruct((t, D_ATTN), BF16),
        compiler_params=_params(),
        name="ctx_attention",
    )(qkv)


def _na_groups(rows):
    kh = min(WIN_H, rows)
    r = np.arange(rows)
    rs = np.clip(r - kh // 2, 0, rows - kh)
    groups = []
    for g in range(rows // ROWS_PER_GROUP):
        qr = r[g * ROWS_PER_GROUP:(g + 1) * ROWS_PER_GROUP]
        k0, k1 = int(rs[qr].min()), int(rs[qr].max()) + kh
        if (k1 - k0) % 2:
            if k1 < rows:
                k1 += 1
            else:
                k0 -= 1
        groups.append((k0, k1))
    return groups, rs, kh


def _na_bias_table(rpb, rows):
    groups, rs, kh = _na_groups(rows)
    cols = np.arange(GRID_W)
    cs = np.clip(cols - WIN_W // 2, 0, GRID_W - WIN_W)
    in_win = (cols[None, :] >= cs[:, None]) & (cols[None, :] < cs[:, None] + WIN_W)
    dc = np.clip(cols[None, :] - cols[:, None], -(WIN_W - 1), WIN_W - 1) + (WIN_W - 1)
    parts = []
    for g, (k0, k1) in enumerate(groups):
        qr = np.arange(g * ROWS_PER_GROUP, (g + 1) * ROWS_PER_GROUP)
        kr = np.arange(k0, k1)
        row_ok = (kr[None, :] >= rs[qr][:, None]) & (kr[None, :] < rs[qr][:, None] + kh)
        dr = np.clip(kr[None, :] - qr[:, None] + (WIN_H - 1), 0, 2 * WIN_H - 2)
        b = rpb[:, dr[:, None, :, None], dc[None, :, None, :]].astype(F32)
        ok = row_ok[:, None, :, None] & in_win[None, :, None, :]
        b = jnp.where(ok[None], b, NEG_INF)
        parts.append(b.reshape(N_HEADS, ROWS_PER_GROUP * GRID_W, (k1 - k0) * GRID_W))
    return jnp.concatenate(parts, axis=-1)


def _na_attn_kernel(q_ref, k_ref, v_ref, kc_ref, vc_ref, bias_ref, o_ref, *, groups):
    gq = ROWS_PER_GROUP * GRID_W
    lane = lax.broadcasted_iota(jnp.int32, (gq, LANES), 1)
    first = lane < HEAD_DIM
    kc = kc_ref[...]
    vc = vc_ref[...]
    for g, (k0, k1) in enumerate(groups):
        q = q_ref[g * gq:(g + 1) * gq, :]
        k = k_ref[k0 * GRID_W:k1 * GRID_W, :]
        v = v_ref[k0 * GRID_W:k1 * GRID_W, :]
        boff = sum((b - a) * GRID_W for a, b in groups[:g])
        outs = []
        for hh, sel in enumerate((first, jnp.logical_not(first))):
            qm = jnp.where(sel, q, jnp.zeros_like(q))
            s_win = _dot_nt(qm, k) * ATTN_SCALE + bias_ref[hh, :, boff:boff + (k1 - k0) * GRID_W]
            s_ctx = _dot_nt(qm, kc) * ATTN_SCALE
            outs.append(_softmax_pv([s_win, s_ctx], [v, vc]))
        o_ref[g * gq:(g + 1) * gq, :] = jnp.where(first, outs[0], outs[1]).astype(BF16)


def _na_attention(qkv, kc, vc, bias, seq, past):
    t = qkv.shape[0]
    nb = t // seq
    npair = D_ATTN // LANES
    groups, _, _ = _na_groups(seq // GRID_W)
    tok = lambda off: pl.BlockSpec((seq, LANES), lambda hp, b: (b, off + hp))
    ctx = pl.BlockSpec((past, LANES), lambda hp, b: (b, hp))
    return pl.pallas_call(
        functools.partial(_na_attn_kernel, groups=groups),
        grid=(npair, nb),
        in_specs=[tok(0), tok(npair), tok(2 * npair), ctx, ctx,
                  pl.BlockSpec((2,) + bias.shape[1:], lambda hp, b: (hp, 0, 0))],
        out_specs=pl.BlockSpec((seq, LANES), lambda hp, b: (b, hp)),
        out_shape=jax.ShapeDtypeStruct((t, D_ATTN), BF16),
        compiler_params=_params(2),
        name="na_attention",
    )(qkv, qkv, qkv, kc, vc, bias)


def _gelu_tanh(x):
    cdf = 0.5 * (1.0 + jnp.tanh(np.sqrt(2.0 / np.pi).astype(np.float32) * (x + 0.044715 * (x * x * x))))
    return x * cdf


def _lru_kernel(xl_ref, gl_ref, h0_ref, cw_ref, cb_ref, wbd_ref, ba_ref, bi_ref, coef_ref,
                y_ref, hfin_ref, xpad, xc, a_buf, hf_buf, hb_buf, *, seq):
    pitch = seq // N_SEG + SEG_PAD
    lp = N_SEG * pitch
    halo = SUBLANES
    chunk = LANES

    xpad[0:halo, :] = jnp.zeros((halo, D_LRU), F32)
    xpad[halo + seq:2 * halo + seq, :] = jnp.zeros((halo, D_LRU), F32)
    xpad[halo:halo + seq, :] = xl_ref[...]
    left = CONV_W // 2
    for c in range(seq // chunk):
        r0 = c * chunk
        out = cb_ref[...]
        for j in range(CONV_W):
            out = out + xpad[halo + r0 + j - left:halo + r0 + j - left + chunk, :] * cw_ref[j:j + 1, :]
        xc[r0:r0 + chunk, :] = out

    a_buf[:, seq:lp, :] = jnp.ones((N_SLABS, lp - seq, LANES), F32)
    hf_buf[:, seq:lp, :] = jnp.zeros((N_SLABS, lp - seq, LANES), F32)
    hb_buf[:, seq:lp, :] = jnp.zeros((N_SLABS, lp - seq, LANES), F32)

    row = lax.broadcasted_iota(jnp.int32, (N_SEG, LANES), 0)
    for d, h_buf in enumerate((hf_buf, hb_buf)):
        def gates_body(i, carry, d=d, h_buf=h_buf):
            r0 = pl.multiple_of(i * chunk, chunk)
            for c in range(N_LRU_TILES):
                cols = slice(c * LRU_TILE, (c + 1) * LRU_TILE)
                x = xc[pl.ds(r0, chunk), cols]
                pre = _dot(x.astype(BF16), wbd_ref[d, c])
                r = _sigmoid(pre[:, :LRU_TILE] + ba_ref[d:d + 1, cols])
                ig = _sigmoid(pre[:, LRU_TILE:] + bi_ref[d:d + 1, cols])
                log_a = coef_ref[d:d + 1, cols] * r
                a = jnp.exp(log_a)
                b = jnp.sqrt((1.0 - a) * (1.0 + a)) * (ig * x)
                for half in range(LRU_TILE // LANES):
                    k = c * (LRU_TILE // LANES) + half
                    a_buf[k, pl.ds(r0, chunk), :] = a[:, half * LANES:(half + 1) * LANES]
                    h_buf[k, pl.ds(r0, chunk), :] = b[:, half * LANES:(half + 1) * LANES]
            return carry

        lax.fori_loop(0, seq // chunk, gates_body, 0)

        def seg_rows(t, d=d):
            tt = t if d == 0 else pitch - 1 - t
            return pl.ds(tt, N_SEG, stride=pitch)

        def totals_body(t, carry, h_buf=h_buf):
            hs, ps = carry
            idx = seg_rows(t)
            nh, np_ = [], []
            for k in range(N_SLABS):
                a = a_buf[k, idx, :]
                b = h_buf[k, idx, :]
                nh.append(a * hs[k] + b)
                np_.append(ps[k] * a)
            return tuple(nh), tuple(np_)

        zeros = tuple(jnp.zeros((N_SEG, LANES), F32) for _ in range(N_SLABS))
        ones = tuple(jnp.ones((N_SEG, LANES), F32) for _ in range(N_SLABS))
        h_end, p_end = lax.fori_loop(0, pitch, totals_body, (zeros, ones))

        order = range(N_SEG) if d == 0 else range(N_SEG - 1, -1, -1)
        starts = []
        for k in range(N_SLABS):
            c = h0_ref[0, d:d + 1, k * LANES:(k + 1) * LANES]
            cm = jnp.zeros((N_SEG, LANES), F32)
            for s in order:
                cm = jnp.where(row == s, c, cm)
                c = p_end[k][s:s + 1, :] * c + h_end[k][s:s + 1, :]
            starts.append(cm)
            hfin_ref[0, d:d + 1, k * LANES:(k + 1) * LANES] = c

        def scan_body(t, hs, h_buf=h_buf):
            idx = seg_rows(t)
            nh = []
            for k in range(N_SLABS):
                h = a_buf[k, idx, :] * hs[k] + h_buf[k, idx, :]
                h_buf[k, idx, :] = h
                nh.append(h)
            return tuple(nh)

        lax.fori_loop(0, pitch, scan_body, tuple(starts))

    def out_body(i, carry):
        r0 = pl.multiple_of(i * chunk, chunk)
        for k in range(N_SLABS):
            cols = slice(k * LANES, (k + 1) * LANES)
            h = hf_buf[k, pl.ds(r0, chunk), :] + hb_buf[k, pl.ds(r0, chunk), :]
            y_ref[pl.ds(r0, chunk), cols] = (h * _gelu_tanh(gl_ref[pl.ds(r0, chunk), cols])).astype(BF16)
        return carry

    lax.fori_loop(0, seq // chunk, out_body, 0)


def _lru(xl, gl, h0, conv_w, conv_b, wbd, ba, bi, coef, seq):
    t = xl.shape[0]
    nb = t // seq
    lp = N_SEG * (seq // N_SEG + SEG_PAD)
    tok = pl.BlockSpec((seq, D_LRU), lambda b: (b, 0))
    state = pl.BlockSpec((1, 2, D_LRU), lambda b: (b, 0, 0))
    scan_buf = pltpu.VMEM((N_SLABS, lp, LANES), F32)
    return pl.pallas_call(
        functools.partial(_lru_kernel, seq=seq),
        grid=(nb,),
        in_specs=[tok, tok, state, _const_spec(conv_w.shape), _const_spec(conv_b.shape),
                  _const_spec(wbd.shape), _const_spec(ba.shape), _const_spec(bi.shape),
                  _const_spec(coef.shape)],
        out_specs=[tok, state],
        out_shape=[jax.ShapeDtypeStruct((t, D_LRU), BF16), jax.ShapeDtypeStruct((nb, 2, D_LRU), F32)],
        scratch_shapes=[pltpu.VMEM((seq + 2 * SUBLANES, D_LRU), F32), pltpu.VMEM((seq, D_LRU), F32),
                        scan_buf, scan_buf, scan_buf],
        compiler_params=_params(),
        name="rglru",
    )(xl, gl, h0, conv_w, conv_b, wbd, ba, bi, coef)


def _swiglu_weights(w_in, w_out):
    g = w_in[:, :D_FF].reshape(D_MODEL, N_FF_CHUNKS, FF_CHUNK)
    u = w_in[:, D_FF:].reshape(D_MODEL, N_FF_CHUNKS, FF_CHUNK)
    wgu = jnp.concatenate([g, u], axis=-1).transpose(1, 0, 2).astype(BF16)
    wo = w_out.reshape(N_FF_CHUNKS, FF_CHUNK, D_MODEL).astype(BF16)
    return wgu, wo


def _blockdiag_weights(wa, wi):
    per = LRU_TILE // LRU_BLOCK
    eye = jnp.eye(per, dtype=F32)

    def bd(w):
        w = w.reshape(2, N_LRU_TILES, per, LRU_BLOCK, LRU_BLOCK).astype(F32)
        full = jnp.einsum("dtpij,pq->dtpiqj", w, eye)
        return full.reshape(2, N_LRU_TILES, LRU_TILE, LRU_TILE)

    return jnp.concatenate([bd(wa), bd(wi)], axis=-1).astype(BF16)


def kernel(x_prompt, x_sample, cache_k, cache_v, state_lru, c, c_ctx, w_mod, b_mod, norm_g, ffn1_w_in, ffn1_w_out,
           w_in, rpb, conv_w, conv_b, lru_wa, lru_ba, lru_wi, lru_bi, lru_lambda, w_br_attn, w_br_lru, w_out,
           ffn2_w_in, ffn2_w_out, final_g):
    assert w_mod.shape[0] == 1, "single trunk layer"
    nb_ctx, seq_ctx, _ = x_prompt.shape
    nb_lat, seq_lat, _ = x_sample.shape
    past = cache_k.shape[3]

    n_cond = 1 + nb_lat
    cond = jnp.concatenate([c_ctx[None, :], c, jnp.zeros((2 * SUBLANES - n_cond, D_MODEL), F32)], axis=0)
    mods3 = _mods(cond, w_mod[0], b_mod[0]).reshape(2 * SUBLANES, N_MOD, D_MODEL)

    wgu1, wo1 = _swiglu_weights(ffn1_w_in[0], ffn1_w_out[0])
    wgu2, wo2 = _swiglu_weights(ffn2_w_in[0], ffn2_w_out[0])
    win = w_in[0].astype(BF16)
    wba = w_br_attn[0].astype(BF16)
    wbl = w_br_lru[0].astype(BF16)
    wout = w_out[0].astype(BF16)
    wbd = _blockdiag_weights(lru_wa[0], lru_wi[0])
    coef = -LRU_C * jax.nn.softplus(-lru_lambda[0])
    ng = norm_g[0]
    fg = final_g.reshape(1, D_MODEL)
    cb = conv_b[0].reshape(1, D_LRU)

    ctx_row = lambda tok: 0
    lat_row = lambda tok: 1 + tok // seq_lat

    xp = x_prompt.reshape(nb_ctx * seq_ctx, D_MODEL)
    xs = x_sample.reshape(nb_lat * seq_lat, D_MODEL)

    x1p, qkvp, kvp, xlp, glp, gatesp = _ffn1(xp, mods3, ctx_row, ng, wgu1, wo1, win, 256)
    attnp = _ctx_attention(qkvp, seq_ctx)
    lrup, h_fin = _lru(xlp, glp, jnp.zeros((nb_ctx, 2, D_LRU), F32), conv_w[0], cb, wbd,
                       lru_ba[0], lru_bi[0], coef, seq_ctx)
    yp = _ffn2(x1p, attnp, lrup, gatesp, mods3, ctx_row, ng, fg, wba, wbl, wout, wgu2, wo2, 512)

    kc = cache_k[:, 0].transpose(0, 2, 1, 3).reshape(nb_lat * past, D_ATTN).astype(BF16)
    vc = cache_v[:, 0].transpose(0, 2, 1, 3).reshape(nb_lat * past, D_ATTN).astype(BF16)
    bias = _na_bias_table(rpb[0], seq_lat // GRID_W)
    x1s, qkvs, _, xls, gls, gatess = _ffn1(xs, mods3, lat_row, ng, wgu1, wo1, win, 256)
    attns = _na_attention(qkvs, kc, vc, bias, seq_lat, past)
    lrus, _ = _lru(xls, gls, state_lru[:, 0], conv_w[0], cb, wbd, lru_ba[0], lru_bi[0], coef, seq_lat)
    ys = _ffn2(x1s, attns, lrus, gatess, mods3, lat_row, ng, fg, wba, wbl, wout, wgu2, wo2, 512)

    kv = kvp.reshape(nb_ctx, seq_ctx, 2, N_HEADS, HEAD_DIM)
    new_k = kv[:, :, 0].transpose(0, 2, 1, 3)[:, None]
    new_v = kv[:, :, 1].transpose(0, 2, 1, 3)[:, None]
    return (yp.reshape(nb_ctx, seq_ctx, D_MODEL), ys.reshape(nb_lat, seq_lat, D_MODEL),
            new_k, new_v, h_fin[:, None])
```

```python
import functools

import numpy as np
import jax
import jax.numpy as jnp
from jax import lax
from jax.experimental import pallas as pl
from jax.experimental.pallas import tpu as pltpu

F32 = jnp.float32
BF16 = jnp.bfloat16

D_MODEL = 1024
N_HEADS = 8
HEAD_DIM = 64
D_ATTN = N_HEADS * HEAD_DIM
GRID_W = 64
WIN_H = 8
WIN_W = 16
D_LRU = 1024
LRU_BLOCKS = 16
LRU_BLOCK = D_LRU // LRU_BLOCKS
CONV_W = 4
LRU_C = 8.0
D_FF = 2816
N_MOD = 9
EPS = 1e-6
NEG_INF = -1e30
ATTN_SCALE = HEAD_DIM ** -0.5

LANES = 128
SUBLANES = 8
MXU_DIM = 256
VMEM_LIMIT_BYTES = 56 * 1024 * 1024

FF_CHUNK = MXU_DIM
N_FF_CHUNKS = D_FF // FF_CHUNK
LRU_TILE = MXU_DIM
N_LRU_TILES = D_LRU // LRU_TILE
N_SLABS = D_LRU // LANES
N_SEG = SUBLANES
SEG_PAD = 4
ROWS_PER_GROUP = 4


def _sigmoid(x):
    return 0.5 * jnp.tanh(0.5 * x) + 0.5


def _rms_mod(x, g, shift, scale):
    y = x * lax.rsqrt(jnp.mean(x * x, axis=-1, keepdims=True) + EPS)
    y = y * g
    return y * (1 + scale) + shift


def _dot(a, b):
    return jnp.dot(a, b, preferred_element_type=F32)


def _dot_nt(a, b):
    return lax.dot_general(a, b, (((1,), (1,)), ((), ())), preferred_element_type=F32)


def _params(n_axes=1):
    return pltpu.CompilerParams(dimension_semantics=("arbitrary",) * n_axes,
                                vmem_limit_bytes=VMEM_LIMIT_BYTES)


def _const_spec(shape):
    nd = len(shape)
    return pl.BlockSpec(shape, lambda *_: (0,) * nd, pipeline_mode=pl.Buffered(1))


def _mods_kernel(cond_ref, w_ref, b_ref, o_ref):
    c = cond_ref[...]
    s = (c * _sigmoid(c)).astype(BF16)
    o_ref[...] = _dot(s, w_ref[...].astype(BF16)) + b_ref[...]


def _mods(cond, w_mod, b_mod):
    n = w_mod.shape[1]
    bn = D_MODEL
    return pl.pallas_call(
        _mods_kernel,
        grid=(n // bn,),
        in_specs=[pl.BlockSpec(cond.shape, lambda j: (0, 0)),
                  pl.BlockSpec((D_MODEL, bn), lambda j: (0, j)),
                  pl.BlockSpec((1, bn), lambda j: (0, j))],
        out_specs=pl.BlockSpec((cond.shape[0], bn), lambda j: (0, j)),
        out_shape=jax.ShapeDtypeStruct((cond.shape[0], n), F32),
        compiler_params=_params(),
        name="mods",
    )(cond, w_mod, b_mod.reshape(1, n))


def _swiglu_into(acc_ref, h_ref, wgu_ref, wo_ref):
    acc_ref[...] = jnp.zeros_like(acc_ref)

    def body(j, carry):
        gu = _dot(h_ref[...], wgu_ref[j])
        g = gu[:, :FF_CHUNK]
        u = gu[:, FF_CHUNK:]
        a = ((g * _sigmoid(g)) * u).astype(BF16)
        acc_ref[...] += _dot(a, wo_ref[j])
        return carry

    lax.fori_loop(0, N_FF_CHUNKS, body, 0, unroll=True)


def _ffn1_kernel(x_ref, mod_ref, ng_ref, wgu_ref, wo_ref, win_ref,
                 x1_ref, qkv_ref, xl_ref, gl_ref, gates_ref, *rest):
    kv_ref = rest[0] if len(rest) == 3 else None
    acc_ref, h_ref = rest[-2:]
    x = x_ref[...]
    m = mod_ref[0]
    h_ref[...] = _rms_mod(x, ng_ref[0:1], m[0:1], m[1:2]).astype(BF16)
    _swiglu_into(acc_ref, h_ref, wgu_ref, wo_ref)
    x1 = x + (0.5 * m[2:3]) * acc_ref[...]
    x1_ref[...] = x1
    h2 = _rms_mod(x1, ng_ref[1:2], m[3:4], m[4:5]).astype(BF16)
    c0, c1, c2, c3 = 3 * D_ATTN, 3 * D_ATTN + D_LRU, 3 * D_ATTN + 2 * D_LRU, 3 * D_ATTN + 2 * D_LRU + 2 * D_MODEL
    qkv = _dot(h2, win_ref[:, 0:c0])
    qkv_ref[...] = qkv.astype(BF16)
    if kv_ref is not None:
        kv_ref[...] = qkv[:, D_ATTN:]
    xl_ref[...] = _dot(h2, win_ref[:, c0:c1])
    gl_ref[...] = _dot(h2, win_ref[:, c1:c2])
    gates_ref[...] = _dot(h2, win_ref[:, c2:c3])


def _ffn1(x, mods3, mod_row, norm_g, wgu, wo, win, tm, want_kv):
    t = x.shape[0]
    row = lambda w: pl.BlockSpec((tm, w), lambda i: (i, 0))
    out_specs = [row(D_MODEL), row(3 * D_ATTN), row(D_LRU), row(D_LRU), row(2 * D_MODEL)]
    out_shape = [jax.ShapeDtypeStruct((t, D_MODEL), F32),
                 jax.ShapeDtypeStruct((t, 3 * D_ATTN), BF16),
                 jax.ShapeDtypeStruct((t, D_LRU), F32),
                 jax.ShapeDtypeStruct((t, D_LRU), F32),
                 jax.ShapeDtypeStruct((t, 2 * D_MODEL), F32)]
    if want_kv:
        out_specs.append(row(2 * D_ATTN))
        out_shape.append(jax.ShapeDtypeStruct((t, 2 * D_ATTN), F32))
    return pl.pallas_call(
        _ffn1_kernel,
        grid=(t // tm,),
        in_specs=[row(D_MODEL),
                  pl.BlockSpec((1, N_MOD, D_MODEL), lambda i: (mod_row(i * tm), 0, 0)),
                  _const_spec(norm_g.shape), _const_spec(wgu.shape), _const_spec(wo.shape),
                  _const_spec(win.shape)],
        out_specs=out_specs,
        out_shape=out_shape,
        scratch_shapes=[pltpu.VMEM((tm, D_MODEL), F32), pltpu.VMEM((tm, D_MODEL), BF16)],
        compiler_params=_params(),
        name="ffn1_inproj",
    )(x, mods3, norm_g, wgu, wo, win)


def _ffn2_kernel(x1_ref, attn_ref, lru_ref, gates_ref, mod_ref, ng_ref, fg_ref,
                 wba_ref, wbl_ref, wout_ref, wgu_ref, wo_ref, y_ref, acc_ref, h_ref):
    m = mod_ref[0]
    gt = gates_ref[...]
    mm = (_sigmoid(gt[:, :D_MODEL]) * _dot(attn_ref[...], wba_ref[...])
          + _sigmoid(gt[:, D_MODEL:]) * _dot(lru_ref[...], wbl_ref[...]))
    x2 = x1_ref[...] + m[5:6] * _dot(mm.astype(BF16), wout_ref[...])
    h_ref[...] = _rms_mod(x2, ng_ref[2:3], m[6:7], m[7:8]).astype(BF16)
    _swiglu_into(acc_ref, h_ref, wgu_ref, wo_ref)
    x3 = x2 + (0.5 * m[8:9]) * acc_ref[...]
    y = x3 * lax.rsqrt(jnp.mean(x3 * x3, axis=-1, keepdims=True) + EPS)
    y_ref[...] = y * fg_ref[...]


def _ffn2(x1, attn, lru, gates, mods3, mod_row, norm_g, final_g, wba, wbl, wout, wgu, wo, tm):
    t = x1.shape[0]
    row = lambda w: pl.BlockSpec((tm, w), lambda i: (i, 0))
    return pl.pallas_call(
        _ffn2_kernel,
        grid=(t // tm,),
        in_specs=[row(D_MODEL), row(D_ATTN), row(D_LRU), row(2 * D_MODEL),
                  pl.BlockSpec((1, N_MOD, D_MODEL), lambda i: (mod_row(i * tm), 0, 0)),
                  _const_spec(norm_g.shape), _const_spec(final_g.shape),
                  _const_spec(wba.shape), _const_spec(wbl.shape), _const_spec(wout.shape),
                  _const_spec(wgu.shape), _const_spec(wo.shape)],
        out_specs=row(D_MODEL),
        out_shape=jax.ShapeDtypeStruct((t, D_MODEL), F32),
        scratch_shapes=[pltpu.VMEM((tm, D_MODEL), F32), pltpu.VMEM((tm, D_MODEL), BF16)],
        compiler_params=_params(),
        name="merge_ffn2",
    )(x1, attn, lru, gates, mods3, norm_g, final_g, wba, wbl, wout, wgu, wo)


def _softmax_pv(scores, values):
    m = scores[0].max(axis=-1, keepdims=True)
    for s in scores[1:]:
        m = jnp.maximum(m, s.max(axis=-1, keepdims=True))
    l = None
    o = None
    for s, v in zip(scores, values):
        p = jnp.exp(s - m)
        ls = p.sum(axis=-1, keepdims=True)
        os_ = _dot(p.astype(BF16), v)
        l = ls if l is None else l + ls
        o = os_ if o is None else o + os_
    return o / l


def _ctx_attn_kernel(qkv_ref, o_ref):
    seq = qkv_ref.shape[0]
    lane = lax.broadcasted_iota(jnp.int32, (seq, LANES), 1)
    first = lane < HEAD_DIM
    for hp in range(D_ATTN // LANES):
        q = qkv_ref[:, hp * LANES:(hp + 1) * LANES]
        k = qkv_ref[:, D_ATTN + hp * LANES:D_ATTN + (hp + 1) * LANES]
        v = qkv_ref[:, 2 * D_ATTN + hp * LANES:2 * D_ATTN + (hp + 1) * LANES]
        outs = []
        for sel in (first, jnp.logical_not(first)):
            qm = jnp.where(sel, q, jnp.zeros_like(q))
            s = _dot_nt(qm, k) * ATTN_SCALE
            outs.append(_softmax_pv([s], [v]))
        o_ref[:, hp * LANES:(hp + 1) * LANES] = jnp.where(first, outs[0], outs[1]).astype(BF16)


def _ctx_attention(qkv, seq):
    t = qkv.shape[0]
    return pl.pallas_call(
        _ctx_attn_kernel,
        grid=(t // seq,),
        in_specs=[pl.BlockSpec((seq, 3 * D_ATTN), lambda b: (b, 0))],
        out_specs=pl.BlockSpec((seq, D_ATTN), lambda b: (b, 0)),
        out_shape=jax.ShapeDtypeStruct((t, D_ATTN), BF16),
        compiler_params=_params(),
        name="ctx_attention",
    )(qkv)


def _na_groups(rows):
    kh = min(WIN_H, rows)
    r = np.arange(rows)
    rs = np.clip(r - kh // 2, 0, rows - kh)
    groups = []
    for g in range(rows // ROWS_PER_GROUP):
        qr = r[g * ROWS_PER_GROUP:(g + 1) * ROWS_PER_GROUP]
        k0, k1 = int(rs[qr].min()), int(rs[qr].max()) + kh
        if (k1 - k0) % 2:
            if k1 < rows:
                k1 += 1
            else:
                k0 -= 1
        groups.append((k0, k1))
    return groups, rs, kh


def _na_bias_table(rpb, rows):
    groups, rs, kh = _na_groups(rows)
    n_dr = 2 * WIN_H - 1
    cols = np.arange(GRID_W)
    cs = np.clip(cols - WIN_W // 2, 0, GRID_W - WIN_W)
    in_win = (cols[None, :] >= cs[:, None]) & (cols[None, :] < cs[:, None] + WIN_W)
    edge = GRID_W - WIN_W
    v = jnp.concatenate([jnp.repeat(rpb[..., :1], edge, axis=-1), rpb.astype(F32),
                         jnp.repeat(rpb[..., -1:], edge, axis=-1),
                         jnp.zeros(rpb.shape[:-1] + (1,), F32)], axis=-1)
    flat = jnp.tile(v, (1, 1, GRID_W))[..., :GRID_W * (2 * GRID_W - 1)]
    t = flat.reshape(N_HEADS, n_dr, GRID_W, 2 * GRID_W - 1)[..., GRID_W - 1:]
    t = jnp.where(in_win[None, None], t, NEG_INF)
    t = jnp.concatenate([t, jnp.full((N_HEADS, 1, GRID_W, GRID_W), NEG_INF, F32)], axis=1)
    parts = []
    for g, (k0, k1) in enumerate(groups):
        qr = np.arange(g * ROWS_PER_GROUP, (g + 1) * ROWS_PER_GROUP)
        kr = np.arange(k0, k1)
        row_ok = (kr[None, :] >= rs[qr][:, None]) & (kr[None, :] < rs[qr][:, None] + kh)
        dr = np.where(row_ok, kr[None, :] - qr[:, None] + (WIN_H - 1), n_dr)
        b = jnp.take(t, jnp.asarray(dr.reshape(-1), jnp.int32), axis=1)
        b = b.reshape(N_HEADS, ROWS_PER_GROUP, k1 - k0, GRID_W, GRID_W).transpose(0, 1, 3, 2, 4)
        parts.append(b.reshape(N_HEADS, ROWS_PER_GROUP * GRID_W, (k1 - k0) * GRID_W))
    return jnp.concatenate(parts, axis=-1)


def _na_attn_kernel(q_ref, k_ref, v_ref, kc_ref, vc_ref, bias_ref, o_ref, *, groups):
    gq = ROWS_PER_GROUP * GRID_W
    lane = lax.broadcasted_iota(jnp.int32, (gq, LANES), 1)
    first = lane < HEAD_DIM
    kc = kc_ref[...]
    vc = vc_ref[...]
    for g, (k0, k1) in enumerate(groups):
        q = q_ref[g * gq:(g + 1) * gq, :]
        k = k_ref[k0 * GRID_W:k1 * GRID_W, :]
        v = v_ref[k0 * GRID_W:k1 * GRID_W, :]
        boff = sum((b - a) * GRID_W for a, b in groups[:g])
        outs = []
        for hh, sel in enumerate((first, jnp.logical_not(first))):
            qm = jnp.where(sel, q, jnp.zeros_like(q))
            s_win = _dot_nt(qm, k) * ATTN_SCALE + bias_ref[hh, :, boff:boff + (k1 - k0) * GRID_W]
            s_ctx = _dot_nt(qm, kc) * ATTN_SCALE
            outs.append(_softmax_pv([s_win, s_ctx], [v, vc]))
        o_ref[g * gq:(g + 1) * gq, :] = jnp.where(first, outs[0], outs[1]).astype(BF16)


def _na_attention(qkv, kc, vc, bias, seq, past):
    t = qkv.shape[0]
    nb = t // seq
    npair = D_ATTN // LANES
    groups, _, _ = _na_groups(seq // GRID_W)
    tok = lambda off: pl.BlockSpec((seq, LANES), lambda hp, b: (b, off + hp))
    ctx = pl.BlockSpec((past, LANES), lambda hp, b: (b, hp))
    return pl.pallas_call(
        functools.partial(_na_attn_kernel, groups=groups),
        grid=(npair, nb),
        in_specs=[tok(0), tok(npair), tok(2 * npair), ctx, ctx,
                  pl.BlockSpec((2,) + bias.shape[1:], lambda hp, b: (hp, 0, 0))],
        out_specs=pl.BlockSpec((seq, LANES), lambda hp, b: (b, hp)),
        out_shape=jax.ShapeDtypeStruct((t, D_ATTN), BF16),
        compiler_params=_params(2),
        name="na_attention",
    )(qkv, qkv, qkv, kc, vc, bias)


def _gelu_tanh(x):
    cdf = 0.5 * (1.0 + jnp.tanh(np.sqrt(2.0 / np.pi).astype(np.float32) * (x + 0.044715 * (x * x * x))))
    return x * cdf


def _lru_kernel(xl_ref, gl_ref, h0_ref, cw_ref, cb_ref, wbd_ref, ba_ref, bi_ref, coef_ref,
                y_ref, hfin_ref, xpad, xc, a_buf, hf_buf, hb_buf, *, seq):
    pitch = seq // N_SEG + SEG_PAD
    lp = N_SEG * pitch
    halo = SUBLANES
    chunk = LANES

    xpad[0:halo, :] = jnp.zeros((halo, D_LRU), F32)
    xpad[halo + seq:2 * halo + seq, :] = jnp.zeros((halo, D_LRU), F32)
    xpad[halo:halo + seq, :] = xl_ref[...]
    left = CONV_W // 2
    for c in range(seq // chunk):
        r0 = c * chunk
        out = cb_ref[...]
        for j in range(CONV_W):
            out = out + xpad[halo + r0 + j - left:halo + r0 + j - left + chunk, :] * cw_ref[j:j + 1, :]
        xc[r0:r0 + chunk, :] = out

    a_buf[:, seq:lp, :] = jnp.ones((N_SLABS, lp - seq, LANES), F32)
    hf_buf[:, seq:lp, :] = jnp.zeros((N_SLABS, lp - seq, LANES), F32)
    hb_buf[:, seq:lp, :] = jnp.zeros((N_SLABS, lp - seq, LANES), F32)

    row = lax.broadcasted_iota(jnp.int32, (N_SEG, LANES), 0)
    for d, h_buf in enumerate((hf_buf, hb_buf)):
        def gates_body(i, carry, d=d, h_buf=h_buf):
            r0 = pl.multiple_of(i * chunk, chunk)
            for c in range(N_LRU_TILES):
                cols = slice(c * LRU_TILE, (c + 1) * LRU_TILE)
                x = xc[pl.ds(r0, chunk), cols]
                pre = _dot(x.astype(BF16), wbd_ref[d, c])
                r = _sigmoid(pre[:, :LRU_TILE] + ba_ref[d:d + 1, cols])
                ig = _sigmoid(pre[:, LRU_TILE:] + bi_ref[d:d + 1, cols])
                log_a = coef_ref[d:d + 1, cols] * r
                a = jnp.exp(log_a)
                b = jnp.sqrt((1.0 - a) * (1.0 + a)) * (ig * x)
                for half in range(LRU_TILE // LANES):
                    k = c * (LRU_TILE // LANES) + half
                    a_buf[k, pl.ds(r0, chunk), :] = a[:, half * LANES:(half + 1) * LANES]
                    h_buf[k, pl.ds(r0, chunk), :] = b[:, half * LANES:(half + 1) * LANES]
            return carry

        lax.fori_loop(0, seq // chunk, gates_body, 0)

        def seg_rows(t, d=d):
            tt = t if d == 0 else pitch - 1 - t
            return pl.ds(tt, N_SEG, stride=pitch)

        def totals_body(t, carry, h_buf=h_buf):
            hs, ps = carry
            idx = seg_rows(t)
            nh, np_ = [], []
            for k in range(N_SLABS):
                a = a_buf[k, idx, :]
                b = h_buf[k, idx, :]
                nh.append(a * hs[k] + b)
                np_.append(ps[k] * a)
            return tuple(nh), tuple(np_)

        zeros = tuple(jnp.zeros((N_SEG, LANES), F32) for _ in range(N_SLABS))
        ones = tuple(jnp.ones((N_SEG, LANES), F32) for _ in range(N_SLABS))
        h_end, p_end = lax.fori_loop(0, pitch, totals_body, (zeros, ones))

        order = range(N_SEG) if d == 0 else range(N_SEG - 1, -1, -1)
        starts = []
        for k in range(N_SLABS):
            c = h0_ref[0, d:d + 1, k * LANES:(k + 1) * LANES]
            cm = jnp.zeros((N_SEG, LANES), F32)
            for s in order:
                cm = jnp.where(row == s, c, cm)
                c = p_end[k][s:s + 1, :] * c + h_end[k][s:s + 1, :]
            starts.append(cm)
            hfin_ref[0, d:d + 1, k * LANES:(k + 1) * LANES] = c

        def scan_body(t, hs, h_buf=h_buf):
            idx = seg_rows(t)
            nh = []
            for k in range(N_SLABS):
                h = a_buf[k, idx, :] * hs[k] + h_buf[k, idx, :]
                h_buf[k, idx, :] = h
                nh.append(h)
            return tuple(nh)

        lax.fori_loop(0, pitch, scan_body, tuple(starts))

    def out_body(i, carry):
        r0 = pl.multiple_of(i * chunk, chunk)
        for k in range(N_SLABS):
            cols = slice(k * LANES, (k + 1) * LANES)
            h = hf_buf[k, pl.ds(r0, chunk), :] + hb_buf[k, pl.ds(r0, chunk), :]
            y_ref[pl.ds(r0, chunk), cols] = (h * _gelu_tanh(gl_ref[pl.ds(r0, chunk), cols])).astype(BF16)
        return carry

    lax.fori_loop(0, seq // chunk, out_body, 0)


def _lru(xl, gl, h0, conv_w, conv_b, wbd, ba, bi, coef, seq):
    t = xl.shape[0]
    nb = t // seq
    lp = N_SEG * (seq // N_SEG + SEG_PAD)
    tok = pl.BlockSpec((seq, D_LRU), lambda b: (b, 0))
    state = pl.BlockSpec((1, 2, D_LRU), lambda b: (b, 0, 0))
    scan_buf = pltpu.VMEM((N_SLABS, lp, LANES), F32)
    return pl.pallas_call(
        functools.partial(_lru_kernel, seq=seq),
        grid=(nb,),
        in_specs=[tok, tok, state, _const_spec(conv_w.shape), _const_spec(conv_b.shape),
                  _const_spec(wbd.shape), _const_spec(ba.shape), _const_spec(bi.shape),
                  _const_spec(coef.shape)],
        out_specs=[tok, state],
        out_shape=[jax.ShapeDtypeStruct((t, D_LRU), BF16), jax.ShapeDtypeStruct((nb, 2, D_LRU), F32)],
        scratch_shapes=[pltpu.VMEM((seq + 2 * SUBLANES, D_LRU), F32), pltpu.VMEM((seq, D_LRU), F32),
                        scan_buf, scan_buf, scan_buf],
        compiler_params=_params(),
        name="rglru",
    )(xl, gl, h0, conv_w, conv_b, wbd, ba, bi, coef)


def _swiglu_weights(w_in, w_out):
    g = w_in[:, :D_FF].reshape(D_MODEL, N_FF_CHUNKS, FF_CHUNK)
    u = w_in[:, D_FF:].reshape(D_MODEL, N_FF_CHUNKS, FF_CHUNK)
    wgu = jnp.concatenate([g, u], axis=-1).transpose(1, 0, 2).astype(BF16)
    wo = w_out.reshape(N_FF_CHUNKS, FF_CHUNK, D_MODEL).astype(BF16)
    return wgu, wo


def _blockdiag_weights(wa, wi):
    per = LRU_TILE // LRU_BLOCK
    eye = jnp.eye(per, dtype=F32)

    def bd(w):
        w = w.reshape(2, N_LRU_TILES, per, LRU_BLOCK, LRU_BLOCK).astype(F32)
        full = jnp.einsum("dtpij,pq->dtpiqj", w, eye)
        return full.reshape(2, N_LRU_TILES, LRU_TILE, LRU_TILE)

    return jnp.concatenate([bd(wa), bd(wi)], axis=-1).astype(BF16)


def kernel(x_prompt, x_sample, cache_k, cache_v, state_lru, c, c_ctx, w_mod, b_mod, norm_g, ffn1_w_in, ffn1_w_out,
           w_in, rpb, conv_w, conv_b, lru_wa, lru_ba, lru_wi, lru_bi, lru_lambda, w_br_attn, w_br_lru, w_out,
           ffn2_w_in, ffn2_w_out, final_g):
    assert w_mod.shape[0] == 1, "single trunk layer"
    nb_ctx, seq_ctx, _ = x_prompt.shape
    nb_lat, seq_lat, _ = x_sample.shape
    past = cache_k.shape[3]

    n_cond = 1 + nb_lat
    cond = jnp.concatenate([c_ctx[None, :], c, jnp.zeros((2 * SUBLANES - n_cond, D_MODEL), F32)], axis=0)
    mods3 = _mods(cond, w_mod[0], b_mod[0]).reshape(2 * SUBLANES, N_MOD, D_MODEL)

    wgu1, wo1 = _swiglu_weights(ffn1_w_in[0], ffn1_w_out[0])
    wgu2, wo2 = _swiglu_weights(ffn2_w_in[0], ffn2_w_out[0])
    win = w_in[0].astype(BF16)
    wba = w_br_attn[0].astype(BF16)
    wbl = w_br_lru[0].astype(BF16)
    wout = w_out[0].astype(BF16)
    wbd = _blockdiag_weights(lru_wa[0], lru_wi[0])
    coef = -LRU_C * jax.nn.softplus(-lru_lambda[0])
    ng = norm_g[0]
    fg = final_g.reshape(1, D_MODEL)
    cb = conv_b[0].reshape(1, D_LRU)

    ctx_row = lambda tok: 0
    lat_row = lambda tok: 1 + tok // seq_lat

    xp = x_prompt.reshape(nb_ctx * seq_ctx, D_MODEL)
    xs = x_sample.reshape(nb_lat * seq_lat, D_MODEL)

    x1p, qkvp, xlp, glp, gatesp, kvp = _ffn1(xp, mods3, ctx_row, ng, wgu1, wo1, win, 256, True)
    attnp = _ctx_attention(qkvp, seq_ctx)
    lrup, h_fin = _lru(xlp, glp, jnp.zeros((nb_ctx, 2, D_LRU), F32), conv_w[0], cb, wbd,
                       lru_ba[0], lru_bi[0], coef, seq_ctx)
    yp = _ffn2(x1p, attnp, lrup, gatesp, mods3, ctx_row, ng, fg, wba, wbl, wout, wgu2, wo2, 512)

    kc = cache_k[:, 0].transpose(0, 2, 1, 3).reshape(nb_lat * past, D_ATTN).astype(BF16)
    vc = cache_v[:, 0].transpose(0, 2, 1, 3).reshape(nb_lat * past, D_ATTN).astype(BF16)
    bias = _na_bias_table(rpb[0], seq_lat // GRID_W)
    x1s, qkvs, xls, gls, gatess = _ffn1(xs, mods3, lat_row, ng, wgu1, wo1, win, 256, False)
    attns = _na_attention(qkvs, kc, vc, bias, seq_lat, past)
    lrus, _ = _lru(xls, gls, state_lru[:, 0], conv_w[0], cb, wbd, lru_ba[0], lru_bi[0], coef, seq_lat)
    ys = _ffn2(x1s, attns, lrus, gatess, mods3, lat_row, ng, fg, wba, wbl, wout, wgu2, wo2, 512)

    kv = kvp.reshape(nb_ctx, seq_ctx, 2, N_HEADS, HEAD_DIM)
    new_k = kv[:, :, 0].transpose(0, 2, 1, 3)[:, None]
    new_v = kv[:, :, 1].transpose(0, 2, 1, 3)[:, None]
    return (yp.reshape(nb_ctx, seq_ctx, D_MODEL), ys.reshape(nb_lat, seq_lat, D_MODEL),
            new_k, new_v, h_fin[:, None])
```

```python
import functools

import numpy as np
import jax
import jax.numpy as jnp
from jax import lax
from jax.experimental import pallas as pl
from jax.experimental.pallas import tpu as pltpu

F32 = jnp.float32
BF16 = jnp.bfloat16

D_MODEL = 1024
N_HEADS = 8
HEAD_DIM = 64
D_ATTN = N_HEADS * HEAD_DIM
GRID_W = 64
WIN_H = 8
WIN_W = 16
D_LRU = 1024
LRU_BLOCKS = 16
LRU_BLOCK = D_LRU // LRU_BLOCKS
CONV_W = 4
LRU_C = 8.0
D_FF = 2816
N_MOD = 9
EPS = 1e-6
NEG_INF = -1e30
ATTN_SCALE = HEAD_DIM ** -0.5

LANES = 128
SUBLANES = 8
MXU_DIM = 256
VMEM_LIMIT_BYTES = 56 * 1024 * 1024

FF_CHUNK = MXU_DIM
N_FF_CHUNKS = D_FF // FF_CHUNK
LRU_TILE = MXU_DIM
N_LRU_TILES = D_LRU // LRU_TILE
N_SLABS = D_LRU // LANES
N_SEG = SUBLANES
SEG_PAD = 4
ROWS_PER_GROUP = 4


def _sigmoid(x):
    return 0.5 * jnp.tanh(0.5 * x) + 0.5


def _rms_mod(x, g, shift, scale):
    y = x * lax.rsqrt(jnp.mean(x * x, axis=-1, keepdims=True) + EPS)
    y = y * g
    return y * (1 + scale) + shift


def _dot(a, b):
    return jnp.dot(a, b, preferred_element_type=F32)


def _dot_nt(a, b):
    return lax.dot_general(a, b, (((1,), (1,)), ((), ())), preferred_element_type=F32)


def _params(n_axes=1):
    return pltpu.CompilerParams(dimension_semantics=("arbitrary",) * n_axes,
                                vmem_limit_bytes=VMEM_LIMIT_BYTES)


def _const_spec(shape):
    nd = len(shape)
    return pl.BlockSpec(shape, lambda *_: (0,) * nd, pipeline_mode=pl.Buffered(1))


def _mods_kernel(cond_ref, w_ref, b_ref, o_ref):
    c = cond_ref[...]
    s = (c * _sigmoid(c)).astype(BF16)
    o_ref[...] = _dot(s, w_ref[...].astype(BF16)) + b_ref[...]


def _mods(cond, w_mod, b_mod):
    n = w_mod.shape[1]
    bn = D_MODEL
    return pl.pallas_call(
        _mods_kernel,
        grid=(n // bn,),
        in_specs=[pl.BlockSpec(cond.shape, lambda j: (0, 0)),
                  pl.BlockSpec((D_MODEL, bn), lambda j: (0, j)),
                  pl.BlockSpec((1, bn), lambda j: (0, j))],
        out_specs=pl.BlockSpec((cond.shape[0], bn), lambda j: (0, j)),
        out_shape=jax.ShapeDtypeStruct((cond.shape[0], n), F32),
        compiler_params=_params(),
        name="mods",
    )(cond, w_mod, b_mod.reshape(1, n))


def _swiglu(h_ref, a_ref, wgu_ref, wo_ref):
    h = h_ref[...]
    for j in range(N_FF_CHUNKS):
        cols = slice(j * FF_CHUNK, (j + 1) * FF_CHUNK)
        g = _dot(h, wgu_ref[:, cols])
        u = _dot(h, wgu_ref[:, D_FF + j * FF_CHUNK:D_FF + (j + 1) * FF_CHUNK])
        a_ref[:, cols] = ((g * _sigmoid(g)) * u).astype(BF16)
    return _dot(a_ref[...], wo_ref[...])


def _ffn1_kernel(x_ref, mod_ref, ng_ref, wgu_ref, wo_ref, win_ref,
                 x1_ref, qkv_ref, xl_ref, gl_ref, gates_ref, *rest):
    kv_ref = rest[0] if len(rest) == 3 else None
    a_ref, h_ref = rest[-2:]
    x = x_ref[...]
    m = mod_ref[0]
    h_ref[...] = _rms_mod(x, ng_ref[0:1], m[0:1], m[1:2]).astype(BF16)
    x1 = x + (0.5 * m[2:3]) * _swiglu(h_ref, a_ref, wgu_ref, wo_ref)
    x1_ref[...] = x1
    h2 = _rms_mod(x1, ng_ref[1:2], m[3:4], m[4:5]).astype(BF16)
    c0, c1, c2, c3 = 3 * D_ATTN, 3 * D_ATTN + D_LRU, 3 * D_ATTN + 2 * D_LRU, 3 * D_ATTN + 2 * D_LRU + 2 * D_MODEL
    qkv = _dot(h2, win_ref[:, 0:c0])
    qkv_ref[...] = qkv.astype(BF16)
    if kv_ref is not None:
        kv_ref[...] = qkv[:, D_ATTN:]
    xl_ref[...] = _dot(h2, win_ref[:, c0:c1])
    gl_ref[...] = _dot(h2, win_ref[:, c1:c2])
    gates_ref[...] = _dot(h2, win_ref[:, c2:c3])


def _ffn1(x, mods3, mod_row, norm_g, wgu, wo, win, tm, want_kv):
    t = x.shape[0]
    row = lambda w: pl.BlockSpec((tm, w), lambda i: (i, 0))
    out_specs = [row(D_MODEL), row(3 * D_ATTN), row(D_LRU), row(D_LRU), row(2 * D_MODEL)]
    out_shape = [jax.ShapeDtypeStruct((t, D_MODEL), F32),
                 jax.ShapeDtypeStruct((t, 3 * D_ATTN), BF16),
                 jax.ShapeDtypeStruct((t, D_LRU), F32),
                 jax.ShapeDtypeStruct((t, D_LRU), F32),
                 jax.ShapeDtypeStruct((t, 2 * D_MODEL), F32)]
    if want_kv:
        out_specs.append(row(2 * D_ATTN))
        out_shape.append(jax.ShapeDtypeStruct((t, 2 * D_ATTN), F32))
    return pl.pallas_call(
        _ffn1_kernel,
        grid=(t // tm,),
        in_specs=[row(D_MODEL),
                  pl.BlockSpec((1, N_MOD, D_MODEL), lambda i: (mod_row(i * tm), 0, 0)),
                  _const_spec(norm_g.shape), _const_spec(wgu.shape), _const_spec(wo.shape),
                  _const_spec(win.shape)],
        out_specs=out_specs,
        out_shape=out_shape,
        scratch_shapes=[pltpu.VMEM((tm, D_FF), BF16), pltpu.VMEM((tm, D_MODEL), BF16)],
        compiler_params=_params(),
        name="ffn1_inproj",
    )(x, mods3, norm_g, wgu, wo, win)


def _ffn2_kernel(x1_ref, attn_ref, lru_ref, gates_ref, mod_ref, ng_ref, fg_ref,
                 wba_ref, wbl_ref, wout_ref, wgu_ref, wo_ref, y_ref, a_ref, h_ref):
    m = mod_ref[0]
    gt = gates_ref[...]
    mm = (_sigmoid(gt[:, :D_MODEL]) * _dot(attn_ref[...], wba_ref[...])
          + _sigmoid(gt[:, D_MODEL:]) * _dot(lru_ref[...], wbl_ref[...]))
    x2 = x1_ref[...] + m[5:6] * _dot(mm.astype(BF16), wout_ref[...])
    h_ref[...] = _rms_mod(x2, ng_ref[2:3], m[6:7], m[7:8]).astype(BF16)
    x3 = x2 + (0.5 * m[8:9]) * _swiglu(h_ref, a_ref, wgu_ref, wo_ref)
    y = x3 * lax.rsqrt(jnp.mean(x3 * x3, axis=-1, keepdims=True) + EPS)
    y_ref[...] = y * fg_ref[...]


def _ffn2(x1, attn, lru, gates, mods3, mod_row, norm_g, final_g, wba, wbl, wout, wgu, wo, tm):
    t = x1.shape[0]
    row = lambda w: pl.BlockSpec((tm, w), lambda i: (i, 0))
    return pl.pallas_call(
        _ffn2_kernel,
        grid=(t // tm,),
        in_specs=[row(D_MODEL), row(D_ATTN), row(D_LRU), row(2 * D_MODEL),
                  pl.BlockSpec((1, N_MOD, D_MODEL), lambda i: (mod_row(i * tm), 0, 0)),
                  _const_spec(norm_g.shape), _const_spec(final_g.shape),
                  _const_spec(wba.shape), _const_spec(wbl.shape), _const_spec(wout.shape),
                  _const_spec(wgu.shape), _const_spec(wo.shape)],
        out_specs=row(D_MODEL),
        out_shape=jax.ShapeDtypeStruct((t, D_MODEL), F32),
        scratch_shapes=[pltpu.VMEM((tm, D_FF), BF16), pltpu.VMEM((tm, D_MODEL), BF16)],
        compiler_params=_params(),
        name="merge_ffn2",
    )(x1, attn, lru, gates, mods3, norm_g, final_g, wba, wbl, wout, wgu, wo)


def _softmax_pv(scores, values):
    m = scores[0].max(axis=-1, keepdims=True)
    for s in scores[1:]:
        m = jnp.maximum(m, s.max(axis=-1, keepdims=True))
    l = None
    o = None
    for s, v in zip(scores, values):
        p = jnp.exp(s - m)
        ls = p.sum(axis=-1, keepdims=True)
        os_ = _dot(p.astype(BF16), v)
        l = ls if l is None else l + ls
        o = os_ if o is None else o + os_
    return o / l


def _ctx_attn_kernel(qkv_ref, o_ref):
    seq = qkv_ref.shape[0]
    lane = lax.broadcasted_iota(jnp.int32, (seq, LANES), 1)
    first = lane < HEAD_DIM
    for hp in range(D_ATTN // LANES):
        q = qkv_ref[:, hp * LANES:(hp + 1) * LANES]
        k = qkv_ref[:, D_ATTN + hp * LANES:D_ATTN + (hp + 1) * LANES]
        v = qkv_ref[:, 2 * D_ATTN + hp * LANES:2 * D_ATTN + (hp + 1) * LANES]
        outs = []
        for sel in (first, jnp.logical_not(first)):
            qm = jnp.where(sel, q, jnp.zeros_like(q))
            s = _dot_nt(qm, k) * ATTN_SCALE
            outs.append(_softmax_pv([s], [v]))
        o_ref[:, hp * LANES:(hp + 1) * LANES] = jnp.where(first, outs[0], outs[1]).astype(BF16)


def _ctx_attention(qkv, seq):
    t = qkv.shape[0]
    return pl.pallas_call(
        _ctx_attn_kernel,
        grid=(t // seq,),
        in_specs=[pl.BlockSpec((seq, 3 * D_ATTN), lambda b: (b, 0))],
        out_specs=pl.BlockSpec((seq, D_ATTN), lambda b: (b, 0)),
        out_shape=jax.ShapeDtypeStruct((t, D_ATTN), BF16),
        compiler_params=_params(),
        name="ctx_attention",
    )(qkv)


def _na_groups(rows):
    kh = min(WIN_H, rows)
    r = np.arange(rows)
    rs = np.clip(r - kh // 2, 0, rows - kh)
    groups = []
    for g in range(rows // ROWS_PER_GROUP):
        qr = r[g * ROWS_PER_GROUP:(g + 1) * ROWS_PER_GROUP]
        k0, k1 = int(rs[qr].min()), int(rs[qr].max()) + kh
        if (k1 - k0) % 2:
            if k1 < rows:
                k1 += 1
            else:
                k0 -= 1
        groups.append((k0, k1))
    return groups, rs, kh


def _na_rel_rows(rpb):
    edge = GRID_W - WIN_W
    v = jnp.concatenate([rpb[..., WIN_W - 1:],
                         jnp.repeat(rpb[..., -1:], edge, axis=-1),
                         jnp.zeros(rpb.shape[:-1] + (1,), F32),
                         jnp.repeat(rpb[..., :1], edge, axis=-1),
                         rpb[..., :WIN_W - 1]], axis=-1).astype(F32)
    return jnp.pad(v, ((0, 0), (0, 1), (0, 0)))


def _na_build_bias(rel_ref, bias_ref, groups, rs, kh):
    shape = (GRID_W, LANES)
    qc = lax.broadcasted_iota(jnp.int32, shape, 0)
    lane = lax.broadcasted_iota(jnp.int32, shape, 1)
    left = lane < GRID_W
    kc = jnp.where(left, lane, lane - GRID_W)
    cs = jnp.clip(qc - WIN_W // 2, 0, GRID_W - WIN_W)
    in_win = (kc >= cs) & (kc < cs + WIN_W)
    neg = jnp.full(shape, NEG_INF, F32)
    for hh in range(2):
        def tile(r, j, half, hh=hh):
            row = jnp.broadcast_to(rel_ref[hh, j - r + WIN_H - 1:j - r + WIN_H, :], shape)
            return pltpu.roll(row, half * GRID_W, 1, stride=1, stride_axis=0)

        boff = 0
        for g, (k0, k1) in enumerate(groups):
            for i in range(ROWS_PER_GROUP):
                r = g * ROWS_PER_GROUP + i
                for jp in range((k1 - k0) // 2):
                    j = k0 + 2 * jp
                    a_ok = rs[r] <= j < rs[r] + kh
                    b_ok = rs[r] <= j + 1 < rs[r] + kh
                    if a_ok and b_ok:
                        t = jnp.where(in_win, jnp.where(left, tile(r, j, 0), tile(r, j + 1, 1)), neg)
                    elif a_ok:
                        t = jnp.where(in_win & left, tile(r, j, 0), neg)
                    elif b_ok:
                        t = jnp.where(in_win & jnp.logical_not(left), tile(r, j + 1, 1), neg)
                    else:
                        t = neg
                    bias_ref[hh, i * GRID_W:(i + 1) * GRID_W, boff + jp * LANES:boff + (jp + 1) * LANES] = t
            boff += (k1 - k0) * GRID_W


def _na_attn_kernel(q_ref, k_ref, v_ref, kc_ref, vc_ref, rel_ref, o_ref, bias_ref, *, groups, rs, kh):
    @pl.when(pl.program_id(1) == 0)
    def _():
        _na_build_bias(rel_ref, bias_ref, groups, rs, kh)

    gq = ROWS_PER_GROUP * GRID_W
    lane = lax.broadcasted_iota(jnp.int32, (gq, LANES), 1)
    first = lane < HEAD_DIM
    kc = kc_ref[...]
    vc = vc_ref[...]
    for g, (k0, k1) in enumerate(groups):
        q = q_ref[g * gq:(g + 1) * gq, :]
        k = k_ref[k0 * GRID_W:k1 * GRID_W, :]
        v = v_ref[k0 * GRID_W:k1 * GRID_W, :]
        boff = sum((b - a) * GRID_W for a, b in groups[:g])
        outs = []
        for hh, sel in enumerate((first, jnp.logical_not(first))):
            qm = jnp.where(sel, q, jnp.zeros_like(q))
            s_win = _dot_nt(qm, k) * ATTN_SCALE + bias_ref[hh, :, boff:boff + (k1 - k0) * GRID_W]
            s_ctx = _dot_nt(qm, kc) * ATTN_SCALE
            outs.append(_softmax_pv([s_win, s_ctx], [v, vc]))
        o_ref[g * gq:(g + 1) * gq, :] = jnp.where(first, outs[0], outs[1]).astype(BF16)


def _na_attention(qkv, kc, vc, rel, seq, past):
    t = qkv.shape[0]
    nb = t // seq
    npair = D_ATTN // LANES
    groups, rs, kh = _na_groups(seq // GRID_W)
    width = sum(k1 - k0 for k0, k1 in groups) * GRID_W
    tok = lambda off: pl.BlockSpec((seq, LANES), lambda hp, b: (b, off + hp))
    ctx = pl.BlockSpec((past, LANES), lambda hp, b: (b, hp))
    return pl.pallas_call(
        functools.partial(_na_attn_kernel, groups=groups, rs=[int(x) for x in rs], kh=kh),
        grid=(npair, nb),
        in_specs=[tok(0), tok(npair), tok(2 * npair), ctx, ctx,
                  pl.BlockSpec((2,) + rel.shape[1:], lambda hp, b: (hp, 0, 0))],
        out_specs=pl.BlockSpec((seq, LANES), lambda hp, b: (b, hp)),
        out_shape=jax.ShapeDtypeStruct((t, D_ATTN), BF16),
        scratch_shapes=[pltpu.VMEM((2, ROWS_PER_GROUP * GRID_W, width), F32)],
        compiler_params=_params(2),
        name="na_attention",
    )(qkv, qkv, qkv, kc, vc, rel)


def _gelu_tanh(x):
    cdf = 0.5 * (1.0 + jnp.tanh(np.sqrt(2.0 / np.pi).astype(np.float32) * (x + 0.044715 * (x * x * x))))
    return x * cdf


def _lru_kernel(xl_ref, gl_ref, h0_ref, cw_ref, cb_ref, wbd_ref, ba_ref, bi_ref, coef_ref,
                y_ref, hfin_ref, xpad, xc, a_buf, hf_buf, hb_buf, *, seq):
    pitch = seq // N_SEG + SEG_PAD
    lp = N_SEG * pitch
    halo = SUBLANES
    chunk = LANES

    xpad[0:halo, :] = jnp.zeros((halo, D_LRU), F32)
    xpad[halo + seq:2 * halo + seq, :] = jnp.zeros((halo, D_LRU), F32)
    xpad[halo:halo + seq, :] = xl_ref[...]
    left = CONV_W // 2
    for c in range(seq // chunk):
        r0 = c * chunk
        out = cb_ref[...]
        for j in range(CONV_W):
            out = out + xpad[halo + r0 + j - left:halo + r0 + j - left + chunk, :] * cw_ref[j:j + 1, :]
        xc[r0:r0 + chunk, :] = out

    a_buf[:, seq:lp, :] = jnp.ones((N_SLABS, lp - seq, LANES), F32)
    hf_buf[:, seq:lp, :] = jnp.zeros((N_SLABS, lp - seq, LANES), F32)
    hb_buf[:, seq:lp, :] = jnp.zeros((N_SLABS, lp - seq, LANES), F32)

    row = lax.broadcasted_iota(jnp.int32, (N_SEG, LANES), 0)
    for d, h_buf in enumerate((hf_buf, hb_buf)):
        def gates_body(i, carry, d=d, h_buf=h_buf):
            r0 = pl.multiple_of(i * chunk, chunk)
            for c in range(N_LRU_TILES):
                cols = slice(c * LRU_TILE, (c + 1) * LRU_TILE)
                x = xc[pl.ds(r0, chunk), cols]
                pre = _dot(x.astype(BF16), wbd_ref[d, c])
                r = _sigmoid(pre[:, :LRU_TILE] + ba_ref[d:d + 1, cols])
                ig = _sigmoid(pre[:, LRU_TILE:] + bi_ref[d:d + 1, cols])
                log_a = coef_ref[d:d + 1, cols] * r
                a = jnp.exp(log_a)
                b = jnp.sqrt((1.0 - a) * (1.0 + a)) * (ig * x)
                for half in range(LRU_TILE // LANES):
                    k = c * (LRU_TILE // LANES) + half
                    a_buf[k, pl.ds(r0, chunk), :] = a[:, half * LANES:(half + 1) * LANES]
                    h_buf[k, pl.ds(r0, chunk), :] = b[:, half * LANES:(half + 1) * LANES]
            return carry

        lax.fori_loop(0, seq // chunk, gates_body, 0)

        def seg_rows(t, d=d):
            tt = t if d == 0 else pitch - 1 - t
            return pl.ds(tt, N_SEG, stride=pitch)

        def totals_body(t, carry, h_buf=h_buf):
            hs, ps = carry
            idx = seg_rows(t)
            nh, np_ = [], []
            for k in range(N_SLABS):
                a = a_buf[k, idx, :]
                b = h_buf[k, idx, :]
                nh.append(a * hs[k] + b)
                np_.append(ps[k] * a)
            return tuple(nh), tuple(np_)

        zeros = tuple(jnp.zeros((N_SEG, LANES), F32) for _ in range(N_SLABS))
        ones = tuple(jnp.ones((N_SEG, LANES), F32) for _ in range(N_SLABS))
        h_end, p_end = lax.fori_loop(0, pitch, totals_body, (zeros, ones))

        order = range(N_SEG) if d == 0 else range(N_SEG - 1, -1, -1)
        starts = []
        for k in range(N_SLABS):
            c = h0_ref[0, d:d + 1, k * LANES:(k + 1) * LANES]
            cm = jnp.zeros((N_SEG, LANES), F32)
            for s in order:
                cm = jnp.where(row == s, c, cm)
                c = p_end[k][s:s + 1, :] * c + h_end[k][s:s + 1, :]
            starts.append(cm)
            hfin_ref[0, d:d + 1, k * LANES:(k + 1) * LANES] = c

        def scan_body(t, hs, h_buf=h_buf):
            idx = seg_rows(t)
            nh = []
            for k in range(N_SLABS):
                h = a_buf[k, idx, :] * hs[k] + h_buf[k, idx, :]
                h_buf[k, idx, :] = h
                nh.append(h)
            return tuple(nh)

        lax.fori_loop(0, pitch, scan_body, tuple(starts))

    def out_body(i, carry):
        r0 = pl.multiple_of(i * chunk, chunk)
        for k in range(N_SLABS):
            cols = slice(k * LANES, (k + 1) * LANES)
            h = hf_buf[k, pl.ds(r0, chunk), :] + hb_buf[k, pl.ds(r0, chunk), :]
            y_ref[pl.ds(r0, chunk), cols] = (h * _gelu_tanh(gl_ref[pl.ds(r0, chunk), cols])).astype(BF16)
        return carry

    lax.fori_loop(0, seq // chunk, out_body, 0)


def _lru(xl, gl, h0, conv_w, conv_b, wbd, ba, bi, coef, seq):
    t = xl.shape[0]
    nb = t // seq
    lp = N_SEG * (seq // N_SEG + SEG_PAD)
    tok = pl.BlockSpec((seq, D_LRU), lambda b: (b, 0))
    state = pl.BlockSpec((1, 2, D_LRU), lambda b: (b, 0, 0))
    scan_buf = pltpu.VMEM((N_SLABS, lp, LANES), F32)
    return pl.pallas_call(
        functools.partial(_lru_kernel, seq=seq),
        grid=(nb,),
        in_specs=[tok, tok, state, _const_spec(conv_w.shape), _const_spec(conv_b.shape),
                  _const_spec(wbd.shape), _const_spec(ba.shape), _const_spec(bi.shape),
                  _const_spec(coef.shape)],
        out_specs=[tok, state],
        out_shape=[jax.ShapeDtypeStruct((t, D_LRU), BF16), jax.ShapeDtypeStruct((nb, 2, D_LRU), F32)],
        scratch_shapes=[pltpu.VMEM((seq + 2 * SUBLANES, D_LRU), F32), pltpu.VMEM((seq, D_LRU), F32),
                        scan_buf, scan_buf, scan_buf],
        compiler_params=_params(),
        name="rglru",
    )(xl, gl, h0, conv_w, conv_b, wbd, ba, bi, coef)


def _swiglu_weights(w_in, w_out):
    return w_in.astype(BF16), w_out.astype(BF16)


def _blockdiag_weights(wa, wi):
    per = LRU_TILE // LRU_BLOCK
    eye = jnp.eye(per, dtype=F32)

    def bd(w):
        w = w.reshape(2, N_LRU_TILES, per, LRU_BLOCK, LRU_BLOCK).astype(F32)
        full = jnp.einsum("dtpij,pq->dtpiqj", w, eye)
        return full.reshape(2, N_LRU_TILES, LRU_TILE, LRU_TILE)

    return jnp.concatenate([bd(wa), bd(wi)], axis=-1).astype(BF16)


def kernel(x_prompt, x_sample, cache_k, cache_v, state_lru, c, c_ctx, w_mod, b_mod, norm_g, ffn1_w_in, ffn1_w_out,
           w_in, rpb, conv_w, conv_b, lru_wa, lru_ba, lru_wi, lru_bi, lru_lambda, w_br_attn, w_br_lru, w_out,
           ffn2_w_in, ffn2_w_out, final_g):
    assert w_mod.shape[0] == 1, "single trunk layer"
    nb_ctx, seq_ctx, _ = x_prompt.shape
    nb_lat, seq_lat, _ = x_sample.shape
    past = cache_k.shape[3]

    n_cond = 1 + nb_lat
    cond = jnp.concatenate([c_ctx[None, :], c, jnp.zeros((2 * SUBLANES - n_cond, D_MODEL), F32)], axis=0)
    mods3 = _mods(cond, w_mod[0], b_mod[0]).reshape(2 * SUBLANES, N_MOD, D_MODEL)

    wgu1, wo1 = _swiglu_weights(ffn1_w_in[0], ffn1_w_out[0])
    wgu2, wo2 = _swiglu_weights(ffn2_w_in[0], ffn2_w_out[0])
    win = w_in[0].astype(BF16)
    wba = w_br_attn[0].astype(BF16)
    wbl = w_br_lru[0].astype(BF16)
    wout = w_out[0].astype(BF16)
    wbd = _blockdiag_weights(lru_wa[0], lru_wi[0])
    coef = -LRU_C * jax.nn.softplus(-lru_lambda[0])
    ng = norm_g[0]
    fg = final_g.reshape(1, D_MODEL)
    cb = conv_b[0].reshape(1, D_LRU)

    ctx_row = lambda tok: 0
    lat_row = lambda tok: 1 + tok // seq_lat

    xp = x_prompt.reshape(nb_ctx * seq_ctx, D_MODEL)
    xs = x_sample.reshape(nb_lat * seq_lat, D_MODEL)

    x1p, qkvp, xlp, glp, gatesp, kvp = _ffn1(xp, mods3, ctx_row, ng, wgu1, wo1, win, 256, True)
    attnp = _ctx_attention(qkvp, seq_ctx)
    lrup, h_fin = _lru(xlp, glp, jnp.zeros((nb_ctx, 2, D_LRU), F32), conv_w[0], cb, wbd,
                       lru_ba[0], lru_bi[0], coef, seq_ctx)
    yp = _ffn2(x1p, attnp, lrup, gatesp, mods3, ctx_row, ng, fg, wba, wbl, wout, wgu2, wo2, 512)

    kc = cache_k[:, 0].transpose(0, 2, 1, 3).reshape(nb_lat * past, D_ATTN).astype(BF16)
    vc = cache_v[:, 0].transpose(0, 2, 1, 3).reshape(nb_lat * past, D_ATTN).astype(BF16)
    rel = _na_rel_rows(rpb[0])
    x1s, qkvs, xls, gls, gatess = _ffn1(xs, mods3, lat_row, ng, wgu1, wo1, win, 256, False)
    attns = _na_attention(qkvs, kc, vc, rel, seq_lat, past)
    lrus, _ = _lru(xls, gls, state_lru[:, 0], conv_w[0], cb, wbd, lru_ba[0], lru_bi[0], coef, seq_lat)
    ys = _ffn2(x1s, attns, lrus, gatess, mods3, lat_row, ng, fg, wba, wbl, wout, wgu2, wo2, 512)

    kv = kvp.reshape(nb_ctx, seq_ctx, 2, N_HEADS, HEAD_DIM)
    new_k = kv[:, :, 0].transpose(0, 2, 1, 3)[:, None]
    new_v = kv[:, :, 1].transpose(0, 2, 1, 3)[:, None]
    return (yp.reshape(nb_ctx, seq_ctx, D_MODEL), ys.reshape(nb_lat, seq_lat, D_MODEL),
            new_k, new_v, h_fin[:, None])
```

```python
import functools

import numpy as np
import jax
import jax.numpy as jnp
from jax import lax
from jax.experimental import pallas as pl
from jax.experimental.pallas import tpu as pltpu

F32 = jnp.float32
BF16 = jnp.bfloat16

D_MODEL = 1024
N_HEADS = 8
HEAD_DIM = 64
D_ATTN = N_HEADS * HEAD_DIM
GRID_W = 64
WIN_H = 8
WIN_W = 16
D_LRU = 1024
LRU_BLOCKS = 16
LRU_BLOCK = D_LRU // LRU_BLOCKS
CONV_W = 4
LRU_C = 8.0
D_FF = 2816
N_MOD = 9
EPS = 1e-6
NEG_INF = -1e30
ATTN_SCALE = HEAD_DIM ** -0.5

LANES = 128
SUBLANES = 8
MXU_DIM = 256
VMEM_LIMIT_BYTES = 56 * 1024 * 1024

FF_CHUNK = MXU_DIM
N_FF_CHUNKS = D_FF // FF_CHUNK
LRU_TILE = MXU_DIM
N_LRU_TILES = D_LRU // LRU_TILE
N_SLABS = D_LRU // LANES
N_SEG = SUBLANES
SEG_PAD = 4
ROWS_PER_GROUP = 4


def _sigmoid(x):
    return 0.5 * jnp.tanh(0.5 * x) + 0.5


def _rms_mod(x, g, shift, scale):
    y = x * lax.rsqrt(jnp.mean(x * x, axis=-1, keepdims=True) + EPS)
    y = y * g
    return y * (1 + scale) + shift


def _dot(a, b):
    return jnp.dot(a, b, preferred_element_type=F32)


def _dot_nt(a, b):
    return lax.dot_general(a, b, (((1,), (1,)), ((), ())), preferred_element_type=F32)


def _params(n_axes=1):
    return pltpu.CompilerParams(dimension_semantics=("arbitrary",) * n_axes,
                                vmem_limit_bytes=VMEM_LIMIT_BYTES)


def _const_spec(shape):
    nd = len(shape)
    return pl.BlockSpec(shape, lambda *_: (0,) * nd, pipeline_mode=pl.Buffered(1))


def _mods_kernel(cond_ref, w_ref, b_ref, o_ref):
    c = cond_ref[...]
    s = (c * _sigmoid(c)).astype(BF16)
    o_ref[...] = _dot(s, w_ref[...].astype(BF16)) + b_ref[...]


def _mods(cond, w_mod, b_mod):
    n = w_mod.shape[1]
    bn = D_MODEL
    return pl.pallas_call(
        _mods_kernel,
        grid=(n // bn,),
        in_specs=[pl.BlockSpec(cond.shape, lambda j: (0, 0)),
                  pl.BlockSpec((D_MODEL, bn), lambda j: (0, j)),
                  pl.BlockSpec((1, bn), lambda j: (0, j))],
        out_specs=pl.BlockSpec((cond.shape[0], bn), lambda j: (0, j)),
        out_shape=jax.ShapeDtypeStruct((cond.shape[0], n), F32),
        compiler_params=_params(),
        name="mods",
    )(cond, w_mod, b_mod.reshape(1, n))


def _swiglu(h_ref, a_ref, wgu_ref, wo_ref):
    h = h_ref[...]
    for j in range(N_FF_CHUNKS):
        cols = slice(j * FF_CHUNK, (j + 1) * FF_CHUNK)
        g = _dot(h, wgu_ref[:, cols])
        u = _dot(h, wgu_ref[:, D_FF + j * FF_CHUNK:D_FF + (j + 1) * FF_CHUNK])
        a_ref[:, cols] = ((g * _sigmoid(g)) * u).astype(BF16)
    return _dot(a_ref[...], wo_ref[...])


def _two_stream(n_first, tm_rows):
    first = lambda w: pl.BlockSpec((tm_rows, w), lambda i: (jnp.minimum(i, n_first - 1), 0))
    second = lambda w: pl.BlockSpec((tm_rows, w), lambda i: (jnp.maximum(i - n_first, 0), 0))
    return first, second


def _mod_spec(n_ctx_tiles, tm, seq_lat):
    def index(i):
        return (jnp.where(i < n_ctx_tiles, 0, 1 + (jnp.maximum(i - n_ctx_tiles, 0) * tm) // seq_lat), 0, 0)
    return pl.BlockSpec((1, N_MOD, D_MODEL), index)


def _ffn1_kernel(xp_ref, xs_ref, mod_ref, ng_ref, wgu_ref, wo_ref, win_ref,
                 x1_ref, qkv_ref, xl_ref, gl_ref, gates_ref, newk_ref, newv_ref, a_ref, h_ref, *, n_ctx_tiles):
    is_ctx = pl.program_id(0) < n_ctx_tiles
    x = jnp.where(is_ctx, xp_ref[...], xs_ref[...])
    m = mod_ref[0]
    h_ref[...] = _rms_mod(x, ng_ref[0:1], m[0:1], m[1:2]).astype(BF16)
    x1 = x + (0.5 * m[2:3]) * _swiglu(h_ref, a_ref, wgu_ref, wo_ref)
    x1_ref[...] = x1
    h2 = _rms_mod(x1, ng_ref[1:2], m[3:4], m[4:5]).astype(BF16)
    c0, c1, c2, c3 = 3 * D_ATTN, 3 * D_ATTN + D_LRU, 3 * D_ATTN + 2 * D_LRU, 3 * D_ATTN + 2 * D_LRU + 2 * D_MODEL
    qkv = _dot(h2, win_ref[:, 0:c0])
    qkv_ref[...] = qkv.astype(BF16)

    @pl.when(is_ctx)
    def _():
        for h in range(N_HEADS):
            newk_ref[0, 0, h] = qkv[:, D_ATTN + h * HEAD_DIM:D_ATTN + (h + 1) * HEAD_DIM]
            newv_ref[0, 0, h] = qkv[:, 2 * D_ATTN + h * HEAD_DIM:2 * D_ATTN + (h + 1) * HEAD_DIM]

    xl_ref[...] = _dot(h2, win_ref[:, c0:c1])
    gl_ref[...] = _dot(h2, win_ref[:, c1:c2])
    gates_ref[...] = _dot(h2, win_ref[:, c2:c3])


def _ffn1(xp, xs, mods3, norm_g, wgu, wo, win, nb_ctx, seq_ctx, seq_lat):
    tm = seq_ctx
    n_ctx_tiles = xp.shape[0] // tm
    t = xp.shape[0] + xs.shape[0]
    row = lambda w: pl.BlockSpec((tm, w), lambda i: (i, 0))
    first, second = _two_stream(n_ctx_tiles, tm)
    cache_spec = pl.BlockSpec((1, 1, N_HEADS, seq_ctx, HEAD_DIM),
                              lambda i: (jnp.minimum(i, n_ctx_tiles - 1), 0, 0, 0, 0))
    cache_shape = jax.ShapeDtypeStruct((nb_ctx, 1, N_HEADS, seq_ctx, HEAD_DIM), F32)
    return pl.pallas_call(
        functools.partial(_ffn1_kernel, n_ctx_tiles=n_ctx_tiles),
        grid=(t // tm,),
        in_specs=[first(D_MODEL), second(D_MODEL), _mod_spec(n_ctx_tiles, tm, seq_lat),
                  _const_spec(norm_g.shape), _const_spec(wgu.shape), _const_spec(wo.shape),
                  _const_spec(win.shape)],
        out_specs=[row(D_MODEL), row(3 * D_ATTN), row(D_LRU), row(D_LRU), row(2 * D_MODEL),
                   cache_spec, cache_spec],
        out_shape=[jax.ShapeDtypeStruct((t, D_MODEL), F32),
                   jax.ShapeDtypeStruct((t, 3 * D_ATTN), BF16),
                   jax.ShapeDtypeStruct((t, D_LRU), F32),
                   jax.ShapeDtypeStruct((t, D_LRU), F32),
                   jax.ShapeDtypeStruct((t, 2 * D_MODEL), F32),
                   cache_shape, cache_shape],
        scratch_shapes=[pltpu.VMEM((tm, D_FF), BF16), pltpu.VMEM((tm, D_MODEL), BF16)],
        compiler_params=_params(),
        name="ffn1_inproj",
    )(xp, xs, mods3, norm_g, wgu, wo, win)


def _ffn2_kernel(x1_ref, attnp_ref, attns_ref, lrup_ref, lrus_ref, gates_ref, mod_ref, ng_ref, fg_ref,
                 wba_ref, wbl_ref, wout_ref, wgu_ref, wo_ref, yp_ref, ys_ref, a_ref, h_ref, *, n_ctx_tiles):
    is_ctx = pl.program_id(0) < n_ctx_tiles
    attn = jnp.where(is_ctx, attnp_ref[...], attns_ref[...])
    lru = jnp.where(is_ctx, lrup_ref[...], lrus_ref[...])
    m = mod_ref[0]
    gt = gates_ref[...]
    mm = (_sigmoid(gt[:, :D_MODEL]) * _dot(attn, wba_ref[...])
          + _sigmoid(gt[:, D_MODEL:]) * _dot(lru, wbl_ref[...]))
    x2 = x1_ref[...] + m[5:6] * _dot(mm.astype(BF16), wout_ref[...])
    h_ref[...] = _rms_mod(x2, ng_ref[2:3], m[6:7], m[7:8]).astype(BF16)
    x3 = x2 + (0.5 * m[8:9]) * _swiglu(h_ref, a_ref, wgu_ref, wo_ref)
    y = x3 * lax.rsqrt(jnp.mean(x3 * x3, axis=-1, keepdims=True) + EPS)
    y = y * fg_ref[...]

    @pl.when(is_ctx)
    def _():
        yp_ref[...] = y

    @pl.when(jnp.logical_not(is_ctx))
    def _():
        ys_ref[...] = y


def _ffn2(x1, attnp, attns, lrup, lrus, gates, mods3, norm_g, final_g, wba, wbl, wout, wgu, wo, tm, seq_lat):
    t = x1.shape[0]
    t_ctx = attnp.shape[0]
    n_ctx_tiles = t_ctx // tm
    row = lambda w: pl.BlockSpec((tm, w), lambda i: (i, 0))
    first, second = _two_stream(n_ctx_tiles, tm)
    return pl.pallas_call(
        functools.partial(_ffn2_kernel, n_ctx_tiles=n_ctx_tiles),
        grid=(t // tm,),
        in_specs=[row(D_MODEL), first(D_ATTN), second(D_ATTN), first(D_LRU), second(D_LRU), row(2 * D_MODEL),
                  _mod_spec(n_ctx_tiles, tm, seq_lat),
                  _const_spec(norm_g.shape), _const_spec(final_g.shape),
                  _const_spec(wba.shape), _const_spec(wbl.shape), _const_spec(wout.shape),
                  _const_spec(wgu.shape), _const_spec(wo.shape)],
        out_specs=[first(D_MODEL), second(D_MODEL)],
        out_shape=[jax.ShapeDtypeStruct((t_ctx, D_MODEL), F32), jax.ShapeDtypeStruct((t - t_ctx, D_MODEL), F32)],
        scratch_shapes=[pltpu.VMEM((tm, D_FF), BF16), pltpu.VMEM((tm, D_MODEL), BF16)],
        compiler_params=_params(),
        name="merge_ffn2",
    )(x1, attnp, attns, lrup, lrus, gates, mods3, norm_g, final_g, wba, wbl, wout, wgu, wo)


def _softmax_pv(scores, values):
    m = scores[0].max(axis=-1, keepdims=True)
    for s in scores[1:]:
        m = jnp.maximum(m, s.max(axis=-1, keepdims=True))
    l = None
    o = None
    for s, v in zip(scores, values):
        p = jnp.exp(s - m)
        ls = p.sum(axis=-1, keepdims=True)
        os_ = _dot(p.astype(BF16), v)
        l = ls if l is None else l + ls
        o = os_ if o is None else o + os_
    return o / l


def _ctx_attn_kernel(qkv_ref, o_ref):
    seq = qkv_ref.shape[0]
    lane = lax.broadcasted_iota(jnp.int32, (seq, LANES), 1)
    first = lane < HEAD_DIM
    for hp in range(D_ATTN // LANES):
        q = qkv_ref[:, hp * LANES:(hp + 1) * LANES]
        k = qkv_ref[:, D_ATTN + hp * LANES:D_ATTN + (hp + 1) * LANES]
        v = qkv_ref[:, 2 * D_ATTN + hp * LANES:2 * D_ATTN + (hp + 1) * LANES]
        outs = []
        for sel in (first, jnp.logical_not(first)):
            qm = jnp.where(sel, q, jnp.zeros_like(q))
            s = _dot_nt(qm, k) * ATTN_SCALE
            outs.append(_softmax_pv([s], [v]))
        o_ref[:, hp * LANES:(hp + 1) * LANES] = jnp.where(first, outs[0], outs[1]).astype(BF16)


def _ctx_attention(qkv, nb, seq):
    return pl.pallas_call(
        _ctx_attn_kernel,
        grid=(nb,),
        in_specs=[pl.BlockSpec((seq, 3 * D_ATTN), lambda b: (b, 0))],
        out_specs=pl.BlockSpec((seq, D_ATTN), lambda b: (b, 0)),
        out_shape=jax.ShapeDtypeStruct((nb * seq, D_ATTN), BF16),
        compiler_params=_params(),
        name="ctx_attention",
    )(qkv)


def _na_groups(rows):
    kh = min(WIN_H, rows)
    r = np.arange(rows)
    rs = np.clip(r - kh // 2, 0, rows - kh)
    groups = []
    for g in range(rows // ROWS_PER_GROUP):
        qr = r[g * ROWS_PER_GROUP:(g + 1) * ROWS_PER_GROUP]
        k0, k1 = int(rs[qr].min()), int(rs[qr].max()) + kh
        if (k1 - k0) % 2:
            if k1 < rows:
                k1 += 1
            else:
                k0 -= 1
        groups.append((k0, k1))
    return groups, rs, kh


def _na_rel_rows(rpb):
    edge = GRID_W - WIN_W
    v = jnp.concatenate([rpb[..., WIN_W - 1:],
                         jnp.repeat(rpb[..., -1:], edge, axis=-1),
                         jnp.zeros(rpb.shape[:-1] + (1,), F32),
                         jnp.repeat(rpb[..., :1], edge, axis=-1),
                         rpb[..., :WIN_W - 1]], axis=-1).astype(F32)
    return jnp.pad(v, ((0, 0), (0, 1), (0, 0)))


def _na_build_bias(rel_ref, bias_ref, groups, rs, kh):
    shape = (GRID_W, LANES)
    qc = lax.broadcasted_iota(jnp.int32, shape, 0)
    lane = lax.broadcasted_iota(jnp.int32, shape, 1)
    left = lane < GRID_W
    kc = jnp.where(left, lane, lane - GRID_W)
    cs = jnp.clip(qc - WIN_W // 2, 0, GRID_W - WIN_W)
    in_win = (kc >= cs) & (kc < cs + WIN_W)
    neg = jnp.full(shape, NEG_INF, F32)
    for hh in range(2):
        def tile(r, j, half, hh=hh):
            row = jnp.broadcast_to(rel_ref[hh, j - r + WIN_H - 1:j - r + WIN_H, :], shape)
            return pltpu.roll(row, half * GRID_W, 1, stride=1, stride_axis=0)

        boff = 0
        for g, (k0, k1) in enumerate(groups):
            for i in range(ROWS_PER_GROUP):
                r = g * ROWS_PER_GROUP + i
                for jp in range((k1 - k0) // 2):
                    j = k0 + 2 * jp
                    a_ok = rs[r] <= j < rs[r] + kh
                    b_ok = rs[r] <= j + 1 < rs[r] + kh
                    if a_ok and b_ok:
                        t = jnp.where(in_win, jnp.where(left, tile(r, j, 0), tile(r, j + 1, 1)), neg)
                    elif a_ok:
                        t = jnp.where(in_win & left, tile(r, j, 0), neg)
                    elif b_ok:
                        t = jnp.where(in_win & jnp.logical_not(left), tile(r, j + 1, 1), neg)
                    else:
                        t = neg
                    bias_ref[hh, i * GRID_W:(i + 1) * GRID_W, boff + jp * LANES:boff + (jp + 1) * LANES] = t
            boff += (k1 - k0) * GRID_W


def _na_attn_kernel(q_ref, k_ref, v_ref, kc_ref, vc_ref, rel_ref, o_ref, bias_ref, *, groups, rs, kh):
    @pl.when(pl.program_id(1) == 0)
    def _():
        _na_build_bias(rel_ref, bias_ref, groups, rs, kh)

    gq = ROWS_PER_GROUP * GRID_W
    lane = lax.broadcasted_iota(jnp.int32, (gq, LANES), 1)
    first = lane < HEAD_DIM
    kc = kc_ref[...]
    vc = vc_ref[...]
    for g, (k0, k1) in enumerate(groups):
        q = q_ref[g * gq:(g + 1) * gq, :]
        k = k_ref[k0 * GRID_W:k1 * GRID_W, :]
        v = v_ref[k0 * GRID_W:k1 * GRID_W, :]
        boff = sum((b - a) * GRID_W for a, b in groups[:g])
        outs = []
        for hh, sel in enumerate((first, jnp.logical_not(first))):
            qm = jnp.where(sel, q, jnp.zeros_like(q))
            s_win = _dot_nt(qm, k) * ATTN_SCALE + bias_ref[hh, :, boff:boff + (k1 - k0) * GRID_W]
            s_ctx = _dot_nt(qm, kc) * ATTN_SCALE
            outs.append(_softmax_pv([s_win, s_ctx], [v, vc]))
        o_ref[g * gq:(g + 1) * gq, :] = jnp.where(first, outs[0], outs[1]).astype(BF16)


def _na_attention(qkv, kc, vc, rel, first_seq, nb, seq, past):
    t = nb * seq
    npair = D_ATTN // LANES
    groups, rs, kh = _na_groups(seq // GRID_W)
    width = sum(k1 - k0 for k0, k1 in groups) * GRID_W
    tok = lambda off: pl.BlockSpec((seq, LANES), lambda hp, b: (first_seq + b, off + hp))
    ctx = pl.BlockSpec((past, LANES), lambda hp, b: (b, hp))
    return pl.pallas_call(
        functools.partial(_na_attn_kernel, groups=groups, rs=[int(x) for x in rs], kh=kh),
        grid=(npair, nb),
        in_specs=[tok(0), tok(npair), tok(2 * npair), ctx, ctx,
                  pl.BlockSpec((2,) + rel.shape[1:], lambda hp, b: (hp, 0, 0))],
        out_specs=pl.BlockSpec((seq, LANES), lambda hp, b: (b, hp)),
        out_shape=jax.ShapeDtypeStruct((t, D_ATTN), BF16),
        scratch_shapes=[pltpu.VMEM((2, ROWS_PER_GROUP * GRID_W, width), F32)],
        compiler_params=_params(2),
        name="na_attention",
    )(qkv, qkv, qkv, kc, vc, rel)


def _gelu_tanh(x):
    cdf = 0.5 * (1.0 + jnp.tanh(np.sqrt(2.0 / np.pi).astype(np.float32) * (x + 0.044715 * (x * x * x))))
    return x * cdf


def _lru_kernel(xl_ref, gl_ref, h0_ref, cw_ref, cb_ref, wbd_ref, ba_ref, bi_ref, coef_ref,
                y_ref, hfin_ref, xpad, xc, a_buf, hf_buf, hb_buf, *, seq):
    pitch = seq // N_SEG + SEG_PAD
    lp = N_SEG * pitch
    halo = SUBLANES
    chunk = LANES

    xpad[0:halo, :] = jnp.zeros((halo, D_LRU), F32)
    xpad[halo + seq:2 * halo + seq, :] = jnp.zeros((halo, D_LRU), F32)
    xpad[halo:halo + seq, :] = xl_ref[...]
    left = CONV_W // 2
    for c in range(seq // chunk):
        r0 = c * chunk
        out = cb_ref[...]
        for j in range(CONV_W):
            out = out + xpad[halo + r0 + j - left:halo + r0 + j - left + chunk, :] * cw_ref[j:j + 1, :]
        xc[r0:r0 + chunk, :] = out

    a_buf[:, seq:lp, :] = jnp.ones((N_SLABS, lp - seq, LANES), F32)
    hf_buf[:, seq:lp, :] = jnp.zeros((N_SLABS, lp - seq, LANES), F32)
    hb_buf[:, seq:lp, :] = jnp.zeros((N_SLABS, lp - seq, LANES), F32)

    row = lax.broadcasted_iota(jnp.int32, (N_SEG, LANES), 0)
    for d, h_buf in enumerate((hf_buf, hb_buf)):
        def gates_body(i, carry, d=d, h_buf=h_buf):
            r0 = pl.multiple_of(i * chunk, chunk)
            for c in range(N_LRU_TILES):
                cols = slice(c * LRU_TILE, (c + 1) * LRU_TILE)
                x = xc[pl.ds(r0, chunk), cols]
                pre = _dot(x.astype(BF16), wbd_ref[d, c])
                r = _sigmoid(pre[:, :LRU_TILE] + ba_ref[d:d + 1, cols])
                ig = _sigmoid(pre[:, LRU_TILE:] + bi_ref[d:d + 1, cols])
                log_a = coef_ref[d:d + 1, cols] * r
                a = jnp.exp(log_a)
                b = jnp.sqrt((1.0 - a) * (1.0 + a)) * (ig * x)
                for half in range(LRU_TILE // LANES):
                    k = c * (LRU_TILE // LANES) + half
                    a_buf[k, pl.ds(r0, chunk), :] = a[:, half * LANES:(half + 1) * LANES]
                    h_buf[k, pl.ds(r0, chunk), :] = b[:, half * LANES:(half + 1) * LANES]
            return carry

        lax.fori_loop(0, seq // chunk, gates_body, 0)

        def seg_rows(t, d=d):
            tt = t if d == 0 else pitch - 1 - t
            return pl.ds(tt, N_SEG, stride=pitch)

        def totals_body(t, carry, h_buf=h_buf):
            hs, ps = carry
            idx = seg_rows(t)
            nh, np_ = [], []
            for k in range(N_SLABS):
                a = a_buf[k, idx, :]
                b = h_buf[k, idx, :]
                nh.append(a * hs[k] + b)
                np_.append(ps[k] * a)
            return tuple(nh), tuple(np_)

        zeros = tuple(jnp.zeros((N_SEG, LANES), F32) for _ in range(N_SLABS))
        ones = tuple(jnp.ones((N_SEG, LANES), F32) for _ in range(N_SLABS))
        h_end, p_end = lax.fori_loop(0, pitch, totals_body, (zeros, ones))

        order = range(N_SEG) if d == 0 else range(N_SEG - 1, -1, -1)
        starts = []
        for k in range(N_SLABS):
            c = h0_ref[0, d:d + 1, k * LANES:(k + 1) * LANES]
            cm = jnp.zeros((N_SEG, LANES), F32)
            for s in order:
                cm = jnp.where(row == s, c, cm)
                c = p_end[k][s:s + 1, :] * c + h_end[k][s:s + 1, :]
            starts.append(cm)
            hfin_ref[0, d:d + 1, k * LANES:(k + 1) * LANES] = c

        def scan_body(t, hs, h_buf=h_buf):
            idx = seg_rows(t)
            nh = []
            for k in range(N_SLABS):
                h = a_buf[k, idx, :] * hs[k] + h_buf[k, idx, :]
                h_buf[k, idx, :] = h
                nh.append(h)
            return tuple(nh)

        lax.fori_loop(0, pitch, scan_body, tuple(starts))

    def out_body(i, carry):
        r0 = pl.multiple_of(i * chunk, chunk)
        for k in range(N_SLABS):
            cols = slice(k * LANES, (k + 1) * LANES)
            h = hf_buf[k, pl.ds(r0, chunk), :] + hb_buf[k, pl.ds(r0, chunk), :]
            y_ref[pl.ds(r0, chunk), cols] = (h * _gelu_tanh(gl_ref[pl.ds(r0, chunk), cols])).astype(BF16)
        return carry

    lax.fori_loop(0, seq // chunk, out_body, 0)


def _lru(xl, gl, h0, conv_w, conv_b, wbd, ba, bi, coef, first_seq, nb, seq):
    t = nb * seq
    lp = N_SEG * (seq // N_SEG + SEG_PAD)
    tok_in = pl.BlockSpec((seq, D_LRU), lambda b: (first_seq + b, 0))
    tok = pl.BlockSpec((seq, D_LRU), lambda b: (b, 0))
    state = pl.BlockSpec((1, 2, D_LRU), lambda b: (b, 0, 0))
    scan_buf = pltpu.VMEM((N_SLABS, lp, LANES), F32)
    return pl.pallas_call(
        functools.partial(_lru_kernel, seq=seq),
        grid=(nb,),
        in_specs=[tok_in, tok_in, state, _const_spec(conv_w.shape), _const_spec(conv_b.shape),
                  _const_spec(wbd.shape), _const_spec(ba.shape), _const_spec(bi.shape),
                  _const_spec(coef.shape)],
        out_specs=[tok, state],
        out_shape=[jax.ShapeDtypeStruct((t, D_LRU), BF16), jax.ShapeDtypeStruct((nb, 2, D_LRU), F32)],
        scratch_shapes=[pltpu.VMEM((seq + 2 * SUBLANES, D_LRU), F32), pltpu.VMEM((seq, D_LRU), F32),
                        scan_buf, scan_buf, scan_buf],
        compiler_params=_params(),
        name="rglru",
    )(xl, gl, h0, conv_w, conv_b, wbd, ba, bi, coef)


def _swiglu_weights(w_in, w_out):
    return w_in.astype(BF16), w_out.astype(BF16)


def _blockdiag_weights(wa, wi):
    per = LRU_TILE // LRU_BLOCK
    eye = jnp.eye(per, dtype=F32)

    def bd(w):
        w = w.reshape(2, N_LRU_TILES, per, LRU_BLOCK, LRU_BLOCK).astype(F32)
        full = jnp.einsum("dtpij,pq->dtpiqj", w, eye)
        return full.reshape(2, N_LRU_TILES, LRU_TILE, LRU_TILE)

    return jnp.concatenate([bd(wa), bd(wi)], axis=-1).astype(BF16)


def kernel(x_prompt, x_sample, cache_k, cache_v, state_lru, c, c_ctx, w_mod, b_mod, norm_g, ffn1_w_in, ffn1_w_out,
           w_in, rpb, conv_w, conv_b, lru_wa, lru_ba, lru_wi, lru_bi, lru_lambda, w_br_attn, w_br_lru, w_out,
           ffn2_w_in, ffn2_w_out, final_g):
    assert w_mod.shape[0] == 1, "single trunk layer"
    nb_ctx, seq_ctx, _ = x_prompt.shape
    nb_lat, seq_lat, _ = x_sample.shape
    past = cache_k.shape[3]

    n_cond = 1 + nb_lat
    cond = jnp.concatenate([c_ctx[None, :], c, jnp.zeros((2 * SUBLANES - n_cond, D_MODEL), F32)], axis=0)
    mods3 = _mods(cond, w_mod[0], b_mod[0]).reshape(2 * SUBLANES, N_MOD, D_MODEL)

    wgu1, wo1 = _swiglu_weights(ffn1_w_in[0], ffn1_w_out[0])
    wgu2, wo2 = _swiglu_weights(ffn2_w_in[0], ffn2_w_out[0])
    win = w_in[0].astype(BF16)
    wba = w_br_attn[0].astype(BF16)
    wbl = w_br_lru[0].astype(BF16)
    wout = w_out[0].astype(BF16)
    wbd = _blockdiag_weights(lru_wa[0], lru_wi[0])
    coef = -LRU_C * jax.nn.softplus(-lru_lambda[0])
    ng = norm_g[0]
    fg = final_g.reshape(1, D_MODEL)
    cb = conv_b[0].reshape(1, D_LRU)

    t_ctx = nb_ctx * seq_ctx
    assert t_ctx % seq_lat == 0, "latent sequences must start on a latent-sequence-sized row block"
    xp = x_prompt.reshape(t_ctx, D_MODEL)
    xs = x_sample.reshape(nb_lat * seq_lat, D_MODEL)

    x1, qkv, xl, gl, gates, new_k, new_v = _ffn1(xp, xs, mods3, ng, wgu1, wo1, win, nb_ctx, seq_ctx, seq_lat)

    attnp = _ctx_attention(qkv, nb_ctx, seq_ctx)
    lrup, h_fin = _lru(xl, gl, jnp.zeros((nb_ctx, 2, D_LRU), F32), conv_w[0], cb, wbd,
                       lru_ba[0], lru_bi[0], coef, 0, nb_ctx, seq_ctx)

    kc = cache_k[:, 0].transpose(0, 2, 1, 3).reshape(nb_lat * past, D_ATTN).astype(BF16)
    vc = cache_v[:, 0].transpose(0, 2, 1, 3).reshape(nb_lat * past, D_ATTN).astype(BF16)
    rel = _na_rel_rows(rpb[0])
    first_lat = t_ctx // seq_lat
    attns = _na_attention(qkv, kc, vc, rel, first_lat, nb_lat, seq_lat, past)
    lrus, _ = _lru(xl, gl, state_lru[:, 0], conv_w[0], cb, wbd, lru_ba[0], lru_bi[0], coef,
                   first_lat, nb_lat, seq_lat)

    yp, ys = _ffn2(x1, attnp, attns, lrup, lrus, gates, mods3, ng, fg, wba, wbl, wout, wgu2, wo2, 512, seq_lat)
    return (yp.reshape(nb_ctx, seq_ctx, D_MODEL), ys.reshape(nb_lat, seq_lat, D_MODEL),
            new_k, new_v, h_fin[:, None])
```

```python
import functools

import numpy as np
import jax
import jax.numpy as jnp
from jax import lax
from jax.experimental import pallas as pl
from jax.experimental.pallas import tpu as pltpu

F32 = jnp.float32
BF16 = jnp.bfloat16

D_MODEL = 1024
N_HEADS = 8
HEAD_DIM = 64
D_ATTN = N_HEADS * HEAD_DIM
GRID_W = 64
WIN_H = 8
WIN_W = 16
D_LRU = 1024
LRU_BLOCKS = 16
LRU_BLOCK = D_LRU // LRU_BLOCKS
CONV_W = 4
LRU_C = 8.0
D_FF = 2816
N_MOD = 9
EPS = 1e-6
NEG_INF = -1e30
F32_TINY = float(np.finfo(np.float32).tiny)
ATTN_SCALE = HEAD_DIM ** -0.5

LANES = 128
SUBLANES = 8
MXU_DIM = 256
VMEM_LIMIT_BYTES = 56 * 1024 * 1024

FF_CHUNK = MXU_DIM
N_FF_CHUNKS = D_FF // FF_CHUNK
LRU_TILE = MXU_DIM
N_LRU_TILES = D_LRU // LRU_TILE
N_SLABS = D_LRU // LANES
N_SEG = SUBLANES
SEG_PAD = 4
SCAN_UNROLL = 4
ROWS_PER_GROUP = 4


def _sigmoid(x):
    return 0.5 * jnp.tanh(0.5 * x) + 0.5


def _rms_mod(x, g, shift, scale):
    y = x * lax.rsqrt(jnp.mean(x * x, axis=-1, keepdims=True) + EPS)
    y = y * g
    return y * (1 + scale) + shift


def _dot(a, b):
    return jnp.dot(a, b, preferred_element_type=F32)


def _dot_nt(a, b):
    return lax.dot_general(a, b, (((1,), (1,)), ((), ())), preferred_element_type=F32)


def _params(n_axes=1):
    return pltpu.CompilerParams(dimension_semantics=("arbitrary",) * n_axes,
                                vmem_limit_bytes=VMEM_LIMIT_BYTES)


def _const_spec(shape):
    nd = len(shape)
    return pl.BlockSpec(shape, lambda *_: (0,) * nd, pipeline_mode=pl.Buffered(1))


def _mods_kernel(cond_ref, w_ref, b_ref, o_ref):
    c = cond_ref[...]
    s = (c * _sigmoid(c)).astype(BF16)
    o_ref[...] = _dot(s, w_ref[...].astype(BF16)) + b_ref[...]


def _mods(cond, w_mod, b_mod):
    n = w_mod.shape[1]
    bn = D_MODEL
    return pl.pallas_call(
        _mods_kernel,
        grid=(n // bn,),
        in_specs=[pl.BlockSpec(cond.shape, lambda j: (0, 0)),
                  pl.BlockSpec((D_MODEL, bn), lambda j: (0, j)),
                  pl.BlockSpec((1, bn), lambda j: (0, j))],
        out_specs=pl.BlockSpec((cond.shape[0], bn), lambda j: (0, j)),
        out_shape=jax.ShapeDtypeStruct((cond.shape[0], n), F32),
        compiler_params=_params(),
        name="mods",
    )(cond, w_mod, b_mod.reshape(1, n))


def _swiglu(h_ref, a_ref, wgu_ref, wo_ref):
    h = h_ref[...]
    for j in range(N_FF_CHUNKS):
        cols = slice(j * FF_CHUNK, (j + 1) * FF_CHUNK)
        g = _dot(h, wgu_ref[:, cols])
        u = _dot(h, wgu_ref[:, D_FF + j * FF_CHUNK:D_FF + (j + 1) * FF_CHUNK])
        a_ref[:, cols] = ((g * _sigmoid(g)) * u).astype(BF16)
    return _dot(a_ref[...], wo_ref[...])


def _two_stream(n_first, tm_rows):
    first = lambda w: pl.BlockSpec((tm_rows, w), lambda i: (jnp.minimum(i, n_first - 1), 0))
    second = lambda w: pl.BlockSpec((tm_rows, w), lambda i: (jnp.maximum(i - n_first, 0), 0))
    return first, second


def _mod_spec(n_ctx_tiles, tm, seq_lat):
    def index(i):
        return (jnp.where(i < n_ctx_tiles, 0, 1 + (jnp.maximum(i - n_ctx_tiles, 0) * tm) // seq_lat), 0, 0)
    return pl.BlockSpec((1, N_MOD, D_MODEL), index)


def _ffn1_kernel(xp_ref, xs_ref, mod_ref, ng_ref, wgu_ref, wo_ref, win_ref,
                 x1_ref, qkv_ref, xl_ref, gl_ref, gates_ref, newk_ref, newv_ref, a_ref, h_ref, *, n_ctx_tiles):
    is_ctx = pl.program_id(0) < n_ctx_tiles
    x = jnp.where(is_ctx, xp_ref[...], xs_ref[...])
    m = mod_ref[0]
    h_ref[...] = _rms_mod(x, ng_ref[0:1], m[0:1], m[1:2]).astype(BF16)
    x1 = x + (0.5 * m[2:3]) * _swiglu(h_ref, a_ref, wgu_ref, wo_ref)
    x1_ref[...] = x1
    h2 = _rms_mod(x1, ng_ref[1:2], m[3:4], m[4:5]).astype(BF16)
    c0, c1, c2, c3 = 3 * D_ATTN, 3 * D_ATTN + D_LRU, 3 * D_ATTN + 2 * D_LRU, 3 * D_ATTN + 2 * D_LRU + 2 * D_MODEL
    qkv = _dot(h2, win_ref[:, 0:c0])
    qkv_ref[...] = qkv.astype(BF16)

    @pl.when(is_ctx)
    def _():
        for h in range(N_HEADS):
            newk_ref[0, 0, h] = qkv[:, D_ATTN + h * HEAD_DIM:D_ATTN + (h + 1) * HEAD_DIM]
            newv_ref[0, 0, h] = qkv[:, 2 * D_ATTN + h * HEAD_DIM:2 * D_ATTN + (h + 1) * HEAD_DIM]

    xl_ref[...] = _dot(h2, win_ref[:, c0:c1])
    gl_ref[...] = _dot(h2, win_ref[:, c1:c2])
    gates_ref[...] = _dot(h2, win_ref[:, c2:c3])


def _ffn1(xp, xs, mods3, norm_g, wgu, wo, win, nb_ctx, seq_ctx, seq_lat):
    tm = seq_ctx
    n_ctx_tiles = xp.shape[0] // tm
    t = xp.shape[0] + xs.shape[0]
    row = lambda w: pl.BlockSpec((tm, w), lambda i: (i, 0))
    first, second = _two_stream(n_ctx_tiles, tm)
    cache_spec = pl.BlockSpec((1, 1, N_HEADS, seq_ctx, HEAD_DIM),
                              lambda i: (jnp.minimum(i, n_ctx_tiles - 1), 0, 0, 0, 0))
    cache_shape = jax.ShapeDtypeStruct((nb_ctx, 1, N_HEADS, seq_ctx, HEAD_DIM), F32)
    return pl.pallas_call(
        functools.partial(_ffn1_kernel, n_ctx_tiles=n_ctx_tiles),
        grid=(t // tm,),
        in_specs=[first(D_MODEL), second(D_MODEL), _mod_spec(n_ctx_tiles, tm, seq_lat),
                  _const_spec(norm_g.shape), _const_spec(wgu.shape), _const_spec(wo.shape),
                  _const_spec(win.shape)],
        out_specs=[row(D_MODEL), row(3 * D_ATTN), row(D_LRU), row(D_LRU), row(2 * D_MODEL),
                   cache_spec, cache_spec],
        out_shape=[jax.ShapeDtypeStruct((t, D_MODEL), F32),
                   jax.ShapeDtypeStruct((t, 3 * D_ATTN), BF16),
                   jax.ShapeDtypeStruct((t, D_LRU), F32),
                   jax.ShapeDtypeStruct((t, D_LRU), F32),
                   jax.ShapeDtypeStruct((t, 2 * D_MODEL), F32),
                   cache_shape, cache_shape],
        scratch_shapes=[pltpu.VMEM((tm, D_FF), BF16), pltpu.VMEM((tm, D_MODEL), BF16)],
        compiler_params=_params(),
        name="ffn1_inproj",
    )(xp, xs, mods3, norm_g, wgu, wo, win)


def _ffn2_kernel(x1_ref, attnp_ref, attns_ref, lrup_ref, lrus_ref, gates_ref, mod_ref, ng_ref, fg_ref,
                 wba_ref, wbl_ref, wout_ref, wgu_ref, wo_ref, yp_ref, ys_ref, a_ref, h_ref, *, n_ctx_tiles):
    is_ctx = pl.program_id(0) < n_ctx_tiles
    attn = jnp.where(is_ctx, attnp_ref[...], attns_ref[...])
    lru = jnp.where(is_ctx, lrup_ref[...], lrus_ref[...])
    m = mod_ref[0]
    gt = gates_ref[...]
    mm = (_sigmoid(gt[:, :D_MODEL]) * _dot(attn, wba_ref[...])
          + _sigmoid(gt[:, D_MODEL:]) * _dot(lru, wbl_ref[...]))
    x2 = x1_ref[...] + m[5:6] * _dot(mm.astype(BF16), wout_ref[...])
    h_ref[...] = _rms_mod(x2, ng_ref[2:3], m[6:7], m[7:8]).astype(BF16)
    x3 = x2 + (0.5 * m[8:9]) * _swiglu(h_ref, a_ref, wgu_ref, wo_ref)
    y = x3 * lax.rsqrt(jnp.mean(x3 * x3, axis=-1, keepdims=True) + EPS)
    y = y * fg_ref[...]

    @pl.when(is_ctx)
    def _():
        yp_ref[...] = y

    @pl.when(jnp.logical_not(is_ctx))
    def _():
        ys_ref[...] = y


def _ffn2(x1, attnp, attns, lrup, lrus, gates, mods3, norm_g, final_g, wba, wbl, wout, wgu, wo, tm, seq_lat):
    t = x1.shape[0]
    t_ctx = attnp.shape[0]
    n_ctx_tiles = t_ctx // tm
    row = lambda w: pl.BlockSpec((tm, w), lambda i: (i, 0))
    first, second = _two_stream(n_ctx_tiles, tm)
    return pl.pallas_call(
        functools.partial(_ffn2_kernel, n_ctx_tiles=n_ctx_tiles),
        grid=(t // tm,),
        in_specs=[row(D_MODEL), first(D_ATTN), second(D_ATTN), first(D_LRU), second(D_LRU), row(2 * D_MODEL),
                  _mod_spec(n_ctx_tiles, tm, seq_lat),
                  _const_spec(norm_g.shape), _const_spec(final_g.shape),
                  _const_spec(wba.shape), _const_spec(wbl.shape), _const_spec(wout.shape),
                  _const_spec(wgu.shape), _const_spec(wo.shape)],
        out_specs=[first(D_MODEL), second(D_MODEL)],
        out_shape=[jax.ShapeDtypeStruct((t_ctx, D_MODEL), F32), jax.ShapeDtypeStruct((t - t_ctx, D_MODEL), F32)],
        scratch_shapes=[pltpu.VMEM((tm, D_FF), BF16), pltpu.VMEM((tm, D_MODEL), BF16)],
        compiler_params=_params(),
        name="merge_ffn2",
    )(x1, attnp, attns, lrup, lrus, gates, mods3, norm_g, final_g, wba, wbl, wout, wgu, wo)


def _softmax_pv(scores, values):
    m = scores[0].max(axis=-1, keepdims=True)
    for s in scores[1:]:
        m = jnp.maximum(m, s.max(axis=-1, keepdims=True))
    l = None
    o = None
    for s, v in zip(scores, values):
        p = jnp.exp(s - m)
        ls = p.sum(axis=-1, keepdims=True)
        os_ = _dot(p.astype(BF16), v)
        l = ls if l is None else l + ls
        o = os_ if o is None else o + os_
    return o / l


def _ctx_attn_kernel(qkv_ref, o_ref):
    seq = qkv_ref.shape[0]
    lane = lax.broadcasted_iota(jnp.int32, (seq, LANES), 1)
    first = lane < HEAD_DIM
    for hp in range(D_ATTN // LANES):
        q = qkv_ref[:, hp * LANES:(hp + 1) * LANES]
        k = qkv_ref[:, D_ATTN + hp * LANES:D_ATTN + (hp + 1) * LANES]
        v = qkv_ref[:, 2 * D_ATTN + hp * LANES:2 * D_ATTN + (hp + 1) * LANES]
        outs = []
        for sel in (first, jnp.logical_not(first)):
            qm = jnp.where(sel, q, jnp.zeros_like(q))
            s = _dot_nt(qm, k) * ATTN_SCALE
            outs.append(_softmax_pv([s], [v]))
        o_ref[:, hp * LANES:(hp + 1) * LANES] = jnp.where(first, outs[0], outs[1]).astype(BF16)


def _ctx_attention(qkv, nb, seq):
    return pl.pallas_call(
        _ctx_attn_kernel,
        grid=(nb,),
        in_specs=[pl.BlockSpec((seq, 3 * D_ATTN), lambda b: (b, 0))],
        out_specs=pl.BlockSpec((seq, D_ATTN), lambda b: (b, 0)),
        out_shape=jax.ShapeDtypeStruct((nb * seq, D_ATTN), BF16),
        compiler_params=_params(),
        name="ctx_attention",
    )(qkv)


def _na_groups(rows):
    kh = min(WIN_H, rows)
    r = np.arange(rows)
    rs = np.clip(r - kh // 2, 0, rows - kh)
    groups = []
    for g in range(rows // ROWS_PER_GROUP):
        qr = r[g * ROWS_PER_GROUP:(g + 1) * ROWS_PER_GROUP]
        k0, k1 = int(rs[qr].min()), int(rs[qr].max()) + kh
        if (k1 - k0) % 2:
            if k1 < rows:
                k1 += 1
            else:
                k0 -= 1
        groups.append((k0, k1))
    return groups, rs, kh


def _na_rel_rows(rpb):
    edge = GRID_W - WIN_W
    v = jnp.concatenate([rpb[..., WIN_W - 1:],
                         jnp.repeat(rpb[..., -1:], edge, axis=-1),
                         jnp.zeros(rpb.shape[:-1] + (1,), F32),
                         jnp.repeat(rpb[..., :1], edge, axis=-1),
                         rpb[..., :WIN_W - 1]], axis=-1).astype(F32)
    return jnp.pad(v, ((0, 0), (0, 1), (0, 0)))


def _na_build_bias(rel_ref, bias_ref, groups, rs, kh):
    shape = (GRID_W, LANES)
    qc = lax.broadcasted_iota(jnp.int32, shape, 0)
    lane = lax.broadcasted_iota(jnp.int32, shape, 1)
    left = lane < GRID_W
    kc = jnp.where(left, lane, lane - GRID_W)
    cs = jnp.clip(qc - WIN_W // 2, 0, GRID_W - WIN_W)
    in_win = (kc >= cs) & (kc < cs + WIN_W)
    neg = jnp.full(shape, NEG_INF, F32)
    for hh in range(2):
        def tile(r, j, half, hh=hh):
            row = jnp.broadcast_to(rel_ref[hh, j - r + WIN_H - 1:j - r + WIN_H, :], shape)
            return pltpu.roll(row, half * GRID_W, 1, stride=1, stride_axis=0)

        boff = 0
        for g, (k0, k1) in enumerate(groups):
            for i in range(ROWS_PER_GROUP):
                r = g * ROWS_PER_GROUP + i
                for jp in range((k1 - k0) // 2):
                    j = k0 + 2 * jp
                    a_ok = rs[r] <= j < rs[r] + kh
                    b_ok = rs[r] <= j + 1 < rs[r] + kh
                    if a_ok and b_ok:
                        t = jnp.where(in_win, jnp.where(left, tile(r, j, 0), tile(r, j + 1, 1)), neg)
                    elif a_ok:
                        t = jnp.where(in_win & left, tile(r, j, 0), neg)
                    elif b_ok:
                        t = jnp.where(in_win & jnp.logical_not(left), tile(r, j + 1, 1), neg)
                    else:
                        t = neg
                    bias_ref[hh, i * GRID_W:(i + 1) * GRID_W, boff + jp * LANES:boff + (jp + 1) * LANES] = t
            boff += (k1 - k0) * GRID_W


def _na_attn_kernel(q_ref, k_ref, v_ref, kc_ref, vc_ref, rel_ref, o_ref, bias_ref, *, groups, rs, kh):
    @pl.when(pl.program_id(1) == 0)
    def _():
        _na_build_bias(rel_ref, bias_ref, groups, rs, kh)

    gq = ROWS_PER_GROUP * GRID_W
    lane = lax.broadcasted_iota(jnp.int32, (gq, LANES), 1)
    first = lane < HEAD_DIM
    kc = kc_ref[...]
    vc = vc_ref[...]
    for g, (k0, k1) in enumerate(groups):
        q = q_ref[g * gq:(g + 1) * gq, :]
        k = k_ref[k0 * GRID_W:k1 * GRID_W, :]
        v = v_ref[k0 * GRID_W:k1 * GRID_W, :]
        boff = sum((b - a) * GRID_W for a, b in groups[:g])
        outs = []
        for hh, sel in enumerate((first, jnp.logical_not(first))):
            qm = jnp.where(sel, q, jnp.zeros_like(q))
            s_win = _dot_nt(qm, k) * ATTN_SCALE + bias_ref[hh, :, boff:boff + (k1 - k0) * GRID_W]
            s_ctx = _dot_nt(qm, kc) * ATTN_SCALE
            outs.append(_softmax_pv([s_win, s_ctx], [v, vc]))
        o_ref[g * gq:(g + 1) * gq, :] = jnp.where(first, outs[0], outs[1]).astype(BF16)


def _na_attention(qkv, kc, vc, rel, first_seq, nb, seq, past):
    t = nb * seq
    npair = D_ATTN // LANES
    groups, rs, kh = _na_groups(seq // GRID_W)
    width = sum(k1 - k0 for k0, k1 in groups) * GRID_W
    tok = lambda off: pl.BlockSpec((seq, LANES), lambda hp, b: (first_seq + b, off + hp))
    ctx = pl.BlockSpec((past, LANES), lambda hp, b: (b, hp))
    return pl.pallas_call(
        functools.partial(_na_attn_kernel, groups=groups, rs=[int(x) for x in rs], kh=kh),
        grid=(npair, nb),
        in_specs=[tok(0), tok(npair), tok(2 * npair), ctx, ctx,
                  pl.BlockSpec((2,) + rel.shape[1:], lambda hp, b: (hp, 0, 0))],
        out_specs=pl.BlockSpec((seq, LANES), lambda hp, b: (b, hp)),
        out_shape=jax.ShapeDtypeStruct((t, D_ATTN), BF16),
        scratch_shapes=[pltpu.VMEM((2, ROWS_PER_GROUP * GRID_W, width), F32)],
        compiler_params=_params(2),
        name="na_attention",
    )(qkv, qkv, qkv, kc, vc, rel)


def _gelu_tanh(x):
    c = float(np.sqrt(2.0 / np.pi))
    t = jnp.tanh(x * (c + (c * 0.044715) * (x * x)))
    return x * (0.5 * t + 0.5)


def _lru_kernel(xl_ref, gl_ref, h0_ref, cw_ref, cb_ref, wbd_ref, ba_ref, bi_ref, coef_ref,
                y_ref, hfin_ref, xpad, xc, a_buf, hf_buf, hb_buf, *, seq):
    pitch = seq // N_SEG + SEG_PAD
    lp = N_SEG * pitch
    halo = SUBLANES
    chunk = LANES
    n_chunks = seq // chunk
    left = CONV_W // 2

    for k in range(N_SLABS):
        cols = slice(k * LANES, (k + 1) * LANES)
        xpad[k, 0:halo, :] = jnp.zeros((halo, LANES), F32)
        xpad[k, halo + seq:2 * halo + seq, :] = jnp.zeros((halo, LANES), F32)
        xpad[k, halo:halo + seq, :] = xl_ref[:, cols]

    def conv_body(i, carry):
        r0 = pl.multiple_of(i * chunk, chunk)
        for k in range(N_SLABS):
            cols = slice(k * LANES, (k + 1) * LANES)
            out = cb_ref[:, cols]
            for j in range(CONV_W):
                out = out + xpad[k, pl.ds(r0 + halo + j - left, chunk), :] * cw_ref[j:j + 1, cols]
            xc[k, pl.ds(r0, chunk), :] = out
        return carry

    lax.fori_loop(0, n_chunks, conv_body, 0)

    a_buf[:, seq:lp, :] = jnp.ones((N_SLABS, lp - seq, LANES), F32)
    hf_buf[:, seq:lp, :] = jnp.zeros((N_SLABS, lp - seq, LANES), F32)
    hb_buf[:, seq:lp, :] = jnp.zeros((N_SLABS, lp - seq, LANES), F32)

    row = lax.broadcasted_iota(jnp.int32, (N_SEG, LANES), 0)
    for d, h_buf in enumerate((hf_buf, hb_buf)):
        def gates_body(i, carry, d=d, h_buf=h_buf):
            rows = pl.ds(pl.multiple_of(i * chunk, chunk), chunk)
            per = LRU_TILE // LANES
            for c in range(N_LRU_TILES):
                xs = [xc[c * per + half, rows, :] for half in range(per)]
                pre = _dot(jnp.concatenate(xs, axis=1).astype(BF16), wbd_ref[d, c])
                for half in range(per):
                    k = c * per + half
                    cols = slice(k * LANES, (k + 1) * LANES)
                    tr = jnp.tanh(pre[:, half * LANES:(half + 1) * LANES] + ba_ref[d:d + 1, cols])
                    ti = jnp.tanh(pre[:, LRU_TILE + half * LANES:LRU_TILE + (half + 1) * LANES]
                                  + bi_ref[d:d + 1, cols])
                    a = jnp.exp2(coef_ref[d:d + 1, cols] * tr + coef_ref[d:d + 1, cols])
                    y = (1.0 - a) * (1.0 + a)
                    mult = y * lax.rsqrt(jnp.maximum(y, F32_TINY))
                    a_buf[k, rows, :] = a
                    h_buf[k, rows, :] = mult * ((0.5 * ti + 0.5) * xs[half])
            return carry

        lax.fori_loop(0, n_chunks, gates_body, 0)

        def seg_rows(t, d=d):
            tt = t if d == 0 else pitch - 1 - t
            return pl.ds(tt, N_SEG, stride=pitch)

        def totals_body(t, carry, h_buf=h_buf):
            hs, ps = carry
            idx = seg_rows(t)
            nh, np_ = [], []
            for k in range(N_SLABS):
                a = a_buf[k, idx, :]
                b = h_buf[k, idx, :]
                nh.append(a * hs[k] + b)
                np_.append(ps[k] * a)
            return tuple(nh), tuple(np_)

        zeros = tuple(jnp.zeros((N_SEG, LANES), F32) for _ in range(N_SLABS))
        ones = tuple(jnp.ones((N_SEG, LANES), F32) for _ in range(N_SLABS))
        h_end, p_end = lax.fori_loop(0, pitch, totals_body, (zeros, ones), unroll=SCAN_UNROLL)

        order = range(N_SEG) if d == 0 else range(N_SEG - 1, -1, -1)
        starts = []
        for k in range(N_SLABS):
            c = h0_ref[0, d:d + 1, k * LANES:(k + 1) * LANES]
            cm = jnp.zeros((N_SEG, LANES), F32)
            for s in order:
                cm = jnp.where(row == s, c, cm)
                c = p_end[k][s:s + 1, :] * c + h_end[k][s:s + 1, :]
            starts.append(cm)
            hfin_ref[0, d:d + 1, k * LANES:(k + 1) * LANES] = c

        def scan_body(t, hs, h_buf=h_buf):
            idx = seg_rows(t)
            nh = []
            for k in range(N_SLABS):
                h = a_buf[k, idx, :] * hs[k] + h_buf[k, idx, :]
                h_buf[k, idx, :] = h
                nh.append(h)
            return tuple(nh)

        lax.fori_loop(0, pitch, scan_body, tuple(starts), unroll=SCAN_UNROLL)

    def out_body(i, carry):
        r0 = pl.multiple_of(i * chunk, chunk)
        for k in range(N_SLABS):
            cols = slice(k * LANES, (k + 1) * LANES)
            h = hf_buf[k, pl.ds(r0, chunk), :] + hb_buf[k, pl.ds(r0, chunk), :]
            y_ref[pl.ds(r0, chunk), cols] = (h * _gelu_tanh(gl_ref[pl.ds(r0, chunk), cols])).astype(BF16)
        return carry

    lax.fori_loop(0, n_chunks, out_body, 0)


def _lru(xl, gl, h0, conv_w, conv_b, wbd, ba, bi, coef, first_seq, nb, seq):
    t = nb * seq
    lp = N_SEG * (seq // N_SEG + SEG_PAD)
    tok_in = pl.BlockSpec((seq, D_LRU), lambda b: (first_seq + b, 0))
    tok = pl.BlockSpec((seq, D_LRU), lambda b: (b, 0))
    state = pl.BlockSpec((1, 2, D_LRU), lambda b: (b, 0, 0))
    scan_buf = pltpu.VMEM((N_SLABS, lp, LANES), F32)
    return pl.pallas_call(
        functools.partial(_lru_kernel, seq=seq),
        grid=(nb,),
        in_specs=[tok_in, tok_in, state, _const_spec(conv_w.shape), _const_spec(conv_b.shape),
                  _const_spec(wbd.shape), _const_spec(ba.shape), _const_spec(bi.shape),
                  _const_spec(coef.shape)],
        out_specs=[tok, state],
        out_shape=[jax.ShapeDtypeStruct((t, D_LRU), BF16), jax.ShapeDtypeStruct((nb, 2, D_LRU), F32)],
        scratch_shapes=[pltpu.VMEM((N_SLABS, seq + 2 * SUBLANES, LANES), F32),
                        pltpu.VMEM((N_SLABS, seq, LANES), F32), scan_buf, scan_buf, scan_buf],
        compiler_params=_params(),
        name="rglru",
    )(xl, gl, h0, conv_w, conv_b, wbd, ba, bi, coef)


def _swiglu_weights(w_in, w_out):
    return w_in.astype(BF16), w_out.astype(BF16)


def _blockdiag_weights(wa, wi):
    per = LRU_TILE // LRU_BLOCK
    eye = jnp.eye(per, dtype=F32)

    def bd(w):
        w = w.reshape(2, N_LRU_TILES, per, LRU_BLOCK, LRU_BLOCK).astype(F32)
        full = jnp.einsum("dtpij,pq->dtpiqj", w, eye)
        return full.reshape(2, N_LRU_TILES, LRU_TILE, LRU_TILE)

    return (0.5 * jnp.concatenate([bd(wa), bd(wi)], axis=-1)).astype(BF16)


def kernel(x_prompt, x_sample, cache_k, cache_v, state_lru, c, c_ctx, w_mod, b_mod, norm_g, ffn1_w_in, ffn1_w_out,
           w_in, rpb, conv_w, conv_b, lru_wa, lru_ba, lru_wi, lru_bi, lru_lambda, w_br_attn, w_br_lru, w_out,
           ffn2_w_in, ffn2_w_out, final_g):
    assert w_mod.shape[0] == 1, "single trunk layer"
    nb_ctx, seq_ctx, _ = x_prompt.shape
    nb_lat, seq_lat, _ = x_sample.shape
    past = cache_k.shape[3]

    n_cond = 1 + nb_lat
    cond = jnp.concatenate([c_ctx[None, :], c, jnp.zeros((2 * SUBLANES - n_cond, D_MODEL), F32)], axis=0)
    mods3 = _mods(cond, w_mod[0], b_mod[0]).reshape(2 * SUBLANES, N_MOD, D_MODEL)

    wgu1, wo1 = _swiglu_weights(ffn1_w_in[0], ffn1_w_out[0])
    wgu2, wo2 = _swiglu_weights(ffn2_w_in[0], ffn2_w_out[0])
    win = w_in[0].astype(BF16)
    wba = w_br_attn[0].astype(BF16)
    wbl = w_br_lru[0].astype(BF16)
    wout = w_out[0].astype(BF16)
    wbd = _blockdiag_weights(lru_wa[0], lru_wi[0])
    coef = (0.5 * float(np.log2(np.e))) * (-LRU_C * jax.nn.softplus(-lru_lambda[0]))
    ba = 0.5 * lru_ba[0]
    bi = 0.5 * lru_bi[0]
    ng = norm_g[0]
    fg = final_g.reshape(1, D_MODEL)
    cb = conv_b[0].reshape(1, D_LRU)

    t_ctx = nb_ctx * seq_ctx
    assert t_ctx % seq_lat == 0, "latent sequences must start on a latent-sequence-sized row block"
    xp = x_prompt.reshape(t_ctx, D_MODEL)
    xs = x_sample.reshape(nb_lat * seq_lat, D_MODEL)

    x1, qkv, xl, gl, gates, new_k, new_v = _ffn1(xp, xs, mods3, ng, wgu1, wo1, win, nb_ctx, seq_ctx, seq_lat)

    attnp = _ctx_attention(qkv, nb_ctx, seq_ctx)
    lrup, h_fin = _lru(xl, gl, jnp.zeros((nb_ctx, 2, D_LRU), F32), conv_w[0], cb, wbd,
                       ba, bi, coef, 0, nb_ctx, seq_ctx)

    kc = cache_k[:, 0].transpose(0, 2, 1, 3).reshape(nb_lat * past, D_ATTN).astype(BF16)
    vc = cache_v[:, 0].transpose(0, 2, 1, 3).reshape(nb_lat * past, D_ATTN).astype(BF16)
    rel = _na_rel_rows(rpb[0])
    first_lat = t_ctx // seq_lat
    attns = _na_attention(qkv, kc, vc, rel, first_lat, nb_lat, seq_lat, past)
    lrus, _ = _lru(xl, gl, state_lru[:, 0], conv_w[0], cb, wbd, ba, bi, coef,
                   first_lat, nb_lat, seq_lat)

    yp, ys = _ffn2(x1, attnp, attns, lrup, lrus, gates, mods3, ng, fg, wba, wbl, wout, wgu2, wo2, 512, seq_lat)
    return (yp.reshape(nb_ctx, seq_ctx, D_MODEL), ys.reshape(nb_lat, seq_lat, D_MODEL),
            new_k, new_v, h_fin[:, None])
```

```python
import functools

import numpy as np
import jax
import jax.numpy as jnp
from jax import lax
from jax.experimental import pallas as pl
from jax.experimental.pallas import tpu as pltpu

F32 = jnp.float32
BF16 = jnp.bfloat16

D_MODEL = 1024
N_HEADS = 8
HEAD_DIM = 64
D_ATTN = N_HEADS * HEAD_DIM
GRID_W = 64
WIN_H = 8
WIN_W = 16
D_LRU = 1024
LRU_BLOCKS = 16
LRU_BLOCK = D_LRU // LRU_BLOCKS
CONV_W = 4
LRU_C = 8.0
D_FF = 2816
N_MOD = 9
EPS = 1e-6
NEG_INF = -1e30
F32_TINY = float(np.finfo(np.float32).tiny)
ATTN_SCALE = HEAD_DIM ** -0.5

LANES = 128
SUBLANES = 8
MXU_DIM = 256
VMEM_LIMIT_BYTES = 56 * 1024 * 1024

FF_CHUNK = MXU_DIM
N_FF_CHUNKS = D_FF // FF_CHUNK
LRU_TILE = MXU_DIM
N_LRU_TILES = D_LRU // LRU_TILE
N_SLABS = D_LRU // LANES
N_SEG = SUBLANES
SEG_PAD = 4
SCAN_UNROLL = 4
ROWS_PER_GROUP = 4


def _sigmoid(x):
    return 0.5 * jnp.tanh(0.5 * x) + 0.5


def _rms_mod(x, g, shift, scale):
    y = x * lax.rsqrt(jnp.mean(x * x, axis=-1, keepdims=True) + EPS)
    y = y * g
    return y * (1 + scale) + shift


def _dot(a, b):
    return jnp.dot(a, b, preferred_element_type=F32)


def _dot_nt(a, b):
    return lax.dot_general(a, b, (((1,), (1,)), ((), ())), preferred_element_type=F32)


def _params(n_axes=1):
    return pltpu.CompilerParams(dimension_semantics=("arbitrary",) * n_axes,
                                vmem_limit_bytes=VMEM_LIMIT_BYTES)


def _const_spec(shape):
    nd = len(shape)
    return pl.BlockSpec(shape, lambda *_: (0,) * nd, pipeline_mode=pl.Buffered(1))


def _mods_kernel(cond_ref, w_ref, b_ref, o_ref):
    c = cond_ref[...]
    s = (c * _sigmoid(c)).astype(BF16)
    o_ref[...] = _dot(s, w_ref[...].astype(BF16)) + b_ref[...]


def _mods(cond, w_mod, b_mod):
    n = w_mod.shape[1]
    bn = D_MODEL
    return pl.pallas_call(
        _mods_kernel,
        grid=(n // bn,),
        in_specs=[pl.BlockSpec(cond.shape, lambda j: (0, 0)),
                  pl.BlockSpec((D_MODEL, bn), lambda j: (0, j)),
                  pl.BlockSpec((1, bn), lambda j: (0, j))],
        out_specs=pl.BlockSpec((cond.shape[0], bn), lambda j: (0, j)),
        out_shape=jax.ShapeDtypeStruct((cond.shape[0], n), F32),
        compiler_params=_params(),
        name="mods",
    )(cond, w_mod, b_mod.reshape(1, n))


def _swiglu(h_ref, a_ref, wgu_ref, wo_ref):
    h = h_ref[...]
    for j in range(N_FF_CHUNKS):
        cols = slice(j * FF_CHUNK, (j + 1) * FF_CHUNK)
        g = _dot(h, wgu_ref[:, cols])
        u = _dot(h, wgu_ref[:, D_FF + j * FF_CHUNK:D_FF + (j + 1) * FF_CHUNK])
        a_ref[:, cols] = ((g * _sigmoid(g)) * u).astype(BF16)
    return _dot(a_ref[...], wo_ref[...])


def _two_stream(n_first, tm_rows):
    first = lambda w: pl.BlockSpec((tm_rows, w), lambda i: (jnp.minimum(i, n_first - 1), 0))
    second = lambda w: pl.BlockSpec((tm_rows, w), lambda i: (jnp.maximum(i - n_first, 0), 0))
    return first, second


def _mod_spec(n_ctx_tiles, tm, seq_lat):
    def index(i):
        return (jnp.where(i < n_ctx_tiles, 0, 1 + (jnp.maximum(i - n_ctx_tiles, 0) * tm) // seq_lat), 0, 0)
    return pl.BlockSpec((1, N_MOD, D_MODEL), index)


def _ffn1_kernel(xp_ref, xs_ref, mod_ref, ng_ref, wgu_ref, wo_ref, win_ref,
                 x1_ref, qkv_ref, xl_ref, gl_ref, gates_ref, newk_ref, newv_ref, a_ref, h_ref, *, n_ctx_tiles):
    is_ctx = pl.program_id(0) < n_ctx_tiles
    x = jnp.where(is_ctx, xp_ref[...], xs_ref[...])
    m = mod_ref[0]
    h_ref[...] = _rms_mod(x, ng_ref[0:1], m[0:1], m[1:2]).astype(BF16)
    x1 = x + (0.5 * m[2:3]) * _swiglu(h_ref, a_ref, wgu_ref, wo_ref)
    x1_ref[...] = x1
    h2 = _rms_mod(x1, ng_ref[1:2], m[3:4], m[4:5]).astype(BF16)
    c0, c1, c2, c3 = 3 * D_ATTN, 3 * D_ATTN + D_LRU, 3 * D_ATTN + 2 * D_LRU, 3 * D_ATTN + 2 * D_LRU + 2 * D_MODEL
    qkv = _dot(h2, win_ref[:, 0:c0])
    qkv_ref[:, :D_ATTN] = (qkv[:, :D_ATTN] * ATTN_SCALE).astype(BF16)
    qkv_ref[:, D_ATTN:] = qkv[:, D_ATTN:].astype(BF16)

    @pl.when(is_ctx)
    def _():
        for h in range(N_HEADS):
            newk_ref[0, 0, h] = qkv[:, D_ATTN + h * HEAD_DIM:D_ATTN + (h + 1) * HEAD_DIM]
            newv_ref[0, 0, h] = qkv[:, 2 * D_ATTN + h * HEAD_DIM:2 * D_ATTN + (h + 1) * HEAD_DIM]

    xl_ref[...] = _dot(h2, win_ref[:, c0:c1])
    gl_ref[...] = _dot(h2, win_ref[:, c1:c2])
    gates_ref[...] = _dot(h2, win_ref[:, c2:c3])


def _ffn1(xp, xs, mods3, norm_g, wgu, wo, win, nb_ctx, seq_ctx, seq_lat):
    tm = seq_ctx
    n_ctx_tiles = xp.shape[0] // tm
    t = xp.shape[0] + xs.shape[0]
    row = lambda w: pl.BlockSpec((tm, w), lambda i: (i, 0))
    first, second = _two_stream(n_ctx_tiles, tm)
    cache_spec = pl.BlockSpec((1, 1, N_HEADS, seq_ctx, HEAD_DIM),
                              lambda i: (jnp.minimum(i, n_ctx_tiles - 1), 0, 0, 0, 0))
    cache_shape = jax.ShapeDtypeStruct((nb_ctx, 1, N_HEADS, seq_ctx, HEAD_DIM), F32)
    return pl.pallas_call(
        functools.partial(_ffn1_kernel, n_ctx_tiles=n_ctx_tiles),
        grid=(t // tm,),
        in_specs=[first(D_MODEL), second(D_MODEL), _mod_spec(n_ctx_tiles, tm, seq_lat),
                  _const_spec(norm_g.shape), _const_spec(wgu.shape), _const_spec(wo.shape),
                  _const_spec(win.shape)],
        out_specs=[row(D_MODEL), row(3 * D_ATTN), row(D_LRU), row(D_LRU), row(2 * D_MODEL),
                   cache_spec, cache_spec],
        out_shape=[jax.ShapeDtypeStruct((t, D_MODEL), F32),
                   jax.ShapeDtypeStruct((t, 3 * D_ATTN), BF16),
                   jax.ShapeDtypeStruct((t, D_LRU), F32),
                   jax.ShapeDtypeStruct((t, D_LRU), F32),
                   jax.ShapeDtypeStruct((t, 2 * D_MODEL), F32),
                   cache_shape, cache_shape],
        scratch_shapes=[pltpu.VMEM((tm, D_FF), BF16), pltpu.VMEM((tm, D_MODEL), BF16)],
        compiler_params=_params(),
        name="ffn1_inproj",
    )(xp, xs, mods3, norm_g, wgu, wo, win)


def _ffn2_kernel(x1_ref, attnp_ref, attns_ref, lrup_ref, lrus_ref, gates_ref, mod_ref, ng_ref, fg_ref,
                 wba_ref, wbl_ref, wout_ref, wgu_ref, wo_ref, yp_ref, ys_ref, a_ref, h_ref, *, n_ctx_tiles):
    is_ctx = pl.program_id(0) < n_ctx_tiles
    attn = jnp.where(is_ctx, attnp_ref[...], attns_ref[...])
    lru = jnp.where(is_ctx, lrup_ref[...], lrus_ref[...])
    m = mod_ref[0]
    gt = gates_ref[...]
    mm = (_sigmoid(gt[:, :D_MODEL]) * _dot(attn, wba_ref[...])
          + _sigmoid(gt[:, D_MODEL:]) * _dot(lru, wbl_ref[...]))
    x2 = x1_ref[...] + m[5:6] * _dot(mm.astype(BF16), wout_ref[...])
    h_ref[...] = _rms_mod(x2, ng_ref[2:3], m[6:7], m[7:8]).astype(BF16)
    x3 = x2 + (0.5 * m[8:9]) * _swiglu(h_ref, a_ref, wgu_ref, wo_ref)
    y = x3 * lax.rsqrt(jnp.mean(x3 * x3, axis=-1, keepdims=True) + EPS)
    y = y * fg_ref[...]

    @pl.when(is_ctx)
    def _():
        yp_ref[...] = y

    @pl.when(jnp.logical_not(is_ctx))
    def _():
        ys_ref[...] = y


def _ffn2(x1, attnp, attns, lrup, lrus, gates, mods3, norm_g, final_g, wba, wbl, wout, wgu, wo, tm, seq_lat):
    t = x1.shape[0]
    t_ctx = attnp.shape[0]
    n_ctx_tiles = t_ctx // tm
    row = lambda w: pl.BlockSpec((tm, w), lambda i: (i, 0))
    first, second = _two_stream(n_ctx_tiles, tm)
    return pl.pallas_call(
        functools.partial(_ffn2_kernel, n_ctx_tiles=n_ctx_tiles),
        grid=(t // tm,),
        in_specs=[row(D_MODEL), first(D_ATTN), second(D_ATTN), first(D_LRU), second(D_LRU), row(2 * D_MODEL),
                  _mod_spec(n_ctx_tiles, tm, seq_lat),
                  _const_spec(norm_g.shape), _const_spec(final_g.shape),
                  _const_spec(wba.shape), _const_spec(wbl.shape), _const_spec(wout.shape),
                  _const_spec(wgu.shape), _const_spec(wo.shape)],
        out_specs=[first(D_MODEL), second(D_MODEL)],
        out_shape=[jax.ShapeDtypeStruct((t_ctx, D_MODEL), F32), jax.ShapeDtypeStruct((t - t_ctx, D_MODEL), F32)],
        scratch_shapes=[pltpu.VMEM((tm, D_FF), BF16), pltpu.VMEM((tm, D_MODEL), BF16)],
        compiler_params=_params(),
        name="merge_ffn2",
    )(x1, attnp, attns, lrup, lrus, gates, mods3, norm_g, final_g, wba, wbl, wout, wgu, wo)


def _softmax_pv(scores, values):
    m = scores[0].max(axis=-1, keepdims=True)
    for s in scores[1:]:
        m = jnp.maximum(m, s.max(axis=-1, keepdims=True))
    l = None
    o = None
    for s, v in zip(scores, values):
        p = jnp.exp(s - m)
        ls = p.sum(axis=-1, keepdims=True)
        os_ = _dot(p.astype(BF16), v)
        l = ls if l is None else l + ls
        o = os_ if o is None else o + os_
    return o / l


def _ctx_attn_kernel(qkv_ref, o_ref):
    seq = qkv_ref.shape[0]
    lane = lax.broadcasted_iota(jnp.int32, (seq, LANES), 1)
    first = lane < HEAD_DIM
    for hp in range(D_ATTN // LANES):
        q = qkv_ref[:, hp * LANES:(hp + 1) * LANES]
        k = qkv_ref[:, D_ATTN + hp * LANES:D_ATTN + (hp + 1) * LANES]
        v = qkv_ref[:, 2 * D_ATTN + hp * LANES:2 * D_ATTN + (hp + 1) * LANES]
        outs = []
        for sel in (first, jnp.logical_not(first)):
            qm = jnp.where(sel, q, jnp.zeros_like(q))
            outs.append(_softmax_pv([_dot_nt(qm, k)], [v]))
        o_ref[:, hp * LANES:(hp + 1) * LANES] = jnp.where(first, outs[0], outs[1]).astype(BF16)


def _ctx_attention(qkv, nb, seq):
    return pl.pallas_call(
        _ctx_attn_kernel,
        grid=(nb,),
        in_specs=[pl.BlockSpec((seq, 3 * D_ATTN), lambda b: (b, 0))],
        out_specs=pl.BlockSpec((seq, D_ATTN), lambda b: (b, 0)),
        out_shape=jax.ShapeDtypeStruct((nb * seq, D_ATTN), BF16),
        compiler_params=_params(),
        name="ctx_attention",
    )(qkv)


def _na_groups(rows):
    kh = min(WIN_H, rows)
    r = np.arange(rows)
    rs = np.clip(r - kh // 2, 0, rows - kh)
    groups = []
    for g in range(rows // ROWS_PER_GROUP):
        qr = r[g * ROWS_PER_GROUP:(g + 1) * ROWS_PER_GROUP]
        k0, k1 = int(rs[qr].min()), int(rs[qr].max()) + kh
        if (k1 - k0) % 2:
            if k1 < rows:
                k1 += 1
            else:
                k0 -= 1
        groups.append((k0, k1))
    return groups, rs, kh


def _na_rel_rows(rpb):
    edge = GRID_W - WIN_W
    v = jnp.concatenate([rpb[..., WIN_W - 1:],
                         jnp.repeat(rpb[..., -1:], edge, axis=-1),
                         jnp.zeros(rpb.shape[:-1] + (1,), F32),
                         jnp.repeat(rpb[..., :1], edge, axis=-1),
                         rpb[..., :WIN_W - 1]], axis=-1).astype(F32)
    return jnp.pad(v, ((0, 0), (0, 1), (0, 0)))


def _na_build_bias(rel_ref, bias_ref, groups, rs, kh):
    shape = (GRID_W, LANES)
    qc = lax.broadcasted_iota(jnp.int32, shape, 0)
    lane = lax.broadcasted_iota(jnp.int32, shape, 1)
    left = lane < GRID_W
    kc = jnp.where(left, lane, lane - GRID_W)
    cs = jnp.clip(qc - WIN_W // 2, 0, GRID_W - WIN_W)
    in_win = (kc >= cs) & (kc < cs + WIN_W)
    neg = jnp.full(shape, NEG_INF, F32)
    for hh in range(2):
        def tile(r, j, half, hh=hh):
            row = jnp.broadcast_to(rel_ref[hh, j - r + WIN_H - 1:j - r + WIN_H, :], shape)
            return pltpu.roll(row, half * GRID_W, 1, stride=1, stride_axis=0)

        boff = 0
        for g, (k0, k1) in enumerate(groups):
            for i in range(ROWS_PER_GROUP):
                r = g * ROWS_PER_GROUP + i
                for jp in range((k1 - k0) // 2):
                    j = k0 + 2 * jp
                    a_ok = rs[r] <= j < rs[r] + kh
                    b_ok = rs[r] <= j + 1 < rs[r] + kh
                    if a_ok and b_ok:
                        t = jnp.where(in_win, jnp.where(left, tile(r, j, 0), tile(r, j + 1, 1)), neg)
                    elif a_ok:
                        t = jnp.where(in_win & left, tile(r, j, 0), neg)
                    elif b_ok:
                        t = jnp.where(in_win & jnp.logical_not(left), tile(r, j + 1, 1), neg)
                    else:
                        t = neg
                    bias_ref[hh, i * GRID_W:(i + 1) * GRID_W, boff + jp * LANES:boff + (jp + 1) * LANES] = t
            boff += (k1 - k0) * GRID_W


def _na_attn_kernel(q_ref, k_ref, v_ref, kc_ref, vc_ref, rel_ref, o_ref, bias_ref, *, groups, rs, kh):
    @pl.when(pl.program_id(1) == 0)
    def _():
        _na_build_bias(rel_ref, bias_ref, groups, rs, kh)

    gq = ROWS_PER_GROUP * GRID_W
    lane = lax.broadcasted_iota(jnp.int32, (gq, LANES), 1)
    first = lane < HEAD_DIM

    def pair(ref):
        return jnp.concatenate([ref[0, 0, 0], ref[0, 0, 1]], axis=1).astype(BF16)

    kc = pair(kc_ref)
    vc = pair(vc_ref)
    for g, (k0, k1) in enumerate(groups):
        q = q_ref[g * gq:(g + 1) * gq, :]
        k = k_ref[k0 * GRID_W:k1 * GRID_W, :]
        v = v_ref[k0 * GRID_W:k1 * GRID_W, :]
        boff = sum((b - a) * GRID_W for a, b in groups[:g])
        outs = []
        for hh, sel in enumerate((first, jnp.logical_not(first))):
            qm = jnp.where(sel, q, jnp.zeros_like(q))
            s_win = _dot_nt(qm, k) + bias_ref[hh, :, boff:boff + (k1 - k0) * GRID_W]
            outs.append(_softmax_pv([s_win, _dot_nt(qm, kc)], [v, vc]))
        o_ref[g * gq:(g + 1) * gq, :] = jnp.where(first, outs[0], outs[1]).astype(BF16)


def _na_attention(qkv, kc, vc, rel, first_seq, nb, seq):
    t = nb * seq
    past = kc.shape[3]
    npair = D_ATTN // LANES
    groups, rs, kh = _na_groups(seq // GRID_W)
    width = sum(k1 - k0 for k0, k1 in groups) * GRID_W
    tok = lambda off: pl.BlockSpec((seq, LANES), lambda hp, b: (first_seq + b, off + hp))
    ctx = pl.BlockSpec((1, 1, 2, past, HEAD_DIM), lambda hp, b: (b, 0, hp, 0, 0))
    return pl.pallas_call(
        functools.partial(_na_attn_kernel, groups=groups, rs=[int(x) for x in rs], kh=kh),
        grid=(npair, nb),
        in_specs=[tok(0), tok(npair), tok(2 * npair), ctx, ctx,
                  pl.BlockSpec((2,) + rel.shape[1:], lambda hp, b: (hp, 0, 0))],
        out_specs=pl.BlockSpec((seq, LANES), lambda hp, b: (b, hp)),
        out_shape=jax.ShapeDtypeStruct((t, D_ATTN), BF16),
        scratch_shapes=[pltpu.VMEM((2, ROWS_PER_GROUP * GRID_W, width), F32)],
        compiler_params=_params(2),
        name="na_attention",
    )(qkv, qkv, qkv, kc, vc, rel)


def _gelu_tanh(x):
    c = float(np.sqrt(2.0 / np.pi))
    t = jnp.tanh(x * (c + (c * 0.044715) * (x * x)))
    return x * (0.5 * t + 0.5)


def _lru_kernel(xl_ref, gl_ref, h0_ref, cw_ref, cb_ref, wbd_ref, ba_ref, bi_ref, coef_ref,
                y_ref, hfin_ref, xpad, xc, a_buf, hf_buf, hb_buf, *, seq):
    pitch = seq // N_SEG + SEG_PAD
    lp = N_SEG * pitch
    halo = SUBLANES
    chunk = LANES
    n_chunks = seq // chunk
    left = CONV_W // 2

    for k in range(N_SLABS):
        cols = slice(k * LANES, (k + 1) * LANES)
        xpad[k, 0:halo, :] = jnp.zeros((halo, LANES), F32)
        xpad[k, halo + seq:2 * halo + seq, :] = jnp.zeros((halo, LANES), F32)
        xpad[k, halo:halo + seq, :] = xl_ref[:, cols]

    def conv_body(i, carry):
        r0 = pl.multiple_of(i * chunk, chunk)
        for k in range(N_SLABS):
            cols = slice(k * LANES, (k + 1) * LANES)
            out = cb_ref[:, cols]
            for j in range(CONV_W):
                out = out + xpad[k, pl.ds(r0 + halo + j - left, chunk), :] * cw_ref[j:j + 1, cols]
            xc[k, pl.ds(r0, chunk), :] = out
        return carry

    lax.fori_loop(0, n_chunks, conv_body, 0)

    a_buf[:, seq:lp, :] = jnp.ones((N_SLABS, lp - seq, LANES), F32)
    hf_buf[:, seq:lp, :] = jnp.zeros((N_SLABS, lp - seq, LANES), F32)
    hb_buf[:, seq:lp, :] = jnp.zeros((N_SLABS, lp - seq, LANES), F32)

    row = lax.broadcasted_iota(jnp.int32, (N_SEG, LANES), 0)
    for d, h_buf in enumerate((hf_buf, hb_buf)):
        def gates_body(i, carry, d=d, h_buf=h_buf):
            rows = pl.ds(pl.multiple_of(i * chunk, chunk), chunk)
            per = LRU_TILE // LANES
            for c in range(N_LRU_TILES):
                xs = [xc[c * per + half, rows, :] for half in range(per)]
                pre = _dot(jnp.concatenate(xs, axis=1).astype(BF16), wbd_ref[d, c])
                for half in range(per):
                    k = c * per + half
                    cols = slice(k * LANES, (k + 1) * LANES)
                    tr = jnp.tanh(pre[:, half * LANES:(half + 1) * LANES] + ba_ref[d:d + 1, cols])
                    ti = jnp.tanh(pre[:, LRU_TILE + half * LANES:LRU_TILE + (half + 1) * LANES]
                                  + bi_ref[d:d + 1, cols])
                    a = jnp.exp2(coef_ref[d:d + 1, cols] * tr + coef_ref[d:d + 1, cols])
                    y = (1.0 - a) * (1.0 + a)
                    mult = y * lax.rsqrt(jnp.maximum(y, F32_TINY))
                    a_buf[k, rows, :] = a
                    h_buf[k, rows, :] = mult * ((0.5 * ti + 0.5) * xs[half])
            return carry

        lax.fori_loop(0, n_chunks, gates_body, 0)

        def seg_rows(t, d=d):
            tt = t if d == 0 else pitch - 1 - t
            return pl.ds(tt, N_SEG, stride=pitch)

        def totals_body(t, carry, h_buf=h_buf):
            hs, ps = carry
            idx = seg_rows(t)
            nh, np_ = [], []
            for k in range(N_SLABS):
                a = a_buf[k, idx, :]
                b = h_buf[k, idx, :]
                nh.append(a * hs[k] + b)
                np_.append(ps[k] * a)
            return tuple(nh), tuple(np_)

        zeros = tuple(jnp.zeros((N_SEG, LANES), F32) for _ in range(N_SLABS))
        ones = tuple(jnp.ones((N_SEG, LANES), F32) for _ in range(N_SLABS))
        h_end, p_end = lax.fori_loop(0, pitch, totals_body, (zeros, ones), unroll=SCAN_UNROLL)

        order = range(N_SEG) if d == 0 else range(N_SEG - 1, -1, -1)
        starts = []
        for k in range(N_SLABS):
            c = h0_ref[0, d:d + 1, k * LANES:(k + 1) * LANES]
            cm = jnp.zeros((N_SEG, LANES), F32)
            for s in order:
                cm = jnp.where(row == s, c, cm)
                c = p_end[k][s:s + 1, :] * c + h_end[k][s:s + 1, :]
            starts.append(cm)
            hfin_ref[0, d:d + 1, k * LANES:(k + 1) * LANES] = c

        def scan_body(t, hs, h_buf=h_buf):
            idx = seg_rows(t)
            nh = []
            for k in range(N_SLABS):
                h = a_buf[k, idx, :] * hs[k] + h_buf[k, idx, :]
                h_buf[k, idx, :] = h
                nh.append(h)
            return tuple(nh)

        lax.fori_loop(0, pitch, scan_body, tuple(starts), unroll=SCAN_UNROLL)

    def out_body(i, carry):
        r0 = pl.multiple_of(i * chunk, chunk)
        for k in range(N_SLABS):
            cols = slice(k * LANES, (k + 1) * LANES)
            h = hf_buf[k, pl.ds(r0, chunk), :] + hb_buf[k, pl.ds(r0, chunk), :]
            y_ref[pl.ds(r0, chunk), cols] = (h * _gelu_tanh(gl_ref[pl.ds(r0, chunk), cols])).astype(BF16)
        return carry

    lax.fori_loop(0, n_chunks, out_body, 0)


def _lru(xl, gl, h0, conv_w, conv_b, wbd, ba, bi, coef, first_seq, nb, seq):
    t = nb * seq
    lp = N_SEG * (seq // N_SEG + SEG_PAD)
    tok_in = pl.BlockSpec((seq, D_LRU), lambda b: (first_seq + b, 0))
    tok = pl.BlockSpec((seq, D_LRU), lambda b: (b, 0))
    state = pl.BlockSpec((1, 2, D_LRU), lambda b: (b, 0, 0))
    scan_buf = pltpu.VMEM((N_SLABS, lp, LANES), F32)
    return pl.pallas_call(
        functools.partial(_lru_kernel, seq=seq),
        grid=(nb,),
        in_specs=[tok_in, tok_in, state, _const_spec(conv_w.shape), _const_spec(conv_b.shape),
                  _const_spec(wbd.shape), _const_spec(ba.shape), _const_spec(bi.shape),
                  _const_spec(coef.shape)],
        out_specs=[tok, state],
        out_shape=[jax.ShapeDtypeStruct((t, D_LRU), BF16), jax.ShapeDtypeStruct((nb, 2, D_LRU), F32)],
        scratch_shapes=[pltpu.VMEM((N_SLABS, seq + 2 * SUBLANES, LANES), F32),
                        pltpu.VMEM((N_SLABS, seq, LANES), F32), scan_buf, scan_buf, scan_buf],
        compiler_params=_params(),
        name="rglru",
    )(xl, gl, h0, conv_w, conv_b, wbd, ba, bi, coef)


def _swiglu_weights(w_in, w_out):
    return w_in.astype(BF16), w_out.astype(BF16)


def _blockdiag_weights(wa, wi):
    per = LRU_TILE // LRU_BLOCK
    out = jnp.zeros((2, N_LRU_TILES, LRU_TILE, 2 * LRU_TILE), BF16)
    for half, w in enumerate((wa, wi)):
        w = (0.5 * w).astype(BF16).reshape(2, N_LRU_TILES, per, LRU_BLOCK, LRU_BLOCK)
        for p in range(per):
            c0 = half * LRU_TILE + p * LRU_BLOCK
            out = out.at[:, :, p * LRU_BLOCK:(p + 1) * LRU_BLOCK, c0:c0 + LRU_BLOCK].set(w[:, :, p])
    return out


def kernel(x_prompt, x_sample, cache_k, cache_v, state_lru, c, c_ctx, w_mod, b_mod, norm_g, ffn1_w_in, ffn1_w_out,
           w_in, rpb, conv_w, conv_b, lru_wa, lru_ba, lru_wi, lru_bi, lru_lambda, w_br_attn, w_br_lru, w_out,
           ffn2_w_in, ffn2_w_out, final_g):
    assert w_mod.shape[0] == 1, "single trunk layer"
    nb_ctx, seq_ctx, _ = x_prompt.shape
    nb_lat, seq_lat, _ = x_sample.shape
    past = cache_k.shape[3]

    n_cond = 1 + nb_lat
    cond = jnp.concatenate([c_ctx[None, :], c, jnp.zeros((2 * SUBLANES - n_cond, D_MODEL), F32)], axis=0)
    mods3 = _mods(cond, w_mod[0], b_mod[0]).reshape(2 * SUBLANES, N_MOD, D_MODEL)

    wgu1, wo1 = _swiglu_weights(ffn1_w_in[0], ffn1_w_out[0])
    wgu2, wo2 = _swiglu_weights(ffn2_w_in[0], ffn2_w_out[0])
    win = w_in[0].astype(BF16)
    wba = w_br_attn[0].astype(BF16)
    wbl = w_br_lru[0].astype(BF16)
    wout = w_out[0].astype(BF16)
    wbd = _blockdiag_weights(lru_wa[0], lru_wi[0])
    coef = (0.5 * float(np.log2(np.e))) * (-LRU_C * jax.nn.softplus(-lru_lambda[0]))
    ba = 0.5 * lru_ba[0]
    bi = 0.5 * lru_bi[0]
    ng = norm_g[0]
    fg = final_g.reshape(1, D_MODEL)
    cb = conv_b[0].reshape(1, D_LRU)

    t_ctx = nb_ctx * seq_ctx
    assert t_ctx % seq_lat == 0, "latent sequences must start on a latent-sequence-sized row block"
    xp = x_prompt.reshape(t_ctx, D_MODEL)
    xs = x_sample.reshape(nb_lat * seq_lat, D_MODEL)

    x1, qkv, xl, gl, gates, new_k, new_v = _ffn1(xp, xs, mods3, ng, wgu1, wo1, win, nb_ctx, seq_ctx, seq_lat)

    attnp = _ctx_attention(qkv, nb_ctx, seq_ctx)
    lrup, h_fin = _lru(xl, gl, jnp.zeros((nb_ctx, 2, D_LRU), F32), conv_w[0], cb, wbd,
                       ba, bi, coef, 0, nb_ctx, seq_ctx)

    rel = _na_rel_rows(rpb[0])
    first_lat = t_ctx // seq_lat
    attns = _na_attention(qkv, cache_k, cache_v, rel, first_lat, nb_lat, seq_lat)
    lrus, _ = _lru(xl, gl, state_lru[:, 0], conv_w[0], cb, wbd, ba, bi, coef,
                   first_lat, nb_lat, seq_lat)

    yp, ys = _ffn2(x1, attnp, attns, lrup, lrus, gates, mods3, ng, fg, wba, wbl, wout, wgu2, wo2, 512, seq_lat)
    return (yp.reshape(nb_ctx, seq_ctx, D_MODEL), ys.reshape(nb_lat, seq_lat, D_MODEL),
            new_k, new_v, h_fin[:, None])
```

```python
import functools

import numpy as np
import jax
import jax.numpy as jnp
from jax import lax
from jax.experimental import pallas as pl
from jax.experimental.pallas import tpu as pltpu

F32 = jnp.float32
BF16 = jnp.bfloat16

D_MODEL = 1024
N_HEADS = 8
HEAD_DIM = 64
D_ATTN = N_HEADS * HEAD_DIM
GRID_W = 64
WIN_H = 8
WIN_W = 16
D_LRU = 1024
LRU_BLOCKS = 16
LRU_BLOCK = D_LRU // LRU_BLOCKS
CONV_W = 4
LRU_C = 8.0
D_FF = 2816
N_MOD = 9
EPS = 1e-6
NEG_INF = -1e30
F32_TINY = float(np.finfo(np.float32).tiny)
ATTN_SCALE = HEAD_DIM ** -0.5

LANES = 128
SUBLANES = 8
MXU_DIM = 256
VMEM_LIMIT_BYTES = 56 * 1024 * 1024

FF_CHUNK = MXU_DIM
N_FF_CHUNKS = D_FF // FF_CHUNK
LRU_TILE = MXU_DIM
N_LRU_TILES = D_LRU // LRU_TILE
N_SLABS = D_LRU // LANES
N_SEG = SUBLANES
SEG_PAD = 4
SCAN_UNROLL = 4
ROWS_PER_GROUP = 4


def _sigmoid(x):
    return 0.5 * jnp.tanh(0.5 * x) + 0.5


def _rms_mod(x, g, shift, scale):
    y = x * lax.rsqrt(jnp.mean(x * x, axis=-1, keepdims=True) + EPS)
    y = y * g
    return y * (1 + scale) + shift


def _dot(a, b):
    return jnp.dot(a, b, preferred_element_type=F32)


def _dot_nt(a, b):
    return lax.dot_general(a, b, (((1,), (1,)), ((), ())), preferred_element_type=F32)


def _params(n_axes=1):
    return pltpu.CompilerParams(dimension_semantics=("arbitrary",) * n_axes,
                                vmem_limit_bytes=VMEM_LIMIT_BYTES)


def _const_spec(shape):
    nd = len(shape)
    return pl.BlockSpec(shape, lambda *_: (0,) * nd, pipeline_mode=pl.Buffered(1))


def _mods_kernel(cond_ref, w_ref, b_ref, o_ref):
    c = cond_ref[...]
    s = (c * _sigmoid(c)).astype(BF16)
    o_ref[...] = _dot(s, w_ref[...].astype(BF16)) + b_ref[...]


def _mods(cond, w_mod, b_mod):
    n = w_mod.shape[1]
    bn = D_MODEL
    return pl.pallas_call(
        _mods_kernel,
        grid=(n // bn,),
        in_specs=[pl.BlockSpec(cond.shape, lambda j: (0, 0)),
                  pl.BlockSpec((D_MODEL, bn), lambda j: (0, j)),
                  pl.BlockSpec((1, bn), lambda j: (0, j))],
        out_specs=pl.BlockSpec((cond.shape[0], bn), lambda j: (0, j)),
        out_shape=jax.ShapeDtypeStruct((cond.shape[0], n), F32),
        compiler_params=_params(),
        name="mods",
    )(cond, w_mod, b_mod.reshape(1, n))


def _swiglu(h_ref, a_ref, wgu_ref, wo_ref):
    h = h_ref[...]
    for j in range(N_FF_CHUNKS):
        cols = slice(j * FF_CHUNK, (j + 1) * FF_CHUNK)
        g = _dot(h, wgu_ref[:, cols])
        u = _dot(h, wgu_ref[:, D_FF + j * FF_CHUNK:D_FF + (j + 1) * FF_CHUNK])
        a_ref[:, cols] = ((g * _sigmoid(g)) * u).astype(BF16)
    return _dot(a_ref[...], wo_ref[...])


def _stage_shapes(width):
    rows = 128 if width <= D_MODEL else 32
    return [pltpu.VMEM((2, rows, width), F32), pltpu.SemaphoreType.DMA((2,))]


def _load_weight_bf16(src_hbm, dst, stage, sem):
    rows = stage.shape[1]
    n = src_hbm.shape[0] // rows
    assert n * rows == src_hbm.shape[0] and src_hbm.shape[1] == stage.shape[2]

    def copy(c):
        return pltpu.make_async_copy(src_hbm.at[pl.ds(c * rows, rows)], stage.at[c % 2], sem.at[c % 2])

    copy(0).start()
    for c in range(n):
        if c + 1 < n:
            copy(c + 1).start()
        copy(c).wait()
        dst[c * rows:(c + 1) * rows, :] = stage[c % 2].astype(BF16)


def _two_stream(n_first, tm_rows):
    first = lambda w: pl.BlockSpec((tm_rows, w), lambda i: (jnp.minimum(i, n_first - 1), 0))
    second = lambda w: pl.BlockSpec((tm_rows, w), lambda i: (jnp.maximum(i - n_first, 0), 0))
    return first, second


def _mod_spec(n_ctx_tiles, tm, seq_lat):
    def index(i):
        return (jnp.where(i < n_ctx_tiles, 0, 1 + (jnp.maximum(i - n_ctx_tiles, 0) * tm) // seq_lat), 0, 0)
    return pl.BlockSpec((1, N_MOD, D_MODEL), index)


def _ffn1_kernel(xp_ref, xs_ref, mod_ref, ng_ref, wgu_hbm, wo_hbm, win_hbm,
                 x1_ref, qkv_ref, xl_ref, gl_ref, gates_ref, newk_ref, newv_ref,
                 a_ref, h_ref, wgu_ref, wo_ref, win_ref, stage_wide, sem_wide, stage_d, sem_d, *, n_ctx_tiles):
    @pl.when(pl.program_id(0) == 0)
    def _():
        _load_weight_bf16(wgu_hbm, wgu_ref, stage_wide, sem_wide)
        _load_weight_bf16(wo_hbm, wo_ref, stage_d, sem_d)
        _load_weight_bf16(win_hbm, win_ref, stage_wide, sem_wide)

    is_ctx = pl.program_id(0) < n_ctx_tiles
    x = jnp.where(is_ctx, xp_ref[...], xs_ref[...])
    m = mod_ref[0]
    h_ref[...] = _rms_mod(x, ng_ref[0:1], m[0:1], m[1:2]).astype(BF16)
    x1 = x + (0.5 * m[2:3]) * _swiglu(h_ref, a_ref, wgu_ref, wo_ref)
    x1_ref[...] = x1
    h2 = _rms_mod(x1, ng_ref[1:2], m[3:4], m[4:5]).astype(BF16)
    c0, c1, c2, c3 = 3 * D_ATTN, 3 * D_ATTN + D_LRU, 3 * D_ATTN + 2 * D_LRU, 3 * D_ATTN + 2 * D_LRU + 2 * D_MODEL
    qkv = _dot(h2, win_ref[:, 0:c0])
    qkv_ref[:, :D_ATTN] = (qkv[:, :D_ATTN] * ATTN_SCALE).astype(BF16)
    qkv_ref[:, D_ATTN:] = qkv[:, D_ATTN:].astype(BF16)

    @pl.when(is_ctx)
    def _():
        for h in range(N_HEADS):
            newk_ref[0, 0, h] = qkv[:, D_ATTN + h * HEAD_DIM:D_ATTN + (h + 1) * HEAD_DIM]
            newv_ref[0, 0, h] = qkv[:, 2 * D_ATTN + h * HEAD_DIM:2 * D_ATTN + (h + 1) * HEAD_DIM]

    xl_ref[...] = _dot(h2, win_ref[:, c0:c1])
    gl_ref[...] = _dot(h2, win_ref[:, c1:c2])
    gates_ref[...] = _dot(h2, win_ref[:, c2:c3])


def _ffn1(xp, xs, mods3, norm_g, wgu, wo, win, nb_ctx, seq_ctx, seq_lat):
    tm = seq_ctx
    n_ctx_tiles = xp.shape[0] // tm
    t = xp.shape[0] + xs.shape[0]
    assert wgu.shape[1] == win.shape[1]
    row = lambda w: pl.BlockSpec((tm, w), lambda i: (i, 0))
    hbm = pl.BlockSpec(memory_space=pl.ANY)
    first, second = _two_stream(n_ctx_tiles, tm)
    cache_spec = pl.BlockSpec((1, 1, N_HEADS, seq_ctx, HEAD_DIM),
                              lambda i: (jnp.minimum(i, n_ctx_tiles - 1), 0, 0, 0, 0))
    cache_shape = jax.ShapeDtypeStruct((nb_ctx, 1, N_HEADS, seq_ctx, HEAD_DIM), F32)
    return pl.pallas_call(
        functools.partial(_ffn1_kernel, n_ctx_tiles=n_ctx_tiles),
        grid=(t // tm,),
        in_specs=[first(D_MODEL), second(D_MODEL), _mod_spec(n_ctx_tiles, tm, seq_lat),
                  _const_spec(norm_g.shape), hbm, hbm, hbm],
        out_specs=[row(D_MODEL), row(3 * D_ATTN), row(D_LRU), row(D_LRU), row(2 * D_MODEL),
                   cache_spec, cache_spec],
        out_shape=[jax.ShapeDtypeStruct((t, D_MODEL), F32),
                   jax.ShapeDtypeStruct((t, 3 * D_ATTN), BF16),
                   jax.ShapeDtypeStruct((t, D_LRU), F32),
                   jax.ShapeDtypeStruct((t, D_LRU), F32),
                   jax.ShapeDtypeStruct((t, 2 * D_MODEL), F32),
                   cache_shape, cache_shape],
        scratch_shapes=[pltpu.VMEM((tm, D_FF), BF16), pltpu.VMEM((tm, D_MODEL), BF16),
                        pltpu.VMEM(wgu.shape, BF16), pltpu.VMEM(wo.shape, BF16), pltpu.VMEM(win.shape, BF16)]
        + _stage_shapes(wgu.shape[1]) + _stage_shapes(D_MODEL),
        compiler_params=_params(),
        name="ffn1_inproj",
    )(xp, xs, mods3, norm_g, wgu, wo, win)


def _ffn2_kernel(x1_ref, attnp_ref, attns_ref, lrup_ref, lrus_ref, gates_ref, mod_ref, ng_ref, fg_ref,
                 wba_hbm, wbl_hbm, wout_hbm, wgu_hbm, wo_hbm, yp_ref, ys_ref,
                 a_ref, h_ref, wba_ref, wbl_ref, wout_ref, wgu_ref, wo_ref, stage_wide, sem_wide, stage_d, sem_d,
                 *, n_ctx_tiles):
    @pl.when(pl.program_id(0) == 0)
    def _():
        _load_weight_bf16(wba_hbm, wba_ref, stage_d, sem_d)
        _load_weight_bf16(wbl_hbm, wbl_ref, stage_d, sem_d)
        _load_weight_bf16(wout_hbm, wout_ref, stage_d, sem_d)
        _load_weight_bf16(wgu_hbm, wgu_ref, stage_wide, sem_wide)
        _load_weight_bf16(wo_hbm, wo_ref, stage_d, sem_d)

    is_ctx = pl.program_id(0) < n_ctx_tiles
    attn = jnp.where(is_ctx, attnp_ref[...], attns_ref[...])
    lru = jnp.where(is_ctx, lrup_ref[...], lrus_ref[...])
    m = mod_ref[0]
    gt = gates_ref[...]
    mm = (_sigmoid(gt[:, :D_MODEL]) * _dot(attn, wba_ref[...])
          + _sigmoid(gt[:, D_MODEL:]) * _dot(lru, wbl_ref[...]))
    x2 = x1_ref[...] + m[5:6] * _dot(mm.astype(BF16), wout_ref[...])
    h_ref[...] = _rms_mod(x2, ng_ref[2:3], m[6:7], m[7:8]).astype(BF16)
    x3 = x2 + (0.5 * m[8:9]) * _swiglu(h_ref, a_ref, wgu_ref, wo_ref)
    y = x3 * lax.rsqrt(jnp.mean(x3 * x3, axis=-1, keepdims=True) + EPS)
    y = y * fg_ref[...]

    @pl.when(is_ctx)
    def _():
        yp_ref[...] = y

    @pl.when(jnp.logical_not(is_ctx))
    def _():
        ys_ref[...] = y


def _ffn2(x1, attnp, attns, lrup, lrus, gates, mods3, norm_g, final_g, wba, wbl, wout, wgu, wo, tm, seq_lat):
    t = x1.shape[0]
    t_ctx = attnp.shape[0]
    n_ctx_tiles = t_ctx // tm
    row = lambda w: pl.BlockSpec((tm, w), lambda i: (i, 0))
    hbm = pl.BlockSpec(memory_space=pl.ANY)
    first, second = _two_stream(n_ctx_tiles, tm)
    return pl.pallas_call(
        functools.partial(_ffn2_kernel, n_ctx_tiles=n_ctx_tiles),
        grid=(t // tm,),
        in_specs=[row(D_MODEL), first(D_ATTN), second(D_ATTN), first(D_LRU), second(D_LRU), row(2 * D_MODEL),
                  _mod_spec(n_ctx_tiles, tm, seq_lat),
                  _const_spec(norm_g.shape), _const_spec(final_g.shape), hbm, hbm, hbm, hbm, hbm],
        out_specs=[first(D_MODEL), second(D_MODEL)],
        out_shape=[jax.ShapeDtypeStruct((t_ctx, D_MODEL), F32), jax.ShapeDtypeStruct((t - t_ctx, D_MODEL), F32)],
        scratch_shapes=[pltpu.VMEM((tm, D_FF), BF16), pltpu.VMEM((tm, D_MODEL), BF16)]
        + [pltpu.VMEM(w.shape, BF16) for w in (wba, wbl, wout, wgu, wo)]
        + _stage_shapes(wgu.shape[1]) + _stage_shapes(D_MODEL),
        compiler_params=_params(),
        name="merge_ffn2",
    )(x1, attnp, attns, lrup, lrus, gates, mods3, norm_g, final_g, wba, wbl, wout, wgu, wo)


def _softmax_pv(scores, values):
    m = scores[0].max(axis=-1, keepdims=True)
    for s in scores[1:]:
        m = jnp.maximum(m, s.max(axis=-1, keepdims=True))
    l = None
    o = None
    for s, v in zip(scores, values):
        p = jnp.exp(s - m)
        ls = p.sum(axis=-1, keepdims=True)
        os_ = _dot(p.astype(BF16), v)
        l = ls if l is None else l + ls
        o = os_ if o is None else o + os_
    return o / l


def _ctx_attn_kernel(qkv_ref, o_ref):
    seq = qkv_ref.shape[0]
    lane = lax.broadcasted_iota(jnp.int32, (seq, LANES), 1)
    first = lane < HEAD_DIM
    for hp in range(D_ATTN // LANES):
        q = qkv_ref[:, hp * LANES:(hp + 1) * LANES]
        k = qkv_ref[:, D_ATTN + hp * LANES:D_ATTN + (hp + 1) * LANES]
        v = qkv_ref[:, 2 * D_ATTN + hp * LANES:2 * D_ATTN + (hp + 1) * LANES]
        outs = []
        for sel in (first, jnp.logical_not(first)):
            qm = jnp.where(sel, q, jnp.zeros_like(q))
            outs.append(_softmax_pv([_dot_nt(qm, k)], [v]))
        o_ref[:, hp * LANES:(hp + 1) * LANES] = jnp.where(first, outs[0], outs[1]).astype(BF16)


def _ctx_attention(qkv, nb, seq):
    return pl.pallas_call(
        _ctx_attn_kernel,
        grid=(nb,),
        in_specs=[pl.BlockSpec((seq, 3 * D_ATTN), lambda b: (b, 0))],
        out_specs=pl.BlockSpec((seq, D_ATTN), lambda b: (b, 0)),
        out_shape=jax.ShapeDtypeStruct((nb * seq, D_ATTN), BF16),
        compiler_params=_params(),
        name="ctx_attention",
    )(qkv)


def _na_groups(rows):
    kh = min(WIN_H, rows)
    r = np.arange(rows)
    rs = np.clip(r - kh // 2, 0, rows - kh)
    groups = []
    for g in range(rows // ROWS_PER_GROUP):
        qr = r[g * ROWS_PER_GROUP:(g + 1) * ROWS_PER_GROUP]
        k0, k1 = int(rs[qr].min()), int(rs[qr].max()) + kh
        if (k1 - k0) % 2:
            if k1 < rows:
                k1 += 1
            else:
                k0 -= 1
        groups.append((k0, k1))
    return groups, rs, kh


def _na_rel_rows(rpb):
    edge = GRID_W - WIN_W
    v = jnp.concatenate([rpb[..., WIN_W - 1:],
                         jnp.repeat(rpb[..., -1:], edge, axis=-1),
                         jnp.zeros(rpb.shape[:-1] + (1,), F32),
                         jnp.repeat(rpb[..., :1], edge, axis=-1),
                         rpb[..., :WIN_W - 1]], axis=-1).astype(F32)
    return jnp.pad(v, ((0, 0), (0, 1), (0, 0)))


def _na_build_bias(rel_ref, bias_ref, groups, rs, kh):
    shape = (GRID_W, LANES)
    qc = lax.broadcasted_iota(jnp.int32, shape, 0)
    lane = lax.broadcasted_iota(jnp.int32, shape, 1)
    left = lane < GRID_W
    kc = jnp.where(left, lane, lane - GRID_W)
    cs = jnp.clip(qc - WIN_W // 2, 0, GRID_W - WIN_W)
    in_win = (kc >= cs) & (kc < cs + WIN_W)
    neg = jnp.full(shape, NEG_INF, F32)
    for hh in range(2):
        def tile(r, j, half, hh=hh):
            row = jnp.broadcast_to(rel_ref[hh, j - r + WIN_H - 1:j - r + WIN_H, :], shape)
            return pltpu.roll(row, half * GRID_W, 1, stride=1, stride_axis=0)

        boff = 0
        for g, (k0, k1) in enumerate(groups):
            for i in range(ROWS_PER_GROUP):
                r = g * ROWS_PER_GROUP + i
                for jp in range((k1 - k0) // 2):
                    j = k0 + 2 * jp
                    a_ok = rs[r] <= j < rs[r] + kh
                    b_ok = rs[r] <= j + 1 < rs[r] + kh
                    if a_ok and b_ok:
                        t = jnp.where(in_win, jnp.where(left, tile(r, j, 0), tile(r, j + 1, 1)), neg)
                    elif a_ok:
                        t = jnp.where(in_win & left, tile(r, j, 0), neg)
                    elif b_ok:
                        t = jnp.where(in_win & jnp.logical_not(left), tile(r, j + 1, 1), neg)
                    else:
                        t = neg
                    bias_ref[hh, i * GRID_W:(i + 1) * GRID_W, boff + jp * LANES:boff + (jp + 1) * LANES] = t
            boff += (k1 - k0) * GRID_W


def _na_attn_kernel(q_ref, k_ref, v_ref, kc_ref, vc_ref, rel_ref, o_ref, bias_ref, *, groups, rs, kh):
    @pl.when(pl.program_id(1) == 0)
    def _():
        _na_build_bias(rel_ref, bias_ref, groups, rs, kh)

    gq = ROWS_PER_GROUP * GRID_W
    lane = lax.broadcasted_iota(jnp.int32, (gq, LANES), 1)
    first = lane < HEAD_DIM

    def pair(ref):
        return jnp.concatenate([ref[0, 0, 0], ref[0, 0, 1]], axis=1).astype(BF16)

    kc = pair(kc_ref)
    vc = pair(vc_ref)
    for g, (k0, k1) in enumerate(groups):
        q = q_ref[g * gq:(g + 1) * gq, :]
        k = k_ref[k0 * GRID_W:k1 * GRID_W, :]
        v = v_ref[k0 * GRID_W:k1 * GRID_W, :]
        boff = sum((b - a) * GRID_W for a, b in groups[:g])
        outs = []
        for hh, sel in enumerate((first, jnp.logical_not(first))):
            qm = jnp.where(sel, q, jnp.zeros_like(q))
            s_win = _dot_nt(qm, k) + bias_ref[hh, :, boff:boff + (k1 - k0) * GRID_W]
            outs.append(_softmax_pv([s_win, _dot_nt(qm, kc)], [v, vc]))
        o_ref[g * gq:(g + 1) * gq, :] = jnp.where(first, outs[0], outs[1]).astype(BF16)


def _na_attention(qkv, kc, vc, rel, first_seq, nb, seq):
    t = nb * seq
    past = kc.shape[3]
    npair = D_ATTN // LANES
    groups, rs, kh = _na_groups(seq // GRID_W)
    width = sum(k1 - k0 for k0, k1 in groups) * GRID_W
    tok = lambda off: pl.BlockSpec((seq, LANES), lambda hp, b: (first_seq + b, off + hp))
    ctx = pl.BlockSpec((1, 1, 2, past, HEAD_DIM), lambda hp, b: (b, 0, hp, 0, 0))
    return pl.pallas_call(
        functools.partial(_na_attn_kernel, groups=groups, rs=[int(x) for x in rs], kh=kh),
        grid=(npair, nb),
        in_specs=[tok(0), tok(npair), tok(2 * npair), ctx, ctx,
                  pl.BlockSpec((2,) + rel.shape[1:], lambda hp, b: (hp, 0, 0))],
        out_specs=pl.BlockSpec((seq, LANES), lambda hp, b: (b, hp)),
        out_shape=jax.ShapeDtypeStruct((t, D_ATTN), BF16),
        scratch_shapes=[pltpu.VMEM((2, ROWS_PER_GROUP * GRID_W, width), F32)],
        compiler_params=_params(2),
        name="na_attention",
    )(qkv, qkv, qkv, kc, vc, rel)


def _gelu_tanh(x):
    c = float(np.sqrt(2.0 / np.pi))
    t = jnp.tanh(x * (c + (c * 0.044715) * (x * x)))
    return x * (0.5 * t + 0.5)


def _lru_kernel(xl_ref, gl_ref, h0_ref, cw_ref, cb_ref, wbd_ref, ba_ref, bi_ref, coef_ref,
                y_ref, hfin_ref, xpad, xc, a_buf, hf_buf, hb_buf, *, seq):
    pitch = seq // N_SEG + SEG_PAD
    lp = N_SEG * pitch
    halo = SUBLANES
    chunk = LANES
    n_chunks = seq // chunk
    left = CONV_W // 2

    for k in range(N_SLABS):
        cols = slice(k * LANES, (k + 1) * LANES)
        xpad[k, 0:halo, :] = jnp.zeros((halo, LANES), F32)
        xpad[k, halo + seq:2 * halo + seq, :] = jnp.zeros((halo, LANES), F32)
        xpad[k, halo:halo + seq, :] = xl_ref[:, cols]

    def conv_body(i, carry):
        r0 = pl.multiple_of(i * chunk, chunk)
        for k in range(N_SLABS):
            cols = slice(k * LANES, (k + 1) * LANES)
            out = cb_ref[:, cols]
            for j in range(CONV_W):
                out = out + xpad[k, pl.ds(r0 + halo + j - left, chunk), :] * cw_ref[j:j + 1, cols]
            xc[k, pl.ds(r0, chunk), :] = out
        return carry

    lax.fori_loop(0, n_chunks, conv_body, 0)

    a_buf[:, seq:lp, :] = jnp.ones((N_SLABS, lp - seq, LANES), F32)
    hf_buf[:, seq:lp, :] = jnp.zeros((N_SLABS, lp - seq, LANES), F32)
    hb_buf[:, seq:lp, :] = jnp.zeros((N_SLABS, lp - seq, LANES), F32)

    row = lax.broadcasted_iota(jnp.int32, (N_SEG, LANES), 0)
    for d, h_buf in enumerate((hf_buf, hb_buf)):
        def gates_body(i, carry, d=d, h_buf=h_buf):
            rows = pl.ds(pl.multiple_of(i * chunk, chunk), chunk)
            per = LRU_TILE // LANES
            for c in range(N_LRU_TILES):
                xs = [xc[c * per + half, rows, :] for half in range(per)]
                pre = _dot(jnp.concatenate(xs, axis=1).astype(BF16), wbd_ref[d, c])
                for half in range(per):
                    k = c * per + half
                    cols = slice(k * LANES, (k + 1) * LANES)
                    tr = jnp.tanh(pre[:, half * LANES:(half + 1) * LANES] + ba_ref[d:d + 1, cols])
                    ti = jnp.tanh(pre[:, LRU_TILE + half * LANES:LRU_TILE + (half + 1) * LANES]
                                  + bi_ref[d:d + 1, cols])
                    a = jnp.exp2(coef_ref[d:d + 1, cols] * tr + coef_ref[d:d + 1, cols])
                    y = (1.0 - a) * (1.0 + a)
                    mult = y * lax.rsqrt(jnp.maximum(y, F32_TINY))
                    a_buf[k, rows, :] = a
                    h_buf[k, rows, :] = mult * ((0.5 * ti + 0.5) * xs[half])
            return carry

        lax.fori_loop(0, n_chunks, gates_body, 0)

        def seg_rows(t, d=d):
            tt = t if d == 0 else pitch - 1 - t
            return pl.ds(tt, N_SEG, stride=pitch)

        def totals_body(t, carry, h_buf=h_buf):
            hs, ps = carry
            idx = seg_rows(t)
            nh, np_ = [], []
            for k in range(N_SLABS):
                a = a_buf[k, idx, :]
                b = h_buf[k, idx, :]
                nh.append(a * hs[k] + b)
                np_.append(ps[k] * a)
            return tuple(nh), tuple(np_)

        zeros = tuple(jnp.zeros((N_SEG, LANES), F32) for _ in range(N_SLABS))
        ones = tuple(jnp.ones((N_SEG, LANES), F32) for _ in range(N_SLABS))
        h_end, p_end = lax.fori_loop(0, pitch, totals_body, (zeros, ones), unroll=SCAN_UNROLL)

        order = range(N_SEG) if d == 0 else range(N_SEG - 1, -1, -1)
        starts = []
        for k in range(N_SLABS):
            c = h0_ref[0, d:d + 1, k * LANES:(k + 1) * LANES]
            cm = jnp.zeros((N_SEG, LANES), F32)
            for s in order:
                cm = jnp.where(row == s, c, cm)
                c = p_end[k][s:s + 1, :] * c + h_end[k][s:s + 1, :]
            starts.append(cm)
            hfin_ref[0, d:d + 1, k * LANES:(k + 1) * LANES] = c

        def scan_body(t, hs, h_buf=h_buf):
            idx = seg_rows(t)
            nh = []
            for k in range(N_SLABS):
                h = a_buf[k, idx, :] * hs[k] + h_buf[k, idx, :]
                h_buf[k, idx, :] = h
                nh.append(h)
            return tuple(nh)

        lax.fori_loop(0, pitch, scan_body, tuple(starts), unroll=SCAN_UNROLL)

    def out_body(i, carry):
        r0 = pl.multiple_of(i * chunk, chunk)
        for k in range(N_SLABS):
            cols = slice(k * LANES, (k + 1) * LANES)
            h = hf_buf[k, pl.ds(r0, chunk), :] + hb_buf[k, pl.ds(r0, chunk), :]
            y_ref[pl.ds(r0, chunk), cols] = (h * _gelu_tanh(gl_ref[pl.ds(r0, chunk), cols])).astype(BF16)
        return carry

    lax.fori_loop(0, n_chunks, out_body, 0)


def _lru(xl, gl, h0, conv_w, conv_b, wbd, ba, bi, coef, first_seq, nb, seq):
    t = nb * seq
    lp = N_SEG * (seq // N_SEG + SEG_PAD)
    tok_in = pl.BlockSpec((seq, D_LRU), lambda b: (first_seq + b, 0))
    tok = pl.BlockSpec((seq, D_LRU), lambda b: (b, 0))
    state = pl.BlockSpec((1, 2, D_LRU), lambda b: (b, 0, 0))
    scan_buf = pltpu.VMEM((N_SLABS, lp, LANES), F32)
    return pl.pallas_call(
        functools.partial(_lru_kernel, seq=seq),
        grid=(nb,),
        in_specs=[tok_in, tok_in, state, _const_spec(conv_w.shape), _const_spec(conv_b.shape),
                  _const_spec(wbd.shape), _const_spec(ba.shape), _const_spec(bi.shape),
                  _const_spec(coef.shape)],
        out_specs=[tok, state],
        out_shape=[jax.ShapeDtypeStruct((t, D_LRU), BF16), jax.ShapeDtypeStruct((nb, 2, D_LRU), F32)],
        scratch_shapes=[pltpu.VMEM((N_SLABS, seq + 2 * SUBLANES, LANES), F32),
                        pltpu.VMEM((N_SLABS, seq, LANES), F32), scan_buf, scan_buf, scan_buf],
        compiler_params=_params(),
        name="rglru",
    )(xl, gl, h0, conv_w, conv_b, wbd, ba, bi, coef)


def _blockdiag_weights(wa, wi):
    per = LRU_TILE // LRU_BLOCK
    out = jnp.zeros((2, N_LRU_TILES, LRU_TILE, 2 * LRU_TILE), BF16)
    for half, w in enumerate((wa, wi)):
        w = (0.5 * w).astype(BF16).reshape(2, N_LRU_TILES, per, LRU_BLOCK, LRU_BLOCK)
        for p in range(per):
            c0 = half * LRU_TILE + p * LRU_BLOCK
            out = out.at[:, :, p * LRU_BLOCK:(p + 1) * LRU_BLOCK, c0:c0 + LRU_BLOCK].set(w[:, :, p])
    return out


def kernel(x_prompt, x_sample, cache_k, cache_v, state_lru, c, c_ctx, w_mod, b_mod, norm_g, ffn1_w_in, ffn1_w_out,
           w_in, rpb, conv_w, conv_b, lru_wa, lru_ba, lru_wi, lru_bi, lru_lambda, w_br_attn, w_br_lru, w_out,
           ffn2_w_in, ffn2_w_out, final_g):
    assert w_mod.shape[0] == 1, "single trunk layer"
    nb_ctx, seq_ctx, _ = x_prompt.shape
    nb_lat, seq_lat, _ = x_sample.shape

    n_cond = 1 + nb_lat
    cond = jnp.concatenate([c_ctx[None, :], c, jnp.zeros((2 * SUBLANES - n_cond, D_MODEL), F32)], axis=0)
    mods3 = _mods(cond, w_mod[0], b_mod[0]).reshape(2 * SUBLANES, N_MOD, D_MODEL)

    wgu1, wo1, win = ffn1_w_in[0], ffn1_w_out[0], w_in[0]
    wgu2, wo2 = ffn2_w_in[0], ffn2_w_out[0]
    wba, wbl, wout = w_br_attn[0], w_br_lru[0], w_out[0]
    wbd = _blockdiag_weights(lru_wa[0], lru_wi[0])
    coef = (0.5 * float(np.log2(np.e))) * (-LRU_C * jax.nn.softplus(-lru_lambda[0]))
    ba = 0.5 * lru_ba[0]
    bi = 0.5 * lru_bi[0]
    ng = norm_g[0]
    fg = final_g.reshape(1, D_MODEL)
    cb = conv_b[0].reshape(1, D_LRU)

    t_ctx = nb_ctx * seq_ctx
    assert t_ctx % seq_lat == 0, "latent sequences must start on a latent-sequence-sized row block"
    xp = x_prompt.reshape(t_ctx, D_MODEL)
    xs = x_sample.reshape(nb_lat * seq_lat, D_MODEL)

    x1, qkv, xl, gl, gates, new_k, new_v = _ffn1(xp, xs, mods3, ng, wgu1, wo1, win, nb_ctx, seq_ctx, seq_lat)

    attnp = _ctx_attention(qkv, nb_ctx, seq_ctx)
    lrup, h_fin = _lru(xl, gl, jnp.zeros((nb_ctx, 2, D_LRU), F32), conv_w[0], cb, wbd,
                       ba, bi, coef, 0, nb_ctx, seq_ctx)

    rel = _na_rel_rows(rpb[0])
    first_lat = t_ctx // seq_lat
    attns = _na_attention(qkv, cache_k, cache_v, rel, first_lat, nb_lat, seq_lat)
    lrus, _ = _lru(xl, gl, state_lru[:, 0], conv_w[0], cb, wbd, ba, bi, coef,
                   first_lat, nb_lat, seq_lat)

    yp, ys = _ffn2(x1, attnp, attns, lrup, lrus, gates, mods3, ng, fg, wba, wbl, wout, wgu2, wo2, 512, seq_lat)
    return (yp.reshape(nb_ctx, seq_ctx, D_MODEL), ys.reshape(nb_lat, seq_lat, D_MODEL),
            new_k, new_v, h_fin[:, None])
```

```python
import functools

import numpy as np
import jax
import jax.numpy as jnp
from jax import lax
from jax.experimental import pallas as pl
from jax.experimental.pallas import tpu as pltpu

F32 = jnp.float32
BF16 = jnp.bfloat16

D_MODEL = 1024
N_HEADS = 8
HEAD_DIM = 64
D_ATTN = N_HEADS * HEAD_DIM
GRID_W = 64
WIN_H = 8
WIN_W = 16
D_LRU = 1024
LRU_BLOCKS = 16
LRU_BLOCK = D_LRU // LRU_BLOCKS
CONV_W = 4
LRU_C = 8.0
D_FF = 2816
N_MOD = 9
EPS = 1e-6
NEG_INF = -1e30
F32_TINY = float(np.finfo(np.float32).tiny)
ATTN_SCALE = HEAD_DIM ** -0.5

LANES = 128
SUBLANES = 8
MXU_DIM = 256
VMEM_LIMIT_BYTES = 56 * 1024 * 1024

FF_CHUNK = MXU_DIM
N_FF_CHUNKS = D_FF // FF_CHUNK
LRU_TILE = MXU_DIM
N_LRU_TILES = D_LRU // LRU_TILE
N_SLABS = D_LRU // LANES
N_SEG = SUBLANES
SEG_PAD = 4
SCAN_UNROLL = 4
ROWS_PER_GROUP = 4


def _sigmoid(x):
    return 0.5 * jnp.tanh(0.5 * x) + 0.5


def _rms_mod(x, g, shift, scale):
    y = x * lax.rsqrt(jnp.mean(x * x, axis=-1, keepdims=True) + EPS)
    y = y * g
    return y * (1 + scale) + shift


def _dot(a, b):
    return jnp.dot(a, b, preferred_element_type=F32)


def _dot_nt(a, b):
    return lax.dot_general(a, b, (((1,), (1,)), ((), ())), preferred_element_type=F32)


def _params(n_axes=1):
    return pltpu.CompilerParams(dimension_semantics=("arbitrary",) * n_axes,
                                vmem_limit_bytes=VMEM_LIMIT_BYTES)


def _const_spec(shape):
    nd = len(shape)
    return pl.BlockSpec(shape, lambda *_: (0,) * nd, pipeline_mode=pl.Buffered(1))


def _mods_kernel(cond_ref, w_ref, b_ref, o_ref):
    c = cond_ref[...]
    s = (c * _sigmoid(c)).astype(BF16)
    o_ref[...] = _dot(s, w_ref[...].astype(BF16)) + b_ref[...]


def _mods(cond, w_mod, b_mod):
    n = w_mod.shape[1]
    bn = D_MODEL
    return pl.pallas_call(
        _mods_kernel,
        grid=(n // bn,),
        in_specs=[pl.BlockSpec(cond.shape, lambda j: (0, 0)),
                  pl.BlockSpec((D_MODEL, bn), lambda j: (0, j)),
                  pl.BlockSpec((1, bn), lambda j: (0, j))],
        out_specs=pl.BlockSpec((cond.shape[0], bn), lambda j: (0, j)),
        out_shape=jax.ShapeDtypeStruct((cond.shape[0], n), F32),
        compiler_params=_params(),
        name="mods",
    )(cond, w_mod, b_mod.reshape(1, n))


def _swiglu(h_ref, a_ref, wgu_ref, wo_ref):
    h = h_ref[...]
    for j in range(N_FF_CHUNKS):
        cols = slice(j * FF_CHUNK, (j + 1) * FF_CHUNK)
        g = _dot(h, wgu_ref[:, cols])
        u = _dot(h, wgu_ref[:, D_FF + j * FF_CHUNK:D_FF + (j + 1) * FF_CHUNK])
        a_ref[:, cols] = ((g * _sigmoid(g)) * u).astype(BF16)
    return _dot(a_ref[...], wo_ref[...])


BF16_ROWS = 2 * SUBLANES


def _cast_spec(w, n_steps):
    rows = BF16_ROWS * pl.cdiv(w.shape[0], BF16_ROWS * n_steps)
    n_chunks = w.shape[0] // rows
    assert n_chunks * rows == w.shape[0] and n_chunks <= n_steps
    return pl.BlockSpec((rows, w.shape[1]), lambda i: (jnp.minimum(i, n_chunks - 1), 0))


def _two_stream(n_first, tm_rows):
    first = lambda w: pl.BlockSpec((tm_rows, w), lambda i: (jnp.minimum(i, n_first - 1), 0))
    second = lambda w: pl.BlockSpec((tm_rows, w), lambda i: (jnp.maximum(i - n_first, 0), 0))
    return first, second


def _mod_spec(n_ctx_tiles, tm, seq_lat):
    def index(i):
        return (jnp.where(i < n_ctx_tiles, 0, 1 + (jnp.maximum(i - n_ctx_tiles, 0) * tm) // seq_lat), 0, 0)
    return pl.BlockSpec((1, N_MOD, D_MODEL), index)


N_LATE_WEIGHTS = 5


def _ffn1_kernel(xp_ref, xs_ref, mod_ref, ng_ref, wgu_ref, wo_ref, win_ref, *rest, n_ctx_tiles):
    late_f32 = rest[:N_LATE_WEIGHTS]
    x1_ref, qkv_ref, xl_ref, gl_ref, gates_ref, newk_ref, newv_ref = rest[N_LATE_WEIGHTS:N_LATE_WEIGHTS + 7]
    late_bf16 = rest[N_LATE_WEIGHTS + 7:2 * N_LATE_WEIGHTS + 7]
    a_ref, h_ref = rest[2 * N_LATE_WEIGHTS + 7:]

    for src, dst in zip(late_f32, late_bf16):
        dst[...] = src[...].astype(BF16)

    is_ctx = pl.program_id(0) < n_ctx_tiles
    x = jnp.where(is_ctx, xp_ref[...], xs_ref[...])
    m = mod_ref[0]
    h_ref[...] = _rms_mod(x, ng_ref[0:1], m[0:1], m[1:2]).astype(BF16)
    x1 = x + (0.5 * m[2:3]) * _swiglu(h_ref, a_ref, wgu_ref, wo_ref)
    x1_ref[...] = x1
    h2 = _rms_mod(x1, ng_ref[1:2], m[3:4], m[4:5]).astype(BF16)
    c0, c1, c2, c3 = 3 * D_ATTN, 3 * D_ATTN + D_LRU, 3 * D_ATTN + 2 * D_LRU, 3 * D_ATTN + 2 * D_LRU + 2 * D_MODEL
    qkv = _dot(h2, win_ref[:, 0:c0])
    qkv_ref[:, :D_ATTN] = (qkv[:, :D_ATTN] * ATTN_SCALE).astype(BF16)
    qkv_ref[:, D_ATTN:] = qkv[:, D_ATTN:].astype(BF16)

    @pl.when(is_ctx)
    def _():
        for h in range(N_HEADS):
            newk_ref[0, 0, h] = qkv[:, D_ATTN + h * HEAD_DIM:D_ATTN + (h + 1) * HEAD_DIM]
            newv_ref[0, 0, h] = qkv[:, 2 * D_ATTN + h * HEAD_DIM:2 * D_ATTN + (h + 1) * HEAD_DIM]

    xl_ref[...] = _dot(h2, win_ref[:, c0:c1])
    gl_ref[...] = _dot(h2, win_ref[:, c1:c2])
    gates_ref[...] = _dot(h2, win_ref[:, c2:c3])


def _ffn1(xp, xs, mods3, norm_g, wgu, wo, win, late_weights, nb_ctx, seq_ctx, seq_lat):
    tm = seq_ctx
    n_ctx_tiles = xp.shape[0] // tm
    t = xp.shape[0] + xs.shape[0]
    n_steps = t // tm
    assert len(late_weights) == N_LATE_WEIGHTS
    row = lambda w: pl.BlockSpec((tm, w), lambda i: (i, 0))
    first, second = _two_stream(n_ctx_tiles, tm)
    cache_spec = pl.BlockSpec((1, 1, N_HEADS, seq_ctx, HEAD_DIM),
                              lambda i: (jnp.minimum(i, n_ctx_tiles - 1), 0, 0, 0, 0))
    cache_shape = jax.ShapeDtypeStruct((nb_ctx, 1, N_HEADS, seq_ctx, HEAD_DIM), F32)
    cast_specs = [_cast_spec(w, n_steps) for w in late_weights]
    return pl.pallas_call(
        functools.partial(_ffn1_kernel, n_ctx_tiles=n_ctx_tiles),
        grid=(n_steps,),
        in_specs=[first(D_MODEL), second(D_MODEL), _mod_spec(n_ctx_tiles, tm, seq_lat),
                  _const_spec(norm_g.shape), _const_spec(wgu.shape), _const_spec(wo.shape),
                  _const_spec(win.shape)] + cast_specs,
        out_specs=[row(D_MODEL), row(3 * D_ATTN), row(D_LRU), row(D_LRU), row(2 * D_MODEL),
                   cache_spec, cache_spec] + cast_specs,
        out_shape=[jax.ShapeDtypeStruct((t, D_MODEL), F32),
                   jax.ShapeDtypeStruct((t, 3 * D_ATTN), BF16),
                   jax.ShapeDtypeStruct((t, D_LRU), F32),
                   jax.ShapeDtypeStruct((t, D_LRU), F32),
                   jax.ShapeDtypeStruct((t, 2 * D_MODEL), F32),
                   cache_shape, cache_shape] + [jax.ShapeDtypeStruct(w.shape, BF16) for w in late_weights],
        scratch_shapes=[pltpu.VMEM((tm, D_FF), BF16), pltpu.VMEM((tm, D_MODEL), BF16)],
        compiler_params=_params(),
        name="ffn1_inproj",
    )(xp, xs, mods3, norm_g, wgu, wo, win, *late_weights)


def _ffn2_kernel(x1_ref, attnp_ref, attns_ref, lrup_ref, lrus_ref, gates_ref, mod_ref, ng_ref, fg_ref,
                 wba_ref, wbl_ref, wout_ref, wgu_ref, wo_ref, yp_ref, ys_ref, a_ref, h_ref, *, n_ctx_tiles):
    is_ctx = pl.program_id(0) < n_ctx_tiles
    attn = jnp.where(is_ctx, attnp_ref[...], attns_ref[...])
    lru = jnp.where(is_ctx, lrup_ref[...], lrus_ref[...])
    m = mod_ref[0]
    gt = gates_ref[...]
    mm = (_sigmoid(gt[:, :D_MODEL]) * _dot(attn, wba_ref[...])
          + _sigmoid(gt[:, D_MODEL:]) * _dot(lru, wbl_ref[...]))
    x2 = x1_ref[...] + m[5:6] * _dot(mm.astype(BF16), wout_ref[...])
    h_ref[...] = _rms_mod(x2, ng_ref[2:3], m[6:7], m[7:8]).astype(BF16)
    x3 = x2 + (0.5 * m[8:9]) * _swiglu(h_ref, a_ref, wgu_ref, wo_ref)
    y = x3 * lax.rsqrt(jnp.mean(x3 * x3, axis=-1, keepdims=True) + EPS)
    y = y * fg_ref[...]

    @pl.when(is_ctx)
    def _():
        yp_ref[...] = y

    @pl.when(jnp.logical_not(is_ctx))
    def _():
        ys_ref[...] = y


def _ffn2(x1, attnp, attns, lrup, lrus, gates, mods3, norm_g, final_g, wba, wbl, wout, wgu, wo, tm, seq_lat):
    t = x1.shape[0]
    t_ctx = attnp.shape[0]
    n_ctx_tiles = t_ctx // tm
    row = lambda w: pl.BlockSpec((tm, w), lambda i: (i, 0))
    first, second = _two_stream(n_ctx_tiles, tm)
    return pl.pallas_call(
        functools.partial(_ffn2_kernel, n_ctx_tiles=n_ctx_tiles),
        grid=(t // tm,),
        in_specs=[row(D_MODEL), first(D_ATTN), second(D_ATTN), first(D_LRU), second(D_LRU), row(2 * D_MODEL),
                  _mod_spec(n_ctx_tiles, tm, seq_lat),
                  _const_spec(norm_g.shape), _const_spec(final_g.shape),
                  _const_spec(wba.shape), _const_spec(wbl.shape), _const_spec(wout.shape),
                  _const_spec(wgu.shape), _const_spec(wo.shape)],
        out_specs=[first(D_MODEL), second(D_MODEL)],
        out_shape=[jax.ShapeDtypeStruct((t_ctx, D_MODEL), F32), jax.ShapeDtypeStruct((t - t_ctx, D_MODEL), F32)],
        scratch_shapes=[pltpu.VMEM((tm, D_FF), BF16), pltpu.VMEM((tm, D_MODEL), BF16)],
        compiler_params=_params(),
        name="merge_ffn2",
    )(x1, attnp, attns, lrup, lrus, gates, mods3, norm_g, final_g, wba, wbl, wout, wgu, wo)


def _softmax_pv(scores, values):
    m = scores[0].max(axis=-1, keepdims=True)
    for s in scores[1:]:
        m = jnp.maximum(m, s.max(axis=-1, keepdims=True))
    l = None
    o = None
    for s, v in zip(scores, values):
        p = jnp.exp(s - m)
        ls = p.sum(axis=-1, keepdims=True)
        os_ = _dot(p.astype(BF16), v)
        l = ls if l is None else l + ls
        o = os_ if o is None else o + os_
    return o / l


def _ctx_attn_kernel(qkv_ref, o_ref):
    seq = qkv_ref.shape[0]
    lane = lax.broadcasted_iota(jnp.int32, (seq, LANES), 1)
    first = lane < HEAD_DIM
    for hp in range(D_ATTN // LANES):
        q = qkv_ref[:, hp * LANES:(hp + 1) * LANES]
        k = qkv_ref[:, D_ATTN + hp * LANES:D_ATTN + (hp + 1) * LANES]
        v = qkv_ref[:, 2 * D_ATTN + hp * LANES:2 * D_ATTN + (hp + 1) * LANES]
        outs = []
        for sel in (first, jnp.logical_not(first)):
            qm = jnp.where(sel, q, jnp.zeros_like(q))
            outs.append(_softmax_pv([_dot_nt(qm, k)], [v]))
        o_ref[:, hp * LANES:(hp + 1) * LANES] = jnp.where(first, outs[0], outs[1]).astype(BF16)


def _ctx_attention(qkv, nb, seq):
    return pl.pallas_call(
        _ctx_attn_kernel,
        grid=(nb,),
        in_specs=[pl.BlockSpec((seq, 3 * D_ATTN), lambda b: (b, 0))],
        out_specs=pl.BlockSpec((seq, D_ATTN), lambda b: (b, 0)),
        out_shape=jax.ShapeDtypeStruct((nb * seq, D_ATTN), BF16),
        compiler_params=_params(),
        name="ctx_attention",
    )(qkv)


def _na_groups(rows):
    kh = min(WIN_H, rows)
    r = np.arange(rows)
    rs = np.clip(r - kh // 2, 0, rows - kh)
    groups = []
    for g in range(rows // ROWS_PER_GROUP):
        qr = r[g * ROWS_PER_GROUP:(g + 1) * ROWS_PER_GROUP]
        k0, k1 = int(rs[qr].min()), int(rs[qr].max()) + kh
        if (k1 - k0) % 2:
            if k1 < rows:
                k1 += 1
            else:
                k0 -= 1
        groups.append((k0, k1))
    return groups, rs, kh


def _na_rel_rows(rpb):
    edge = GRID_W - WIN_W
    v = jnp.concatenate([rpb[..., WIN_W - 1:],
                         jnp.repeat(rpb[..., -1:], edge, axis=-1),
                         jnp.zeros(rpb.shape[:-1] + (1,), F32),
                         jnp.repeat(rpb[..., :1], edge, axis=-1),
                         rpb[..., :WIN_W - 1]], axis=-1).astype(F32)
    return jnp.pad(v, ((0, 0), (0, 1), (0, 0)))


def _na_build_bias(rel_ref, bias_ref, groups, rs, kh):
    shape = (GRID_W, LANES)
    qc = lax.broadcasted_iota(jnp.int32, shape, 0)
    lane = lax.broadcasted_iota(jnp.int32, shape, 1)
    left = lane < GRID_W
    kc = jnp.where(left, lane, lane - GRID_W)
    cs = jnp.clip(qc - WIN_W // 2, 0, GRID_W - WIN_W)
    in_win = (kc >= cs) & (kc < cs + WIN_W)
    neg = jnp.full(shape, NEG_INF, F32)
    for hh in range(2):
        def tile(r, j, half, hh=hh):
            row = jnp.broadcast_to(rel_ref[hh, j - r + WIN_H - 1:j - r + WIN_H, :], shape)
            return pltpu.roll(row, half * GRID_W, 1, stride=1, stride_axis=0)

        boff = 0
        for g, (k0, k1) in enumerate(groups):
            for i in range(ROWS_PER_GROUP):
                r = g * ROWS_PER_GROUP + i
                for jp in range((k1 - k0) // 2):
                    j = k0 + 2 * jp
                    a_ok = rs[r] <= j < rs[r] + kh
                    b_ok = rs[r] <= j + 1 < rs[r] + kh
                    if a_ok and b_ok:
                        t = jnp.where(in_win, jnp.where(left, tile(r, j, 0), tile(r, j + 1, 1)), neg)
                    elif a_ok:
                        t = jnp.where(in_win & left, tile(r, j, 0), neg)
                    elif b_ok:
                        t = jnp.where(in_win & jnp.logical_not(left), tile(r, j + 1, 1), neg)
                    else:
                        t = neg
                    bias_ref[hh, i * GRID_W:(i + 1) * GRID_W, boff + jp * LANES:boff + (jp + 1) * LANES] = t
            boff += (k1 - k0) * GRID_W


def _na_attn_kernel(q_ref, k_ref, v_ref, kc_ref, vc_ref, rel_ref, o_ref, bias_ref, *, groups, rs, kh):
    @pl.when(pl.program_id(1) == 0)
    def _():
        _na_build_bias(rel_ref, bias_ref, groups, rs, kh)

    gq = ROWS_PER_GROUP * GRID_W
    lane = lax.broadcasted_iota(jnp.int32, (gq, LANES), 1)
    first = lane < HEAD_DIM

    def pair(ref):
        return jnp.concatenate([ref[0, 0, 0], ref[0, 0, 1]], axis=1).astype(BF16)

    kc = pair(kc_ref)
    vc = pair(vc_ref)
    for g, (k0, k1) in enumerate(groups):
        q = q_ref[g * gq:(g + 1) * gq, :]
        k = k_ref[k0 * GRID_W:k1 * GRID_W, :]
        v = v_ref[k0 * GRID_W:k1 * GRID_W, :]
        boff = sum((b - a) * GRID_W for a, b in groups[:g])
        outs = []
        for hh, sel in enumerate((first, jnp.logical_not(first))):
            qm = jnp.where(sel, q, jnp.zeros_like(q))
            s_win = _dot_nt(qm, k) + bias_ref[hh, :, boff:boff + (k1 - k0) * GRID_W]
            outs.append(_softmax_pv([s_win, _dot_nt(qm, kc)], [v, vc]))
        o_ref[g * gq:(g + 1) * gq, :] = jnp.where(first, outs[0], outs[1]).astype(BF16)


def _na_attention(qkv, kc, vc, rel, first_seq, nb, seq):
    t = nb * seq
    past = kc.shape[3]
    npair = D_ATTN // LANES
    groups, rs, kh = _na_groups(seq // GRID_W)
    width = sum(k1 - k0 for k0, k1 in groups) * GRID_W
    tok = lambda off: pl.BlockSpec((seq, LANES), lambda hp, b: (first_seq + b, off + hp))
    ctx = pl.BlockSpec((1, 1, 2, past, HEAD_DIM), lambda hp, b: (b, 0, hp, 0, 0))
    return pl.pallas_call(
        functools.partial(_na_attn_kernel, groups=groups, rs=[int(x) for x in rs], kh=kh),
        grid=(npair, nb),
        in_specs=[tok(0), tok(npair), tok(2 * npair), ctx, ctx,
                  pl.BlockSpec((2,) + rel.shape[1:], lambda hp, b: (hp, 0, 0))],
        out_specs=pl.BlockSpec((seq, LANES), lambda hp, b: (b, hp)),
        out_shape=jax.ShapeDtypeStruct((t, D_ATTN), BF16),
        scratch_shapes=[pltpu.VMEM((2, ROWS_PER_GROUP * GRID_W, width), F32)],
        compiler_params=_params(2),
        name="na_attention",
    )(qkv, qkv, qkv, kc, vc, rel)


def _gelu_tanh(x):
    c = float(np.sqrt(2.0 / np.pi))
    t = jnp.tanh(x * (c + (c * 0.044715) * (x * x)))
    return x * (0.5 * t + 0.5)


def _lru_kernel(xl_ref, gl_ref, h0_ref, cw_ref, cb_ref, wbd_ref, ba_ref, bi_ref, coef_ref,
                y_ref, hfin_ref, xpad, xc, a_buf, hf_buf, hb_buf, *, seq):
    pitch = seq // N_SEG + SEG_PAD
    lp = N_SEG * pitch
    halo = SUBLANES
    chunk = LANES
    n_chunks = seq // chunk
    left = CONV_W // 2

    for k in range(N_SLABS):
        cols = slice(k * LANES, (k + 1) * LANES)
        xpad[k, 0:halo, :] = jnp.zeros((halo, LANES), F32)
        xpad[k, halo + seq:2 * halo + seq, :] = jnp.zeros((halo, LANES), F32)
        xpad[k, halo:halo + seq, :] = xl_ref[:, cols]

    def conv_body(i, carry):
        r0 = pl.multiple_of(i * chunk, chunk)
        for k in range(N_SLABS):
            cols = slice(k * LANES, (k + 1) * LANES)
            out = cb_ref[:, cols]
            for j in range(CONV_W):
                out = out + xpad[k, pl.ds(r0 + halo + j - left, chunk), :] * cw_ref[j:j + 1, cols]
            xc[k, pl.ds(r0, chunk), :] = out
        return carry

    lax.fori_loop(0, n_chunks, conv_body, 0)

    a_buf[:, seq:lp, :] = jnp.ones((N_SLABS, lp - seq, LANES), F32)
    hf_buf[:, seq:lp, :] = jnp.zeros((N_SLABS, lp - seq, LANES), F32)
    hb_buf[:, seq:lp, :] = jnp.zeros((N_SLABS, lp - seq, LANES), F32)

    row = lax.broadcasted_iota(jnp.int32, (N_SEG, LANES), 0)
    for d, h_buf in enumerate((hf_buf, hb_buf)):
        def gates_body(i, carry, d=d, h_buf=h_buf):
            rows = pl.ds(pl.multiple_of(i * chunk, chunk), chunk)
            per = LRU_TILE // LANES
            for c in range(N_LRU_TILES):
                xs = [xc[c * per + half, rows, :] for half in range(per)]
                pre = _dot(jnp.concatenate(xs, axis=1).astype(BF16), wbd_ref[d, c])
                for half in range(per):
                    k = c * per + half
                    cols = slice(k * LANES, (k + 1) * LANES)
                    tr = jnp.tanh(pre[:, half * LANES:(half + 1) * LANES] + ba_ref[d:d + 1, cols])
                    ti = jnp.tanh(pre[:, LRU_TILE + half * LANES:LRU_TILE + (half + 1) * LANES]
                                  + bi_ref[d:d + 1, cols])
                    a = jnp.exp2(coef_ref[d:d + 1, cols] * tr + coef_ref[d:d + 1, cols])
                    y = (1.0 - a) * (1.0 + a)
                    mult = y * lax.rsqrt(jnp.maximum(y, F32_TINY))
                    a_buf[k, rows, :] = a
                    h_buf[k, rows, :] = mult * ((0.5 * ti + 0.5) * xs[half])
            return carry

        lax.fori_loop(0, n_chunks, gates_body, 0)

        def seg_rows(t, d=d):
            tt = t if d == 0 else pitch - 1 - t
            return pl.ds(tt, N_SEG, stride=pitch)

        def totals_body(t, carry, h_buf=h_buf):
            hs, ps = carry
            idx = seg_rows(t)
            nh, np_ = [], []
            for k in range(N_SLABS):
                a = a_buf[k, idx, :]
                b = h_buf[k, idx, :]
                nh.append(a * hs[k] + b)
                np_.append(ps[k] * a)
            return tuple(nh), tuple(np_)

        zeros = tuple(jnp.zeros((N_SEG, LANES), F32) for _ in range(N_SLABS))
        ones = tuple(jnp.ones((N_SEG, LANES), F32) for _ in range(N_SLABS))
        h_end, p_end = lax.fori_loop(0, pitch, totals_body, (zeros, ones), unroll=SCAN_UNROLL)

        order = range(N_SEG) if d == 0 else range(N_SEG - 1, -1, -1)
        starts = []
        for k in range(N_SLABS):
            c = h0_ref[0, d:d + 1, k * LANES:(k + 1) * LANES]
            cm = jnp.zeros((N_SEG, LANES), F32)
            for s in order:
                cm = jnp.where(row == s, c, cm)
                c = p_end[k][s:s + 1, :] * c + h_end[k][s:s + 1, :]
            starts.append(cm)
            hfin_ref[0, d:d + 1, k * LANES:(k + 1) * LANES] = c

        def scan_body(t, hs, h_buf=h_buf):
            idx = seg_rows(t)
            nh = []
            for k in range(N_SLABS):
                h = a_buf[k, idx, :] * hs[k] + h_buf[k, idx, :]
                h_buf[k, idx, :] = h
                nh.append(h)
            return tuple(nh)

        lax.fori_loop(0, pitch, scan_body, tuple(starts), unroll=SCAN_UNROLL)

    def out_body(i, carry):
        r0 = pl.multiple_of(i * chunk, chunk)
        for k in range(N_SLABS):
            cols = slice(k * LANES, (k + 1) * LANES)
            h = hf_buf[k, pl.ds(r0, chunk), :] + hb_buf[k, pl.ds(r0, chunk), :]
            y_ref[pl.ds(r0, chunk), cols] = (h * _gelu_tanh(gl_ref[pl.ds(r0, chunk), cols])).astype(BF16)
        return carry

    lax.fori_loop(0, n_chunks, out_body, 0)


def _lru(xl, gl, h0, conv_w, conv_b, wbd, ba, bi, coef, first_seq, nb, seq):
    t = nb * seq
    lp = N_SEG * (seq // N_SEG + SEG_PAD)
    tok_in = pl.BlockSpec((seq, D_LRU), lambda b: (first_seq + b, 0))
    tok = pl.BlockSpec((seq, D_LRU), lambda b: (b, 0))
    state = pl.BlockSpec((1, 2, D_LRU), lambda b: (b, 0, 0))
    scan_buf = pltpu.VMEM((N_SLABS, lp, LANES), F32)
    return pl.pallas_call(
        functools.partial(_lru_kernel, seq=seq),
        grid=(nb,),
        in_specs=[tok_in, tok_in, state, _const_spec(conv_w.shape), _const_spec(conv_b.shape),
                  _const_spec(wbd.shape), _const_spec(ba.shape), _const_spec(bi.shape),
                  _const_spec(coef.shape)],
        out_specs=[tok, state],
        out_shape=[jax.ShapeDtypeStruct((t, D_LRU), BF16), jax.ShapeDtypeStruct((nb, 2, D_LRU), F32)],
        scratch_shapes=[pltpu.VMEM((N_SLABS, seq + 2 * SUBLANES, LANES), F32),
                        pltpu.VMEM((N_SLABS, seq, LANES), F32), scan_buf, scan_buf, scan_buf],
        compiler_params=_params(),
        name="rglru",
    )(xl, gl, h0, conv_w, conv_b, wbd, ba, bi, coef)


def _blockdiag_weights(wa, wi):
    per = LRU_TILE // LRU_BLOCK
    out = jnp.zeros((2, N_LRU_TILES, LRU_TILE, 2 * LRU_TILE), BF16)
    for half, w in enumerate((wa, wi)):
        w = (0.5 * w).astype(BF16).reshape(2, N_LRU_TILES, per, LRU_BLOCK, LRU_BLOCK)
        for p in range(per):
            c0 = half * LRU_TILE + p * LRU_BLOCK
            out = out.at[:, :, p * LRU_BLOCK:(p + 1) * LRU_BLOCK, c0:c0 + LRU_BLOCK].set(w[:, :, p])
    return out


def kernel(x_prompt, x_sample, cache_k, cache_v, state_lru, c, c_ctx, w_mod, b_mod, norm_g, ffn1_w_in, ffn1_w_out,
           w_in, rpb, conv_w, conv_b, lru_wa, lru_ba, lru_wi, lru_bi, lru_lambda, w_br_attn, w_br_lru, w_out,
           ffn2_w_in, ffn2_w_out, final_g):
    assert w_mod.shape[0] == 1, "single trunk layer"
    nb_ctx, seq_ctx, _ = x_prompt.shape
    nb_lat, seq_lat, _ = x_sample.shape

    n_cond = 1 + nb_lat
    cond = jnp.concatenate([c_ctx[None, :], c, jnp.zeros((2 * SUBLANES - n_cond, D_MODEL), F32)], axis=0)
    mods3 = _mods(cond, w_mod[0], b_mod[0]).reshape(2 * SUBLANES, N_MOD, D_MODEL)

    wgu1, wo1, win = ffn1_w_in[0].astype(BF16), ffn1_w_out[0].astype(BF16), w_in[0].astype(BF16)
    late_weights = (w_br_attn[0], w_br_lru[0], w_out[0], ffn2_w_in[0], ffn2_w_out[0])
    wbd = _blockdiag_weights(lru_wa[0], lru_wi[0])
    coef = (0.5 * float(np.log2(np.e))) * (-LRU_C * jax.nn.softplus(-lru_lambda[0]))
    ba = 0.5 * lru_ba[0]
    bi = 0.5 * lru_bi[0]
    ng = norm_g[0]
    fg = final_g.reshape(1, D_MODEL)
    cb = conv_b[0].reshape(1, D_LRU)

    t_ctx = nb_ctx * seq_ctx
    assert t_ctx % seq_lat == 0, "latent sequences must start on a latent-sequence-sized row block"
    xp = x_prompt.reshape(t_ctx, D_MODEL)
    xs = x_sample.reshape(nb_lat * seq_lat, D_MODEL)

    x1, qkv, xl, gl, gates, new_k, new_v, wba, wbl, wout, wgu2, wo2 = _ffn1(
        xp, xs, mods3, ng, wgu1, wo1, win, late_weights, nb_ctx, seq_ctx, seq_lat)

    attnp = _ctx_attention(qkv, nb_ctx, seq_ctx)
    lrup, h_fin = _lru(xl, gl, jnp.zeros((nb_ctx, 2, D_LRU), F32), conv_w[0], cb, wbd,
                       ba, bi, coef, 0, nb_ctx, seq_ctx)

    rel = _na_rel_rows(rpb[0])
    first_lat = t_ctx // seq_lat
    attns = _na_attention(qkv, cache_k, cache_v, rel, first_lat, nb_lat, seq_lat)
    lrus, _ = _lru(xl, gl, state_lru[:, 0], conv_w[0], cb, wbd, ba, bi, coef,
                   first_lat, nb_lat, seq_lat)

    yp, ys = _ffn2(x1, attnp, attns, lrup, lrus, gates, mods3, ng, fg, wba, wbl, wout, wgu2, wo2, 512, seq_lat)
    return (yp.reshape(nb_ctx, seq_ctx, D_MODEL), ys.reshape(nb_lat, seq_lat, D_MODEL),
            new_k, new_v, h_fin[:, None])
```

```python
import functools

import numpy as np
import jax
import jax.numpy as jnp
from jax import lax
from jax.experimental import pallas as pl
from jax.experimental.pallas import tpu as pltpu

F32 = jnp.float32
BF16 = jnp.bfloat16

D_MODEL = 1024
N_HEADS = 8
HEAD_DIM = 64
D_ATTN = N_HEADS * HEAD_DIM
GRID_W = 64
WIN_H = 8
WIN_W = 16
D_LRU = 1024
LRU_BLOCKS = 16
LRU_BLOCK = D_LRU // LRU_BLOCKS
CONV_W = 4
LRU_C = 8.0
D_FF = 2816
N_MOD = 9
EPS = 1e-6
NEG_INF = -1e30
F32_TINY = float(np.finfo(np.float32).tiny)
ATTN_SCALE = HEAD_DIM ** -0.5

LANES = 128
SUBLANES = 8
MXU_DIM = 256
VMEM_LIMIT_BYTES = 56 * 1024 * 1024

FF_CHUNK = MXU_DIM
N_FF_CHUNKS = D_FF // FF_CHUNK
LRU_TILE = MXU_DIM
N_LRU_TILES = D_LRU // LRU_TILE
N_SLABS = D_LRU // LANES
N_SEG = SUBLANES
SEG_PAD = 4
SCAN_UNROLL = 4
ROWS_PER_GROUP = 4


def _sigmoid(x):
    return 0.5 * jnp.tanh(0.5 * x) + 0.5


def _rms_mod(x, g, shift, scale):
    y = x * lax.rsqrt(jnp.mean(x * x, axis=-1, keepdims=True) + EPS)
    y = y * g
    return y * (1 + scale) + shift


def _dot(a, b):
    return jnp.dot(a, b, preferred_element_type=F32)


def _dot_nt(a, b):
    return lax.dot_general(a, b, (((1,), (1,)), ((), ())), preferred_element_type=F32)


def _params(n_axes=1):
    return pltpu.CompilerParams(dimension_semantics=("arbitrary",) * n_axes,
                                vmem_limit_bytes=VMEM_LIMIT_BYTES)


def _const_spec(shape):
    nd = len(shape)
    return pl.BlockSpec(shape, lambda *_: (0,) * nd, pipeline_mode=pl.Buffered(1))


def _mods_kernel(cond_ref, w_ref, b_ref, o_ref):
    c = cond_ref[...]
    s = (c * _sigmoid(c)).astype(BF16)
    o_ref[...] = _dot(s, w_ref[...].astype(BF16)) + b_ref[...]


def _mods(cond, w_mod, b_mod):
    n = w_mod.shape[1]
    bn = D_MODEL
    return pl.pallas_call(
        _mods_kernel,
        grid=(n // bn,),
        in_specs=[pl.BlockSpec(cond.shape, lambda j: (0, 0)),
                  pl.BlockSpec((D_MODEL, bn), lambda j: (0, j)),
                  pl.BlockSpec((1, bn), lambda j: (0, j))],
        out_specs=pl.BlockSpec((cond.shape[0], bn), lambda j: (0, j)),
        out_shape=jax.ShapeDtypeStruct((cond.shape[0], n), F32),
        compiler_params=_params(),
        name="mods",
    )(cond, w_mod, b_mod.reshape(1, n))


def _swiglu(h_ref, a_ref, wgu_ref, wo_ref):
    h = h_ref[...]
    for j in range(N_FF_CHUNKS):
        cols = slice(j * FF_CHUNK, (j + 1) * FF_CHUNK)
        g = _dot(h, wgu_ref[:, cols])
        u = _dot(h, wgu_ref[:, D_FF + j * FF_CHUNK:D_FF + (j + 1) * FF_CHUNK])
        a_ref[:, cols] = ((g * _sigmoid(g)) * u).astype(BF16)
    return _dot(a_ref[...], wo_ref[...])


BF16_ROWS = 2 * SUBLANES


def _cast_spec(w, n_steps):
    rows = BF16_ROWS * pl.cdiv(w.shape[0], BF16_ROWS * n_steps)
    n_chunks = w.shape[0] // rows
    assert n_chunks * rows == w.shape[0] and n_chunks <= n_steps
    return pl.BlockSpec((rows, w.shape[1]), lambda i: (jnp.minimum(i, n_chunks - 1), 0))


def _two_stream(n_first, tm_rows):
    first = lambda w: pl.BlockSpec((tm_rows, w), lambda i: (jnp.minimum(i, n_first - 1), 0))
    second = lambda w: pl.BlockSpec((tm_rows, w), lambda i: (jnp.maximum(i - n_first, 0), 0))
    return first, second


def _mod_spec(n_ctx_tiles, tm, seq_lat):
    def index(i):
        return (jnp.where(i < n_ctx_tiles, 0, 1 + (jnp.maximum(i - n_ctx_tiles, 0) * tm) // seq_lat), 0, 0)
    return pl.BlockSpec((1, N_MOD, D_MODEL), index)


N_LATE_WEIGHTS = 5


def _ffn1_kernel(xp_ref, xs_ref, mod_ref, ng_ref, wgu_ref, wo_ref, win_ref, *rest, n_ctx_tiles):
    late_f32 = rest[:N_LATE_WEIGHTS]
    x1_ref, qkv_ref, xl_ref, gl_ref, gates_ref, newk_ref, newv_ref = rest[N_LATE_WEIGHTS:N_LATE_WEIGHTS + 7]
    late_bf16 = rest[N_LATE_WEIGHTS + 7:2 * N_LATE_WEIGHTS + 7]
    a_ref, h_ref = rest[2 * N_LATE_WEIGHTS + 7:]

    for src, dst in zip(late_f32, late_bf16):
        dst[...] = src[...].astype(BF16)

    is_ctx = pl.program_id(0) < n_ctx_tiles
    x = jnp.where(is_ctx, xp_ref[...], xs_ref[...])
    m = mod_ref[0]
    h_ref[...] = _rms_mod(x, ng_ref[0:1], m[0:1], m[1:2]).astype(BF16)
    x1 = x + (0.5 * m[2:3]) * _swiglu(h_ref, a_ref, wgu_ref, wo_ref)
    x1_ref[...] = x1
    h2 = _rms_mod(x1, ng_ref[1:2], m[3:4], m[4:5]).astype(BF16)
    c0, c1, c2, c3 = 3 * D_ATTN, 3 * D_ATTN + D_LRU, 3 * D_ATTN + 2 * D_LRU, 3 * D_ATTN + 2 * D_LRU + 2 * D_MODEL
    qkv = _dot(h2, win_ref[:, 0:c0])
    qkv_ref[:, :D_ATTN] = (qkv[:, :D_ATTN] * ATTN_SCALE).astype(BF16)
    qkv_ref[:, D_ATTN:] = qkv[:, D_ATTN:].astype(BF16)

    @pl.when(is_ctx)
    def _():
        for h in range(N_HEADS):
            newk_ref[0, 0, h] = qkv[:, D_ATTN + h * HEAD_DIM:D_ATTN + (h + 1) * HEAD_DIM]
            newv_ref[0, 0, h] = qkv[:, 2 * D_ATTN + h * HEAD_DIM:2 * D_ATTN + (h + 1) * HEAD_DIM]

    xl_ref[...] = _dot(h2, win_ref[:, c0:c1])
    gl_ref[...] = _dot(h2, win_ref[:, c1:c2])
    gates_ref[...] = _dot(h2, win_ref[:, c2:c3])


def _ffn1(xp, xs, mods3, norm_g, wgu, wo, win, late_weights, nb_ctx, seq_ctx, seq_lat):
    tm = seq_ctx
    n_ctx_tiles = xp.shape[0] // tm
    t = xp.shape[0] + xs.shape[0]
    n_steps = t // tm
    assert len(late_weights) == N_LATE_WEIGHTS
    row = lambda w: pl.BlockSpec((tm, w), lambda i: (i, 0))
    first, second = _two_stream(n_ctx_tiles, tm)
    cache_spec = pl.BlockSpec((1, 1, N_HEADS, seq_ctx, HEAD_DIM),
                              lambda i: (jnp.minimum(i, n_ctx_tiles - 1), 0, 0, 0, 0))
    cache_shape = jax.ShapeDtypeStruct((nb_ctx, 1, N_HEADS, seq_ctx, HEAD_DIM), F32)
    cast_specs = [_cast_spec(w, n_steps) for w in late_weights]
    return pl.pallas_call(
        functools.partial(_ffn1_kernel, n_ctx_tiles=n_ctx_tiles),
        grid=(n_steps,),
        in_specs=[first(D_MODEL), second(D_MODEL), _mod_spec(n_ctx_tiles, tm, seq_lat),
                  _const_spec(norm_g.shape), _const_spec(wgu.shape), _const_spec(wo.shape),
                  _const_spec(win.shape)] + cast_specs,
        out_specs=[row(D_MODEL), row(3 * D_ATTN), row(D_LRU), row(D_LRU), row(2 * D_MODEL),
                   cache_spec, cache_spec] + cast_specs,
        out_shape=[jax.ShapeDtypeStruct((t, D_MODEL), F32),
                   jax.ShapeDtypeStruct((t, 3 * D_ATTN), BF16),
                   jax.ShapeDtypeStruct((t, D_LRU), F32),
                   jax.ShapeDtypeStruct((t, D_LRU), F32),
                   jax.ShapeDtypeStruct((t, 2 * D_MODEL), F32),
                   cache_shape, cache_shape] + [jax.ShapeDtypeStruct(w.shape, BF16) for w in late_weights],
        scratch_shapes=[pltpu.VMEM((tm, D_FF), BF16), pltpu.VMEM((tm, D_MODEL), BF16)],
        compiler_params=_params(),
        name="ffn1_inproj",
    )(xp, xs, mods3, norm_g, wgu, wo, win, *late_weights)


def _ffn2_kernel(x1_ref, attnp_ref, attns_ref, lrup_ref, lrus_ref, gates_ref, mod_ref, ng_ref, fg_ref,
                 wba_ref, wbl_ref, wout_ref, wgu_ref, wo_ref, yp_ref, ys_ref, a_ref, h_ref, *, n_ctx_tiles):
    is_ctx = pl.program_id(0) < n_ctx_tiles
    attn = jnp.where(is_ctx, attnp_ref[...], attns_ref[...])
    lru = jnp.where(is_ctx, lrup_ref[...], lrus_ref[...])
    m = mod_ref[0]
    gt = gates_ref[...]
    mm = (_sigmoid(gt[:, :D_MODEL]) * _dot(attn, wba_ref[...])
          + _sigmoid(gt[:, D_MODEL:]) * _dot(lru, wbl_ref[...]))
    x2 = x1_ref[...] + m[5:6] * _dot(mm.astype(BF16), wout_ref[...])
    h_ref[...] = _rms_mod(x2, ng_ref[2:3], m[6:7], m[7:8]).astype(BF16)
    x3 = x2 + (0.5 * m[8:9]) * _swiglu(h_ref, a_ref, wgu_ref, wo_ref)
    y = x3 * lax.rsqrt(jnp.mean(x3 * x3, axis=-1, keepdims=True) + EPS)
    y = y * fg_ref[...]

    @pl.when(is_ctx)
    def _():
        yp_ref[...] = y

    @pl.when(jnp.logical_not(is_ctx))
    def _():
        ys_ref[...] = y


def _ffn2(x1, attnp, attns, lrup, lrus, gates, mods3, norm_g, final_g, wba, wbl, wout, wgu, wo, tm, seq_lat):
    t = x1.shape[0]
    t_ctx = attnp.shape[0]
    n_ctx_tiles = t_ctx // tm
    row = lambda w: pl.BlockSpec((tm, w), lambda i: (i, 0))
    first, second = _two_stream(n_ctx_tiles, tm)
    return pl.pallas_call(
        functools.partial(_ffn2_kernel, n_ctx_tiles=n_ctx_tiles),
        grid=(t // tm,),
        in_specs=[row(D_MODEL), first(D_ATTN), second(D_ATTN), first(D_LRU), second(D_LRU), row(2 * D_MODEL),
                  _mod_spec(n_ctx_tiles, tm, seq_lat),
                  _const_spec(norm_g.shape), _const_spec(final_g.shape),
                  _const_spec(wba.shape), _const_spec(wbl.shape), _const_spec(wout.shape),
                  _const_spec(wgu.shape), _const_spec(wo.shape)],
        out_specs=[first(D_MODEL), second(D_MODEL)],
        out_shape=[jax.ShapeDtypeStruct((t_ctx, D_MODEL), F32), jax.ShapeDtypeStruct((t - t_ctx, D_MODEL), F32)],
        scratch_shapes=[pltpu.VMEM((tm, D_FF), BF16), pltpu.VMEM((tm, D_MODEL), BF16)],
        compiler_params=_params(),
        name="merge_ffn2",
    )(x1, attnp, attns, lrup, lrus, gates, mods3, norm_g, final_g, wba, wbl, wout, wgu, wo)


def _softmax_pv(scores, values):
    m = scores[0].max(axis=-1, keepdims=True)
    for s in scores[1:]:
        m = jnp.maximum(m, s.max(axis=-1, keepdims=True))
    l = None
    o = None
    for s, v in zip(scores, values):
        p = jnp.exp(s - m)
        ls = p.sum(axis=-1, keepdims=True)
        os_ = _dot(p.astype(BF16), v)
        l = ls if l is None else l + ls
        o = os_ if o is None else o + os_
    return o / l


def _ctx_attn_kernel(qkv_ref, o_ref):
    seq = qkv_ref.shape[0]
    lane = lax.broadcasted_iota(jnp.int32, (seq, LANES), 1)
    first = lane < HEAD_DIM
    for hp in range(D_ATTN // LANES):
        q = qkv_ref[:, hp * LANES:(hp + 1) * LANES]
        k = qkv_ref[:, D_ATTN + hp * LANES:D_ATTN + (hp + 1) * LANES]
        v = qkv_ref[:, 2 * D_ATTN + hp * LANES:2 * D_ATTN + (hp + 1) * LANES]
        qm = jnp.concatenate([jnp.where(first, q, jnp.zeros_like(q)), jnp.where(first, jnp.zeros_like(q), q)], axis=0)
        o = _softmax_pv([_dot_nt(qm, k)], [v])
        o_ref[:, hp * LANES:(hp + 1) * LANES] = jnp.where(first, o[:seq], o[seq:]).astype(BF16)


def _ctx_attention(qkv, nb, seq):
    return pl.pallas_call(
        _ctx_attn_kernel,
        grid=(nb,),
        in_specs=[pl.BlockSpec((seq, 3 * D_ATTN), lambda b: (b, 0))],
        out_specs=pl.BlockSpec((seq, D_ATTN), lambda b: (b, 0)),
        out_shape=jax.ShapeDtypeStruct((nb * seq, D_ATTN), BF16),
        compiler_params=_params(),
        name="ctx_attention",
    )(qkv)


def _na_groups(rows):
    kh = min(WIN_H, rows)
    r = np.arange(rows)
    rs = np.clip(r - kh // 2, 0, rows - kh)
    groups = []
    for g in range(rows // ROWS_PER_GROUP):
        qr = r[g * ROWS_PER_GROUP:(g + 1) * ROWS_PER_GROUP]
        k0, k1 = int(rs[qr].min()), int(rs[qr].max()) + kh
        if (k1 - k0) % 2:
            if k1 < rows:
                k1 += 1
            else:
                k0 -= 1
        groups.append((k0, k1))
    return groups, rs, kh


def _na_rel_rows(rpb):
    edge = GRID_W - WIN_W
    v = jnp.concatenate([rpb[..., WIN_W - 1:],
                         jnp.repeat(rpb[..., -1:], edge, axis=-1),
                         jnp.zeros(rpb.shape[:-1] + (1,), F32),
                         jnp.repeat(rpb[..., :1], edge, axis=-1),
                         rpb[..., :WIN_W - 1]], axis=-1).astype(F32)
    return jnp.pad(v, ((0, 0), (0, 1), (0, 0)))


def _na_build_bias(rel_ref, bias_ref, groups, rs, kh):
    shape = (GRID_W, LANES)
    qc = lax.broadcasted_iota(jnp.int32, shape, 0)
    lane = lax.broadcasted_iota(jnp.int32, shape, 1)
    left = lane < GRID_W
    kc = jnp.where(left, lane, lane - GRID_W)
    cs = jnp.clip(qc - WIN_W // 2, 0, GRID_W - WIN_W)
    in_win = (kc >= cs) & (kc < cs + WIN_W)
    neg = jnp.full(shape, NEG_INF, F32)
    for hh in range(2):
        def tile(r, j, half, hh=hh):
            row = jnp.broadcast_to(rel_ref[hh, j - r + WIN_H - 1:j - r + WIN_H, :], shape)
            return pltpu.roll(row, half * GRID_W, 1, stride=1, stride_axis=0)

        boff = 0
        for g, (k0, k1) in enumerate(groups):
            for i in range(ROWS_PER_GROUP):
                r = g * ROWS_PER_GROUP + i
                for jp in range((k1 - k0) // 2):
                    j = k0 + 2 * jp
                    a_ok = rs[r] <= j < rs[r] + kh
                    b_ok = rs[r] <= j + 1 < rs[r] + kh
                    if a_ok and b_ok:
                        t = jnp.where(in_win, jnp.where(left, tile(r, j, 0), tile(r, j + 1, 1)), neg)
                    elif a_ok:
                        t = jnp.where(in_win & left, tile(r, j, 0), neg)
                    elif b_ok:
                        t = jnp.where(in_win & jnp.logical_not(left), tile(r, j + 1, 1), neg)
                    else:
                        t = neg
                    bias_ref[hh, i * GRID_W:(i + 1) * GRID_W, boff + jp * LANES:boff + (jp + 1) * LANES] = t
            boff += (k1 - k0) * GRID_W


def _na_attn_kernel(q_ref, k_ref, v_ref, kc_ref, vc_ref, rel_ref, o_ref, bias_ref, *, groups, rs, kh):
    @pl.when(pl.program_id(1) == 0)
    def _():
        _na_build_bias(rel_ref, bias_ref, groups, rs, kh)

    gq = ROWS_PER_GROUP * GRID_W
    lane = lax.broadcasted_iota(jnp.int32, (gq, LANES), 1)
    first = lane < HEAD_DIM

    def pair(ref):
        return jnp.concatenate([ref[0, 0, 0], ref[0, 0, 1]], axis=1).astype(BF16)

    kc = pair(kc_ref)
    vc = pair(vc_ref)
    for g, (k0, k1) in enumerate(groups):
        q = q_ref[g * gq:(g + 1) * gq, :]
        k = k_ref[k0 * GRID_W:k1 * GRID_W, :]
        v = v_ref[k0 * GRID_W:k1 * GRID_W, :]
        boff = sum((b - a) * GRID_W for a, b in groups[:g])
        qm = jnp.concatenate([jnp.where(first, q, jnp.zeros_like(q)), jnp.where(first, jnp.zeros_like(q), q)], axis=0)
        bias = jnp.concatenate([bias_ref[hh, :, boff:boff + (k1 - k0) * GRID_W] for hh in range(2)], axis=0)
        o = _softmax_pv([_dot_nt(qm, k) + bias, _dot_nt(qm, kc)], [v, vc])
        o_ref[g * gq:(g + 1) * gq, :] = jnp.where(first, o[:gq], o[gq:]).astype(BF16)


def _na_attention(qkv, kc, vc, rel, first_seq, nb, seq):
    t = nb * seq
    past = kc.shape[3]
    npair = D_ATTN // LANES
    groups, rs, kh = _na_groups(seq // GRID_W)
    width = sum(k1 - k0 for k0, k1 in groups) * GRID_W
    tok = lambda off: pl.BlockSpec((seq, LANES), lambda hp, b: (first_seq + b, off + hp))
    ctx = pl.BlockSpec((1, 1, 2, past, HEAD_DIM), lambda hp, b: (b, 0, hp, 0, 0))
    return pl.pallas_call(
        functools.partial(_na_attn_kernel, groups=groups, rs=[int(x) for x in rs], kh=kh),
        grid=(npair, nb),
        in_specs=[tok(0), tok(npair), tok(2 * npair), ctx, ctx,
                  pl.BlockSpec((2,) + rel.shape[1:], lambda hp, b: (hp, 0, 0))],
        out_specs=pl.BlockSpec((seq, LANES), lambda hp, b: (b, hp)),
        out_shape=jax.ShapeDtypeStruct((t, D_ATTN), BF16),
        scratch_shapes=[pltpu.VMEM((2, ROWS_PER_GROUP * GRID_W, width), F32)],
        compiler_params=_params(2),
        name="na_attention",
    )(qkv, qkv, qkv, kc, vc, rel)


def _gelu_tanh(x):
    c = float(np.sqrt(2.0 / np.pi))
    t = jnp.tanh(x * (c + (c * 0.044715) * (x * x)))
    return x * (0.5 * t + 0.5)


def _build_blockdiag(wa_ref, wi_ref, wbd_ref):
    per = LRU_TILE // LRU_BLOCK
    wbd_ref[...] = jnp.zeros(wbd_ref.shape, BF16)
    zero = jnp.zeros((LRU_BLOCK, LRU_BLOCK), F32)
    for d in range(2):
        for t in range(N_LRU_TILES):
            for p in range(per):
                rows = slice(p * LRU_BLOCK, (p + 1) * LRU_BLOCK)
                for half, w_ref in enumerate((wa_ref, wi_ref)):
                    blk = 0.5 * w_ref[d, t * per + p]
                    piece = jnp.concatenate([blk, zero] if p % 2 == 0 else [zero, blk], axis=1)
                    c0 = half * LRU_TILE + (p // 2) * LANES
                    wbd_ref[d, t, rows, c0:c0 + LANES] = piece.astype(BF16)


def _lru_kernel(xl_ref, gl_ref, h0_ref, cw_ref, cb_ref, wa_ref, wi_ref, ba_ref, bi_ref, coef_ref,
                y_ref, hfin_ref, xpad, xc, a_buf, hf_buf, hb_buf, wbd_ref, *, seq):
    pitch = seq // N_SEG + SEG_PAD
    lp = N_SEG * pitch
    halo = SUBLANES
    chunk = LANES
    n_chunks = seq // chunk
    left = CONV_W // 2

    @pl.when(pl.program_id(0) == 0)
    def _():
        _build_blockdiag(wa_ref, wi_ref, wbd_ref)

    for k in range(N_SLABS):
        cols = slice(k * LANES, (k + 1) * LANES)
        xpad[k, 0:halo, :] = jnp.zeros((halo, LANES), F32)
        xpad[k, halo + seq:2 * halo + seq, :] = jnp.zeros((halo, LANES), F32)
        xpad[k, halo:halo + seq, :] = xl_ref[:, cols]

    def conv_body(i, carry):
        r0 = pl.multiple_of(i * chunk, chunk)
        for k in range(N_SLABS):
            cols = slice(k * LANES, (k + 1) * LANES)
            out = cb_ref[:, cols]
            for j in range(CONV_W):
                out = out + xpad[k, pl.ds(r0 + halo + j - left, chunk), :] * cw_ref[j:j + 1, cols]
            xc[k, pl.ds(r0, chunk), :] = out
        return carry

    lax.fori_loop(0, n_chunks, conv_body, 0)

    a_buf[:, seq:lp, :] = jnp.ones((N_SLABS, lp - seq, LANES), F32)
    hf_buf[:, seq:lp, :] = jnp.zeros((N_SLABS, lp - seq, LANES), F32)
    hb_buf[:, seq:lp, :] = jnp.zeros((N_SLABS, lp - seq, LANES), F32)

    row = lax.broadcasted_iota(jnp.int32, (N_SEG, LANES), 0)
    for d, h_buf in enumerate((hf_buf, hb_buf)):
        def gates_body(i, carry, d=d, h_buf=h_buf):
            rows = pl.ds(pl.multiple_of(i * chunk, chunk), chunk)
            per = LRU_TILE // LANES
            for c in range(N_LRU_TILES):
                xs = [xc[c * per + half, rows, :] for half in range(per)]
                pre = _dot(jnp.concatenate(xs, axis=1).astype(BF16), wbd_ref[d, c])
                for half in range(per):
                    k = c * per + half
                    cols = slice(k * LANES, (k + 1) * LANES)
                    tr = jnp.tanh(pre[:, half * LANES:(half + 1) * LANES] + ba_ref[d:d + 1, cols])
                    ti = jnp.tanh(pre[:, LRU_TILE + half * LANES:LRU_TILE + (half + 1) * LANES]
                                  + bi_ref[d:d + 1, cols])
                    a = jnp.exp2(coef_ref[d:d + 1, cols] * tr + coef_ref[d:d + 1, cols])
                    y = (1.0 - a) * (1.0 + a)
                    mult = y * lax.rsqrt(jnp.maximum(y, F32_TINY))
                    a_buf[k, rows, :] = a
                    h_buf[k, rows, :] = mult * ((0.5 * ti + 0.5) * xs[half])
            return carry

        lax.fori_loop(0, n_chunks, gates_body, 0)

        def seg_rows(t, d=d):
            tt = t if d == 0 else pitch - 1 - t
            return pl.ds(tt, N_SEG, stride=pitch)

        def totals_body(t, carry, h_buf=h_buf):
            hs, ps = carry
            idx = seg_rows(t)
            nh, np_ = [], []
            for k in range(N_SLABS):
                a = a_buf[k, idx, :]
                b = h_buf[k, idx, :]
                nh.append(a * hs[k] + b)
                np_.append(ps[k] * a)
            return tuple(nh), tuple(np_)

        zeros = tuple(jnp.zeros((N_SEG, LANES), F32) for _ in range(N_SLABS))
        ones = tuple(jnp.ones((N_SEG, LANES), F32) for _ in range(N_SLABS))
        h_end, p_end = lax.fori_loop(0, pitch, totals_body, (zeros, ones), unroll=SCAN_UNROLL)

        order = range(N_SEG) if d == 0 else range(N_SEG - 1, -1, -1)
        starts = []
        for k in range(N_SLABS):
            c = h0_ref[0, d:d + 1, k * LANES:(k + 1) * LANES]
            cm = jnp.zeros((N_SEG, LANES), F32)
            for s in order:
                cm = jnp.where(row == s, c, cm)
                c = p_end[k][s:s + 1, :] * c + h_end[k][s:s + 1, :]
            starts.append(cm)
            hfin_ref[0, d:d + 1, k * LANES:(k + 1) * LANES] = c

        def scan_body(t, hs, h_buf=h_buf):
            idx = seg_rows(t)
            nh = []
            for k in range(N_SLABS):
                h = a_buf[k, idx, :] * hs[k] + h_buf[k, idx, :]
                h_buf[k, idx, :] = h
                nh.append(h)
            return tuple(nh)

        lax.fori_loop(0, pitch, scan_body, tuple(starts), unroll=SCAN_UNROLL)

    def out_body(i, carry):
        r0 = pl.multiple_of(i * chunk, chunk)
        for k in range(N_SLABS):
            cols = slice(k * LANES, (k + 1) * LANES)
            h = hf_buf[k, pl.ds(r0, chunk), :] + hb_buf[k, pl.ds(r0, chunk), :]
            y_ref[pl.ds(r0, chunk), cols] = (h * _gelu_tanh(gl_ref[pl.ds(r0, chunk), cols])).astype(BF16)
        return carry

    lax.fori_loop(0, n_chunks, out_body, 0)


def _lru(xl, gl, h0, conv_w, conv_b, wa, wi, ba, bi, coef, first_seq, nb, seq):
    t = nb * seq
    lp = N_SEG * (seq // N_SEG + SEG_PAD)
    tok_in = pl.BlockSpec((seq, D_LRU), lambda b: (first_seq + b, 0))
    tok = pl.BlockSpec((seq, D_LRU), lambda b: (b, 0))
    state = pl.BlockSpec((1, 2, D_LRU), lambda b: (b, 0, 0))
    scan_buf = pltpu.VMEM((N_SLABS, lp, LANES), F32)
    return pl.pallas_call(
        functools.partial(_lru_kernel, seq=seq),
        grid=(nb,),
        in_specs=[tok_in, tok_in, state, _const_spec(conv_w.shape), _const_spec(conv_b.shape),
                  _const_spec(wa.shape), _const_spec(wi.shape), _const_spec(ba.shape), _const_spec(bi.shape),
                  _const_spec(coef.shape)],
        out_specs=[tok, state],
        out_shape=[jax.ShapeDtypeStruct((t, D_LRU), BF16), jax.ShapeDtypeStruct((nb, 2, D_LRU), F32)],
        scratch_shapes=[pltpu.VMEM((N_SLABS, seq + 2 * SUBLANES, LANES), F32),
                        pltpu.VMEM((N_SLABS, seq, LANES), F32), scan_buf, scan_buf, scan_buf,
                        pltpu.VMEM((2, N_LRU_TILES, LRU_TILE, 2 * LRU_TILE), BF16)],
        compiler_params=_params(),
        name="rglru",
    )(xl, gl, h0, conv_w, conv_b, wa, wi, ba, bi, coef)


def kernel(x_prompt, x_sample, cache_k, cache_v, state_lru, c, c_ctx, w_mod, b_mod, norm_g, ffn1_w_in, ffn1_w_out,
           w_in, rpb, conv_w, conv_b, lru_wa, lru_ba, lru_wi, lru_bi, lru_lambda, w_br_attn, w_br_lru, w_out,
           ffn2_w_in, ffn2_w_out, final_g):
    assert w_mod.shape[0] == 1, "single trunk layer"
    nb_ctx, seq_ctx, _ = x_prompt.shape
    nb_lat, seq_lat, _ = x_sample.shape

    n_cond = 1 + nb_lat
    cond = jnp.concatenate([c_ctx[None, :], c, jnp.zeros((2 * SUBLANES - n_cond, D_MODEL), F32)], axis=0)
    mods3 = _mods(cond, w_mod[0], b_mod[0]).reshape(2 * SUBLANES, N_MOD, D_MODEL)

    wgu1, wo1, win = ffn1_w_in[0].astype(BF16), ffn1_w_out[0].astype(BF16), w_in[0].astype(BF16)
    late_weights = (w_br_attn[0], w_br_lru[0], w_out[0], ffn2_w_in[0], ffn2_w_out[0])
    coef = (0.5 * float(np.log2(np.e))) * (-LRU_C * jax.nn.softplus(-lru_lambda[0]))
    ba = 0.5 * lru_ba[0]
    bi = 0.5 * lru_bi[0]
    ng = norm_g[0]
    fg = final_g.reshape(1, D_MODEL)
    cb = conv_b[0].reshape(1, D_LRU)

    t_ctx = nb_ctx * seq_ctx
    assert t_ctx % seq_lat == 0, "latent sequences must start on a latent-sequence-sized row block"
    xp = x_prompt.reshape(t_ctx, D_MODEL)
    xs = x_sample.reshape(nb_lat * seq_lat, D_MODEL)

    x1, qkv, xl, gl, gates, new_k, new_v, wba, wbl, wout, wgu2, wo2 = _ffn1(
        xp, xs, mods3, ng, wgu1, wo1, win, late_weights, nb_ctx, seq_ctx, seq_lat)

    attnp = _ctx_attention(qkv, nb_ctx, seq_ctx)
    lrup, h_fin = _lru(xl, gl, jnp.zeros((nb_ctx, 2, D_LRU), F32), conv_w[0], cb, lru_wa[0], lru_wi[0],
                       ba, bi, coef, 0, nb_ctx, seq_ctx)

    rel = _na_rel_rows(rpb[0])
    first_lat = t_ctx // seq_lat
    attns = _na_attention(qkv, cache_k, cache_v, rel, first_lat, nb_lat, seq_lat)
    lrus, _ = _lru(xl, gl, state_lru[:, 0], conv_w[0], cb, lru_wa[0], lru_wi[0], ba, bi, coef,
                   first_lat, nb_lat, seq_lat)

    yp, ys = _ffn2(x1, attnp, attns, lrup, lrus, gates, mods3, ng, fg, wba, wbl, wout, wgu2, wo2, 512, seq_lat)
    return (yp.reshape(nb_ctx, seq_ctx, D_MODEL), ys.reshape(nb_lat, seq_lat, D_MODEL),
            new_k, new_v, h_fin[:, None])
```

```python
import functools

import numpy as np
import jax
import jax.numpy as jnp
from jax import lax
from jax.experimental import pallas as pl
from jax.experimental.pallas import tpu as pltpu

F32 = jnp.float32
BF16 = jnp.bfloat16

D_MODEL = 1024
N_HEADS = 8
HEAD_DIM = 64
D_ATTN = N_HEADS * HEAD_DIM
GRID_W = 64
WIN_H = 8
WIN_W = 16
D_LRU = 1024
LRU_BLOCKS = 16
LRU_BLOCK = D_LRU // LRU_BLOCKS
CONV_W = 4
LRU_C = 8.0
D_FF = 2816
N_MOD = 9
EPS = 1e-6
NEG_INF = -1e30
F32_TINY = float(np.finfo(np.float32).tiny)
ATTN_SCALE = HEAD_DIM ** -0.5

LANES = 128
SUBLANES = 8
MXU_DIM = 256
VMEM_LIMIT_BYTES = 56 * 1024 * 1024

FF_CHUNK = MXU_DIM
N_FF_CHUNKS = D_FF // FF_CHUNK
LRU_TILE = MXU_DIM
N_LRU_TILES = D_LRU // LRU_TILE
N_SLABS = D_LRU // LANES
N_SEG = SUBLANES
SEG_PAD = 4
SCAN_UNROLL = 4
ROWS_PER_GROUP = 4


def _sigmoid(x):
    return 0.5 * jnp.tanh(0.5 * x) + 0.5


def _rms_mod(x, g, shift, scale):
    y = x * lax.rsqrt(jnp.mean(x * x, axis=-1, keepdims=True) + EPS)
    y = y * g
    return y * (1 + scale) + shift


def _dot(a, b):
    return jnp.dot(a, b, preferred_element_type=F32)


def _dot_nt(a, b):
    return lax.dot_general(a, b, (((1,), (1,)), ((), ())), preferred_element_type=F32)


def _params(n_axes=1):
    return pltpu.CompilerParams(dimension_semantics=("arbitrary",) * n_axes,
                                vmem_limit_bytes=VMEM_LIMIT_BYTES)


def _const_spec(shape):
    nd = len(shape)
    return pl.BlockSpec(shape, lambda *_: (0,) * nd, pipeline_mode=pl.Buffered(1))


def _mods_kernel(cond_ref, w_ref, b_ref, o_ref):
    c = cond_ref[...]
    s = (c * _sigmoid(c)).astype(BF16)
    o_ref[...] = _dot(s, w_ref[...].astype(BF16)) + b_ref[...]


def _mods(cond, w_mod, b_mod):
    n = w_mod.shape[1]
    bn = D_MODEL
    return pl.pallas_call(
        _mods_kernel,
        grid=(n // bn,),
        in_specs=[pl.BlockSpec(cond.shape, lambda j: (0, 0)),
                  pl.BlockSpec((D_MODEL, bn), lambda j: (0, j)),
                  pl.BlockSpec((1, bn), lambda j: (0, j))],
        out_specs=pl.BlockSpec((cond.shape[0], bn), lambda j: (0, j)),
        out_shape=jax.ShapeDtypeStruct((cond.shape[0], n), F32),
        compiler_params=_params(),
        name="mods",
    )(cond, w_mod, b_mod.reshape(1, n))


def _swiglu(h_ref, a_ref, wgu_ref, wo_ref):
    h = h_ref[...]
    for j in range(N_FF_CHUNKS):
        cols = slice(j * FF_CHUNK, (j + 1) * FF_CHUNK)
        g = _dot(h, wgu_ref[:, cols])
        u = _dot(h, wgu_ref[:, D_FF + j * FF_CHUNK:D_FF + (j + 1) * FF_CHUNK])
        a_ref[:, cols] = ((g * _sigmoid(g)) * u).astype(BF16)
    return _dot(a_ref[...], wo_ref[...])


BF16_ROWS = 2 * SUBLANES


def _cast_spec(w, n_steps):
    rows = BF16_ROWS * pl.cdiv(w.shape[0], BF16_ROWS * n_steps)
    n_chunks = w.shape[0] // rows
    assert n_chunks * rows == w.shape[0] and n_chunks <= n_steps
    return pl.BlockSpec((rows, w.shape[1]), lambda i: (jnp.minimum(i, n_chunks - 1), 0))


def _two_stream(n_first, tm_rows):
    first = lambda w: pl.BlockSpec((tm_rows, w), lambda i: (jnp.minimum(i, n_first - 1), 0))
    second = lambda w: pl.BlockSpec((tm_rows, w), lambda i: (jnp.maximum(i - n_first, 0), 0))
    return first, second


def _mod_spec(n_ctx_tiles, tm, seq_lat):
    def index(i):
        return (jnp.where(i < n_ctx_tiles, 0, 1 + (jnp.maximum(i - n_ctx_tiles, 0) * tm) // seq_lat), 0, 0)
    return pl.BlockSpec((1, N_MOD, D_MODEL), index)


N_LATE_WEIGHTS = 5


def _ffn1_kernel(xp_ref, xs_ref, mod_ref, ng_ref, wgu_ref, wo_ref, win_ref, *rest, n_ctx_tiles):
    late_f32 = rest[:N_LATE_WEIGHTS]
    x1_ref, qkv_ref, xl_ref, gl_ref, gates_ref, newk_ref, newv_ref = rest[N_LATE_WEIGHTS:N_LATE_WEIGHTS + 7]
    late_bf16 = rest[N_LATE_WEIGHTS + 7:2 * N_LATE_WEIGHTS + 7]
    a_ref, h_ref = rest[2 * N_LATE_WEIGHTS + 7:]

    for src, dst in zip(late_f32, late_bf16):
        dst[...] = src[...].astype(BF16)

    is_ctx = pl.program_id(0) < n_ctx_tiles
    x = jnp.where(is_ctx, xp_ref[...], xs_ref[...])
    m = mod_ref[0]
    h_ref[...] = _rms_mod(x, ng_ref[0:1], m[0:1], m[1:2]).astype(BF16)
    x1 = x + (0.5 * m[2:3]) * _swiglu(h_ref, a_ref, wgu_ref, wo_ref)
    x1_ref[...] = x1
    h2 = _rms_mod(x1, ng_ref[1:2], m[3:4], m[4:5]).astype(BF16)
    c0, c1, c2, c3 = 3 * D_ATTN, 3 * D_ATTN + D_LRU, 3 * D_ATTN + 2 * D_LRU, 3 * D_ATTN + 2 * D_LRU + 2 * D_MODEL
    qkv = _dot(h2, win_ref[:, 0:c0])
    qkv_ref[:, :D_ATTN] = (qkv[:, :D_ATTN] * ATTN_SCALE).astype(BF16)
    qkv_ref[:, D_ATTN:] = qkv[:, D_ATTN:].astype(BF16)

    @pl.when(is_ctx)
    def _():
        for h in range(N_HEADS):
            newk_ref[0, 0, h] = qkv[:, D_ATTN + h * HEAD_DIM:D_ATTN + (h + 1) * HEAD_DIM]
            newv_ref[0, 0, h] = qkv[:, 2 * D_ATTN + h * HEAD_DIM:2 * D_ATTN + (h + 1) * HEAD_DIM]

    xl_ref[...] = _dot(h2, win_ref[:, c0:c1])
    gl_ref[...] = _dot(h2, win_ref[:, c1:c2])
    gates_ref[...] = _dot(h2, win_ref[:, c2:c3])


def _ffn1(xp, xs, mods3, norm_g, wgu, wo, win, late_weights, nb_ctx, seq_ctx, seq_lat):
    tm = seq_ctx
    n_ctx_tiles = xp.shape[0] // tm
    t = xp.shape[0] + xs.shape[0]
    n_steps = t // tm
    assert len(late_weights) == N_LATE_WEIGHTS
    row = lambda w: pl.BlockSpec((tm, w), lambda i: (i, 0))
    first, second = _two_stream(n_ctx_tiles, tm)
    cache_spec = pl.BlockSpec((1, 1, N_HEADS, seq_ctx, HEAD_DIM),
                              lambda i: (jnp.minimum(i, n_ctx_tiles - 1), 0, 0, 0, 0))
    cache_shape = jax.ShapeDtypeStruct((nb_ctx, 1, N_HEADS, seq_ctx, HEAD_DIM), F32)
    cast_specs = [_cast_spec(w, n_steps) for w in late_weights]
    return pl.pallas_call(
        functools.partial(_ffn1_kernel, n_ctx_tiles=n_ctx_tiles),
        grid=(n_steps,),
        in_specs=[first(D_MODEL), second(D_MODEL), _mod_spec(n_ctx_tiles, tm, seq_lat),
                  _const_spec(norm_g.shape), _const_spec(wgu.shape), _const_spec(wo.shape),
                  _const_spec(win.shape)] + cast_specs,
        out_specs=[row(D_MODEL), row(3 * D_ATTN), row(D_LRU), row(D_LRU), row(2 * D_MODEL),
                   cache_spec, cache_spec] + cast_specs,
        out_shape=[jax.ShapeDtypeStruct((t, D_MODEL), F32),
                   jax.ShapeDtypeStruct((t, 3 * D_ATTN), BF16),
                   jax.ShapeDtypeStruct((t, D_LRU), F32),
                   jax.ShapeDtypeStruct((t, D_LRU), F32),
                   jax.ShapeDtypeStruct((t, 2 * D_MODEL), F32),
                   cache_shape, cache_shape] + [jax.ShapeDtypeStruct(w.shape, BF16) for w in late_weights],
        scratch_shapes=[pltpu.VMEM((tm, D_FF), BF16), pltpu.VMEM((tm, D_MODEL), BF16)],
        compiler_params=_params(),
        name="ffn1_inproj",
    )(xp, xs, mods3, norm_g, wgu, wo, win, *late_weights)


def _ffn2_kernel(x1_ref, attnp_ref, attns_ref, lrup_ref, lrus_ref, gates_ref, mod_ref, ng_ref, fg_ref,
                 wba_ref, wbl_ref, wout_ref, wgu_ref, wo_ref, yp_ref, ys_ref, a_ref, h_ref, *, n_ctx_tiles):
    is_ctx = pl.program_id(0) < n_ctx_tiles
    attn = jnp.where(is_ctx, attnp_ref[...], attns_ref[...])
    lru = jnp.where(is_ctx, lrup_ref[...], lrus_ref[...])
    m = mod_ref[0]
    gt = gates_ref[...]
    mm = (_sigmoid(gt[:, :D_MODEL]) * _dot(attn, wba_ref[...])
          + _sigmoid(gt[:, D_MODEL:]) * _dot(lru, wbl_ref[...]))
    x2 = x1_ref[...] + m[5:6] * _dot(mm.astype(BF16), wout_ref[...])
    h_ref[...] = _rms_mod(x2, ng_ref[2:3], m[6:7], m[7:8]).astype(BF16)
    x3 = x2 + (0.5 * m[8:9]) * _swiglu(h_ref, a_ref, wgu_ref, wo_ref)
    y = x3 * lax.rsqrt(jnp.mean(x3 * x3, axis=-1, keepdims=True) + EPS)
    y = y * fg_ref[...]

    @pl.when(is_ctx)
    def _():
        yp_ref[...] = y

    @pl.when(jnp.logical_not(is_ctx))
    def _():
        ys_ref[...] = y


def _ffn2(x1, attnp, attns, lrup, lrus, gates, mods3, norm_g, final_g, wba, wbl, wout, wgu, wo, tm, seq_lat):
    t = x1.shape[0]
    t_ctx = attnp.shape[0]
    n_ctx_tiles = t_ctx // tm
    row = lambda w: pl.BlockSpec((tm, w), lambda i: (i, 0))
    first, second = _two_stream(n_ctx_tiles, tm)
    return pl.pallas_call(
        functools.partial(_ffn2_kernel, n_ctx_tiles=n_ctx_tiles),
        grid=(t // tm,),
        in_specs=[row(D_MODEL), first(D_ATTN), second(D_ATTN), first(D_LRU), second(D_LRU), row(2 * D_MODEL),
                  _mod_spec(n_ctx_tiles, tm, seq_lat),
                  _const_spec(norm_g.shape), _const_spec(final_g.shape),
                  _const_spec(wba.shape), _const_spec(wbl.shape), _const_spec(wout.shape),
                  _const_spec(wgu.shape), _const_spec(wo.shape)],
        out_specs=[first(D_MODEL), second(D_MODEL)],
        out_shape=[jax.ShapeDtypeStruct((t_ctx, D_MODEL), F32), jax.ShapeDtypeStruct((t - t_ctx, D_MODEL), F32)],
        scratch_shapes=[pltpu.VMEM((tm, D_FF), BF16), pltpu.VMEM((tm, D_MODEL), BF16)],
        compiler_params=_params(),
        name="merge_ffn2",
    )(x1, attnp, attns, lrup, lrus, gates, mods3, norm_g, final_g, wba, wbl, wout, wgu, wo)


def _softmax_pv(scores, values, value_dots=None):
    value_dots = value_dots or [_dot] * len(values)
    m = scores[0].max(axis=-1, keepdims=True)
    for s in scores[1:]:
        m = jnp.maximum(m, s.max(axis=-1, keepdims=True))
    l = None
    o = None
    for s, v, dot in zip(scores, values, value_dots):
        p = jnp.exp(s - m)
        ls = p.sum(axis=-1, keepdims=True)
        os_ = dot(p.astype(BF16), v)
        l = ls if l is None else l + ls
        o = os_ if o is None else o + os_
    return o / l


def _ctx_attn_kernel(qkv_ref, o_ref):
    seq = qkv_ref.shape[0]
    lane = lax.broadcasted_iota(jnp.int32, (seq, LANES), 1)
    first = lane < HEAD_DIM
    for hp in range(D_ATTN // LANES):
        q = qkv_ref[:, hp * LANES:(hp + 1) * LANES]
        k = qkv_ref[:, D_ATTN + hp * LANES:D_ATTN + (hp + 1) * LANES]
        v = qkv_ref[:, 2 * D_ATTN + hp * LANES:2 * D_ATTN + (hp + 1) * LANES]
        qm = jnp.concatenate([jnp.where(first, q, jnp.zeros_like(q)), jnp.where(first, jnp.zeros_like(q), q)], axis=0)
        o = _softmax_pv([_dot_nt(qm, k)], [v])
        o_ref[:, hp * LANES:(hp + 1) * LANES] = jnp.where(first, o[:seq], o[seq:]).astype(BF16)


def _ctx_attention(qkv, nb, seq):
    return pl.pallas_call(
        _ctx_attn_kernel,
        grid=(nb,),
        in_specs=[pl.BlockSpec((seq, 3 * D_ATTN), lambda b: (b, 0))],
        out_specs=pl.BlockSpec((seq, D_ATTN), lambda b: (b, 0)),
        out_shape=jax.ShapeDtypeStruct((nb * seq, D_ATTN), BF16),
        compiler_params=_params(),
        name="ctx_attention",
    )(qkv)


def _na_groups(rows):
    kh = min(WIN_H, rows)
    r = np.arange(rows)
    rs = np.clip(r - kh // 2, 0, rows - kh)
    groups = []
    for g in range(rows // ROWS_PER_GROUP):
        qr = r[g * ROWS_PER_GROUP:(g + 1) * ROWS_PER_GROUP]
        k0, k1 = int(rs[qr].min()), int(rs[qr].max()) + kh
        if (k1 - k0) % 2:
            if k1 < rows:
                k1 += 1
            else:
                k0 -= 1
        groups.append((k0, k1))
    return groups, rs, kh


def _na_rel_rows(rpb):
    edge = GRID_W - WIN_W
    v = jnp.concatenate([rpb[..., WIN_W - 1:],
                         jnp.repeat(rpb[..., -1:], edge, axis=-1),
                         jnp.zeros(rpb.shape[:-1] + (1,), F32),
                         jnp.repeat(rpb[..., :1], edge, axis=-1),
                         rpb[..., :WIN_W - 1]], axis=-1).astype(F32)
    return jnp.pad(v, ((0, 0), (0, 1), (0, 0)))


def _na_build_bias(rel_ref, bias_ref, groups, rs, kh):
    shape = (GRID_W, LANES)
    qc = lax.broadcasted_iota(jnp.int32, shape, 0)
    lane = lax.broadcasted_iota(jnp.int32, shape, 1)
    left = lane < GRID_W
    kc = jnp.where(left, lane, lane - GRID_W)
    cs = jnp.clip(qc - WIN_W // 2, 0, GRID_W - WIN_W)
    in_win = (kc >= cs) & (kc < cs + WIN_W)
    neg = jnp.full(shape, NEG_INF, F32)
    for hh in range(2):
        def tile(r, j, half, hh=hh):
            row = jnp.broadcast_to(rel_ref[hh, j - r + WIN_H - 1:j - r + WIN_H, :], shape)
            return pltpu.roll(row, half * GRID_W, 1, stride=1, stride_axis=0)

        boff = 0
        for g, (k0, k1) in enumerate(groups):
            for i in range(ROWS_PER_GROUP):
                r = g * ROWS_PER_GROUP + i
                for jp in range((k1 - k0) // 2):
                    j = k0 + 2 * jp
                    a_ok = rs[r] <= j < rs[r] + kh
                    b_ok = rs[r] <= j + 1 < rs[r] + kh
                    if a_ok and b_ok:
                        t = jnp.where(in_win, jnp.where(left, tile(r, j, 0), tile(r, j + 1, 1)), neg)
                    elif a_ok:
                        t = jnp.where(in_win & left, tile(r, j, 0), neg)
                    elif b_ok:
                        t = jnp.where(in_win & jnp.logical_not(left), tile(r, j + 1, 1), neg)
                    else:
                        t = neg
                    bias_ref[hh, i * GRID_W:(i + 1) * GRID_W, boff + jp * LANES:boff + (jp + 1) * LANES] = t
            boff += (k1 - k0) * GRID_W


def _na_attn_kernel(q_ref, k_ref, v_ref, kc_ref, vc_ref, rel_ref, o_ref, bias_ref, *, groups, rs, kh):
    @pl.when(pl.program_id(1) == 0)
    def _():
        _na_build_bias(rel_ref, bias_ref, groups, rs, kh)

    gq = ROWS_PER_GROUP * GRID_W
    lane = lax.broadcasted_iota(jnp.int32, (gq, LANES), 1)
    first = lane < HEAD_DIM

    past = kc_ref.shape[-1]
    kc = kc_ref[0, 0].reshape(LANES, past).astype(BF16)
    vc = vc_ref[0, 0].reshape(LANES, past).astype(BF16)
    for g, (k0, k1) in enumerate(groups):
        q = q_ref[g * gq:(g + 1) * gq, :]
        k = k_ref[k0 * GRID_W:k1 * GRID_W, :]
        v = v_ref[k0 * GRID_W:k1 * GRID_W, :]
        boff = sum((b - a) * GRID_W for a, b in groups[:g])
        qm = jnp.concatenate([jnp.where(first, q, jnp.zeros_like(q)), jnp.where(first, jnp.zeros_like(q), q)], axis=0)
        bias = jnp.concatenate([bias_ref[hh, :, boff:boff + (k1 - k0) * GRID_W] for hh in range(2)], axis=0)
        o = _softmax_pv([_dot_nt(qm, k) + bias, _dot(qm, kc)], [v, vc], [_dot, _dot_nt])
        o_ref[g * gq:(g + 1) * gq, :] = jnp.where(first, o[:gq], o[gq:]).astype(BF16)


def _na_attention(qkv, kc, vc, rel, first_seq, nb, seq):
    t = nb * seq
    past = kc.shape[4]
    npair = D_ATTN // LANES
    groups, rs, kh = _na_groups(seq // GRID_W)
    width = sum(k1 - k0 for k0, k1 in groups) * GRID_W
    tok = lambda off: pl.BlockSpec((seq, LANES), lambda hp, b: (first_seq + b, off + hp))
    ctx = pl.BlockSpec((1, 1, 2, HEAD_DIM, past), lambda hp, b: (b, 0, hp, 0, 0))
    return pl.pallas_call(
        functools.partial(_na_attn_kernel, groups=groups, rs=[int(x) for x in rs], kh=kh),
        grid=(npair, nb),
        in_specs=[tok(0), tok(npair), tok(2 * npair), ctx, ctx,
                  pl.BlockSpec((2,) + rel.shape[1:], lambda hp, b: (hp, 0, 0))],
        out_specs=pl.BlockSpec((seq, LANES), lambda hp, b: (b, hp)),
        out_shape=jax.ShapeDtypeStruct((t, D_ATTN), BF16),
        scratch_shapes=[pltpu.VMEM((2, ROWS_PER_GROUP * GRID_W, width), F32)],
        compiler_params=_params(2),
        name="na_attention",
    )(qkv, qkv, qkv, kc, vc, rel)


def _gelu_tanh(x):
    c = float(np.sqrt(2.0 / np.pi))
    t = jnp.tanh(x * (c + (c * 0.044715) * (x * x)))
    return x * (0.5 * t + 0.5)


def _build_blockdiag(wa_ref, wi_ref, wbd_ref):
    per = LRU_TILE // LRU_BLOCK
    wbd_ref[...] = jnp.zeros(wbd_ref.shape, BF16)
    zero = jnp.zeros((LRU_BLOCK, LRU_BLOCK), F32)
    for d in range(2):
        for t in range(N_LRU_TILES):
            for p in range(per):
                rows = slice(p * LRU_BLOCK, (p + 1) * LRU_BLOCK)
                for half, w_ref in enumerate((wa_ref, wi_ref)):
                    blk = 0.5 * w_ref[d, t * per + p]
                    piece = jnp.concatenate([blk, zero] if p % 2 == 0 else [zero, blk], axis=1)
                    c0 = half * LRU_TILE + (p // 2) * LANES
                    wbd_ref[d, t, rows, c0:c0 + LANES] = piece.astype(BF16)


def _lru_kernel(xl_ref, gl_ref, h0_ref, cw_ref, cb_ref, wa_ref, wi_ref, ba_ref, bi_ref, coef_ref,
                y_ref, hfin_ref, xpad, xc, a_buf, hf_buf, hb_buf, wbd_ref, *, seq):
    pitch = seq // N_SEG + SEG_PAD
    lp = N_SEG * pitch
    halo = SUBLANES
    chunk = LANES
    n_chunks = seq // chunk
    left = CONV_W // 2

    @pl.when(pl.program_id(0) == 0)
    def _():
        _build_blockdiag(wa_ref, wi_ref, wbd_ref)

    for k in range(N_SLABS):
        cols = slice(k * LANES, (k + 1) * LANES)
        xpad[k, 0:halo, :] = jnp.zeros((halo, LANES), F32)
        xpad[k, halo + seq:2 * halo + seq, :] = jnp.zeros((halo, LANES), F32)
        xpad[k, halo:halo + seq, :] = xl_ref[:, cols]

    def conv_body(i, carry):
        r0 = pl.multiple_of(i * chunk, chunk)
        for k in range(N_SLABS):
            cols = slice(k * LANES, (k + 1) * LANES)
            out = cb_ref[:, cols]
            for j in range(CONV_W):
                out = out + xpad[k, pl.ds(r0 + halo + j - left, chunk), :] * cw_ref[j:j + 1, cols]
            xc[k, pl.ds(r0, chunk), :] = out
        return carry

    lax.fori_loop(0, n_chunks, conv_body, 0)

    a_buf[:, seq:lp, :] = jnp.ones((N_SLABS, lp - seq, LANES), F32)
    hf_buf[:, seq:lp, :] = jnp.zeros((N_SLABS, lp - seq, LANES), F32)
    hb_buf[:, seq:lp, :] = jnp.zeros((N_SLABS, lp - seq, LANES), F32)

    row = lax.broadcasted_iota(jnp.int32, (N_SEG, LANES), 0)
    for d, h_buf in enumerate((hf_buf, hb_buf)):
        def gates_body(i, carry, d=d, h_buf=h_buf):
            rows = pl.ds(pl.multiple_of(i * chunk, chunk), chunk)
            per = LRU_TILE // LANES
            for c in range(N_LRU_TILES):
                xs = [xc[c * per + half, rows, :] for half in range(per)]
                pre = _dot(jnp.concatenate(xs, axis=1).astype(BF16), wbd_ref[d, c])
                for half in range(per):
                    k = c * per + half
                    cols = slice(k * LANES, (k + 1) * LANES)
                    tr = jnp.tanh(pre[:, half * LANES:(half + 1) * LANES] + ba_ref[d:d + 1, cols])
                    ti = jnp.tanh(pre[:, LRU_TILE + half * LANES:LRU_TILE + (half + 1) * LANES]
                                  + bi_ref[d:d + 1, cols])
                    a = jnp.exp2(coef_ref[d:d + 1, cols] * tr + coef_ref[d:d + 1, cols])
                    y = (1.0 - a) * (1.0 + a)
                    mult = y * lax.rsqrt(jnp.maximum(y, F32_TINY))
                    a_buf[k, rows, :] = a
                    h_buf[k, rows, :] = mult * ((0.5 * ti + 0.5) * xs[half])
            return carry

        lax.fori_loop(0, n_chunks, gates_body, 0, unroll=min(4, n_chunks))

        base0 = jnp.int32(0 if d == 0 else pitch - SCAN_UNROLL)
        base_step = SCAN_UNROLL if d == 0 else -SCAN_UNROLL

        def seg_rows(base, u, d=d):
            return pl.ds(base + (u if d == 0 else SCAN_UNROLL - 1 - u), N_SEG, stride=pitch)

        def totals_body(_, carry, h_buf=h_buf):
            base, hs, ps = carry[0], list(carry[1]), list(carry[2])
            for u in range(SCAN_UNROLL):
                idx = seg_rows(base, u)
                for k in range(N_SLABS):
                    a = a_buf[k, idx, :]
                    hs[k] = a * hs[k] + h_buf[k, idx, :]
                    ps[k] = ps[k] * a
            return base + base_step, tuple(hs), tuple(ps)

        zeros = tuple(jnp.zeros((N_SEG, LANES), F32) for _ in range(N_SLABS))
        ones = tuple(jnp.ones((N_SEG, LANES), F32) for _ in range(N_SLABS))
        _, h_end, p_end = lax.fori_loop(0, pitch // SCAN_UNROLL, totals_body, (base0, zeros, ones))

        order = range(N_SEG) if d == 0 else range(N_SEG - 1, -1, -1)
        starts = []
        for k in range(N_SLABS):
            c = h0_ref[0, d:d + 1, k * LANES:(k + 1) * LANES]
            cm = jnp.zeros((N_SEG, LANES), F32)
            for s in order:
                cm = jnp.where(row == s, c, cm)
                c = p_end[k][s:s + 1, :] * c + h_end[k][s:s + 1, :]
            starts.append(cm)
            hfin_ref[0, d:d + 1, k * LANES:(k + 1) * LANES] = c

        def scan_body(_, carry, h_buf=h_buf):
            base, hs = carry[0], list(carry[1])
            for u in range(SCAN_UNROLL):
                idx = seg_rows(base, u)
                for k in range(N_SLABS):
                    hs[k] = a_buf[k, idx, :] * hs[k] + h_buf[k, idx, :]
                    h_buf[k, idx, :] = hs[k]
            return base + base_step, tuple(hs)

        lax.fori_loop(0, pitch // SCAN_UNROLL, scan_body, (base0, tuple(starts)))

    def out_body(i, carry):
        r0 = pl.multiple_of(i * chunk, chunk)
        for k in range(N_SLABS):
            cols = slice(k * LANES, (k + 1) * LANES)
            h = hf_buf[k, pl.ds(r0, chunk), :] + hb_buf[k, pl.ds(r0, chunk), :]
            y_ref[pl.ds(r0, chunk), cols] = (h * _gelu_tanh(gl_ref[pl.ds(r0, chunk), cols])).astype(BF16)
        return carry

    lax.fori_loop(0, n_chunks, out_body, 0)


def _lru(xl, gl, h0, conv_w, conv_b, wa, wi, ba, bi, coef, first_seq, nb, seq):
    t = nb * seq
    lp = N_SEG * (seq // N_SEG + SEG_PAD)
    tok_in = pl.BlockSpec((seq, D_LRU), lambda b: (first_seq + b, 0))
    tok = pl.BlockSpec((seq, D_LRU), lambda b: (b, 0))
    state = pl.BlockSpec((1, 2, D_LRU), lambda b: (b, 0, 0))
    scan_buf = pltpu.VMEM((N_SLABS, lp, LANES), F32)
    return pl.pallas_call(
        functools.partial(_lru_kernel, seq=seq),
        grid=(nb,),
        in_specs=[tok_in, tok_in, state, _const_spec(conv_w.shape), _const_spec(conv_b.shape),
                  _const_spec(wa.shape), _const_spec(wi.shape), _const_spec(ba.shape), _const_spec(bi.shape),
                  _const_spec(coef.shape)],
        out_specs=[tok, state],
        out_shape=[jax.ShapeDtypeStruct((t, D_LRU), BF16), jax.ShapeDtypeStruct((nb, 2, D_LRU), F32)],
        scratch_shapes=[pltpu.VMEM((N_SLABS, seq + 2 * SUBLANES, LANES), F32),
                        pltpu.VMEM((N_SLABS, seq, LANES), F32), scan_buf, scan_buf, scan_buf,
                        pltpu.VMEM((2, N_LRU_TILES, LRU_TILE, 2 * LRU_TILE), BF16)],
        compiler_params=_params(),
        name="rglru",
    )(xl, gl, h0, conv_w, conv_b, wa, wi, ba, bi, coef)


def kernel(x_prompt, x_sample, cache_k, cache_v, state_lru, c, c_ctx, w_mod, b_mod, norm_g, ffn1_w_in, ffn1_w_out,
           w_in, rpb, conv_w, conv_b, lru_wa, lru_ba, lru_wi, lru_bi, lru_lambda, w_br_attn, w_br_lru, w_out,
           ffn2_w_in, ffn2_w_out, final_g):
    assert w_mod.shape[0] == 1, "single trunk layer"
    nb_ctx, seq_ctx, _ = x_prompt.shape
    nb_lat, seq_lat, _ = x_sample.shape

    n_cond = 1 + nb_lat
    cond = jnp.concatenate([c_ctx[None, :], c, jnp.zeros((2 * SUBLANES - n_cond, D_MODEL), F32)], axis=0)
    mods3 = _mods(cond, w_mod[0], b_mod[0]).reshape(2 * SUBLANES, N_MOD, D_MODEL)

    wgu1, wo1, win = ffn1_w_in[0].astype(BF16), ffn1_w_out[0].astype(BF16), w_in[0].astype(BF16)
    late_weights = (w_br_attn[0], w_br_lru[0], w_out[0], ffn2_w_in[0], ffn2_w_out[0])
    coef = (0.5 * float(np.log2(np.e))) * (-LRU_C * jax.nn.softplus(-lru_lambda[0]))
    ba = 0.5 * lru_ba[0]
    bi = 0.5 * lru_bi[0]
    ng = norm_g[0]
    fg = final_g.reshape(1, D_MODEL)
    cb = conv_b[0].reshape(1, D_LRU)

    t_ctx = nb_ctx * seq_ctx
    assert t_ctx % seq_lat == 0, "latent sequences must start on a latent-sequence-sized row block"
    xp = x_prompt.reshape(t_ctx, D_MODEL)
    xs = x_sample.reshape(nb_lat * seq_lat, D_MODEL)

    x1, qkv, xl, gl, gates, new_k, new_v, wba, wbl, wout, wgu2, wo2 = _ffn1(
        xp, xs, mods3, ng, wgu1, wo1, win, late_weights, nb_ctx, seq_ctx, seq_lat)

    attnp = _ctx_attention(qkv, nb_ctx, seq_ctx)
    lrup, h_fin = _lru(xl, gl, jnp.zeros((nb_ctx, 2, D_LRU), F32), conv_w[0], cb, lru_wa[0], lru_wi[0],
                       ba, bi, coef, 0, nb_ctx, seq_ctx)

    rel = _na_rel_rows(rpb[0])
    first_lat = t_ctx // seq_lat
    attns = _na_attention(qkv, cache_k.swapaxes(3, 4), cache_v.swapaxes(3, 4), rel, first_lat, nb_lat, seq_lat)
    lrus, _ = _lru(xl, gl, state_lru[:, 0], conv_w[0], cb, lru_wa[0], lru_wi[0], ba, bi, coef,
                   first_lat, nb_lat, seq_lat)

    yp, ys = _ffn2(x1, attnp, attns, lrup, lrus, gates, mods3, ng, fg, wba, wbl, wout, wgu2, wo2, 512, seq_lat)
    return (yp.reshape(nb_ctx, seq_ctx, D_MODEL), ys.reshape(nb_lat, seq_lat, D_MODEL),
            new_k, new_v, h_fin[:, None])
```

```python
import functools

import numpy as np
import jax
import jax.numpy as jnp
from jax import lax
from jax.experimental import pallas as pl
from jax.experimental.pallas import tpu as pltpu

F32 = jnp.float32
BF16 = jnp.bfloat16

D_MODEL = 1024
N_HEADS = 8
HEAD_DIM = 64
D_ATTN = N_HEADS * HEAD_DIM
GRID_W = 64
WIN_H = 8
WIN_W = 16
D_LRU = 1024
LRU_BLOCKS = 16
LRU_BLOCK = D_LRU // LRU_BLOCKS
CONV_W = 4
LRU_C = 8.0
D_FF = 2816
N_MOD = 9
EPS = 1e-6
NEG_INF = -1e30
F32_TINY = float(np.finfo(np.float32).tiny)
ATTN_SCALE = HEAD_DIM ** -0.5

LANES = 128
SUBLANES = 8
MXU_DIM = 256
VMEM_LIMIT_BYTES = 56 * 1024 * 1024

FF_CHUNK = MXU_DIM
N_FF_CHUNKS = D_FF // FF_CHUNK
LRU_TILE = MXU_DIM
N_LRU_TILES = D_LRU // LRU_TILE
N_SLABS = D_LRU // LANES
N_SEG = SUBLANES
SEG_PAD = 4
SCAN_UNROLL = 4
ROWS_PER_GROUP = 4


def _sigmoid(x):
    return 0.5 * jnp.tanh(0.5 * x) + 0.5


def _rms_mod(x, g, shift, scale):
    y = x * lax.rsqrt(jnp.mean(x * x, axis=-1, keepdims=True) + EPS)
    y = y * g
    return y * (1 + scale) + shift


def _dot(a, b):
    return jnp.dot(a, b, preferred_element_type=F32)


def _dot_nt(a, b):
    return lax.dot_general(a, b, (((1,), (1,)), ((), ())), preferred_element_type=F32)


def _params(n_axes=1):
    return pltpu.CompilerParams(dimension_semantics=("arbitrary",) * n_axes,
                                vmem_limit_bytes=VMEM_LIMIT_BYTES)


def _const_spec(shape):
    nd = len(shape)
    return pl.BlockSpec(shape, lambda *_: (0,) * nd, pipeline_mode=pl.Buffered(1))


def _mods_kernel(cond_ref, w_ref, b_ref, o_ref):
    c = cond_ref[...]
    s = (c * _sigmoid(c)).astype(BF16)
    o_ref[...] = _dot(s, w_ref[...].astype(BF16)) + b_ref[...]


def _mods(cond, w_mod, b_mod):
    n = w_mod.shape[1]
    bn = D_MODEL
    return pl.pallas_call(
        _mods_kernel,
        grid=(n // bn,),
        in_specs=[pl.BlockSpec(cond.shape, lambda j: (0, 0)),
                  pl.BlockSpec((D_MODEL, bn), lambda j: (0, j)),
                  pl.BlockSpec((1, bn), lambda j: (0, j))],
        out_specs=pl.BlockSpec((cond.shape[0], bn), lambda j: (0, j)),
        out_shape=jax.ShapeDtypeStruct((cond.shape[0], n), F32),
        compiler_params=_params(),
        name="mods",
    )(cond, w_mod, b_mod.reshape(1, n))


def _swiglu_hidden(h_ref, a_ref, wgu_ref):
    h = h_ref[...]
    for j in range(N_FF_CHUNKS):
        cols = slice(j * FF_CHUNK, (j + 1) * FF_CHUNK)
        g = _dot(h, wgu_ref[:, cols])
        u = _dot(h, wgu_ref[:, D_FF + j * FF_CHUNK:D_FF + (j + 1) * FF_CHUNK])
        a_ref[:, cols] = ((g * _sigmoid(g)) * u).astype(BF16)


def _swiglu(h_ref, a_ref, wgu_ref, wo_ref):
    _swiglu_hidden(h_ref, a_ref, wgu_ref)
    return _dot(a_ref[...], wo_ref[...])


BF16_ROWS = 2 * SUBLANES


def _cast_spec(w, n_steps):
    rows = BF16_ROWS * pl.cdiv(w.shape[0], BF16_ROWS * n_steps)
    n_chunks = w.shape[0] // rows
    assert n_chunks * rows == w.shape[0] and n_chunks <= n_steps
    return pl.BlockSpec((rows, w.shape[1]), lambda i: (jnp.minimum(i, n_chunks - 1), 0))


def _two_stream(n_first, tm_rows):
    first = lambda w: pl.BlockSpec((tm_rows, w), lambda i: (jnp.minimum(i, n_first - 1), 0))
    second = lambda w: pl.BlockSpec((tm_rows, w), lambda i: (jnp.maximum(i - n_first, 0), 0))
    return first, second


def _mod_spec(n_ctx_tiles, tm, seq_lat):
    def index(i):
        return (jnp.where(i < n_ctx_tiles, 0, 1 + (jnp.maximum(i - n_ctx_tiles, 0) * tm) // seq_lat), 0, 0)
    return pl.BlockSpec((1, N_MOD, D_MODEL), index)


N_LATE_WEIGHTS = 5

def _ffn1_kernel(xp_ref, xs_ref, mod_ref, ng_ref, wgu_ref, wo_ref, win_ref, *rest, n_ctx_tiles):
    late_f32 = rest[:N_LATE_WEIGHTS]
    x1_ref, qkv_ref, xl_ref, gl_ref, gates_ref, newk_ref, newv_ref = rest[N_LATE_WEIGHTS:N_LATE_WEIGHTS + 7]
    late_bf16 = rest[N_LATE_WEIGHTS + 7:2 * N_LATE_WEIGHTS + 7]
    a_ref, h_ref = rest[2 * N_LATE_WEIGHTS + 7:]

    for src, dst in zip(late_f32, late_bf16):
        dst[...] = src[...].astype(BF16)

    is_ctx = pl.program_id(0) < n_ctx_tiles
    x = jnp.where(is_ctx, xp_ref[...], xs_ref[...])
    m = mod_ref[0]
    h_ref[...] = _rms_mod(x, ng_ref[0:1], m[0:1], m[1:2]).astype(BF16)
    x1 = x + (0.5 * m[2:3]) * _swiglu(h_ref, a_ref, wgu_ref, wo_ref)
    x1_ref[...] = x1
    h2 = _rms_mod(x1, ng_ref[1:2], m[3:4], m[4:5]).astype(BF16)
    c0, c1, c2, c3 = 3 * D_ATTN, 3 * D_ATTN + D_LRU, 3 * D_ATTN + 2 * D_LRU, 3 * D_ATTN + 2 * D_LRU + 2 * D_MODEL
    qkv = _dot(h2, win_ref[:, 0:c0])
    qkv_ref[:, :D_ATTN] = (qkv[:, :D_ATTN] * ATTN_SCALE).astype(BF16)
    qkv_ref[:, D_ATTN:] = qkv[:, D_ATTN:].astype(BF16)
    ctx_block = jnp.broadcast_to(is_ctx, newk_ref.shape)
    for ref, off in ((newk_ref, D_ATTN), (newv_ref, 2 * D_ATTN)):
        heads = [qkv[:, off + h * HEAD_DIM:off + (h + 1) * HEAD_DIM] for h in range(N_HEADS)]
        pltpu.store(ref, jnp.stack(heads, axis=0)[None, None], mask=ctx_block)
    xl_ref[...] = _dot(h2, win_ref[:, c0:c1])
    gl_ref[...] = _dot(h2, win_ref[:, c1:c2])
    gates_ref[...] = _dot(h2, win_ref[:, c2:c3])


def _ffn1(xp, xs, mods3, norm_g, wgu, wo, win, late_weights, nb_ctx, seq_ctx, seq_lat):
    tm = seq_ctx
    n_ctx_tiles = xp.shape[0] // tm
    t = xp.shape[0] + xs.shape[0]
    n_steps = t // tm
    assert len(late_weights) == N_LATE_WEIGHTS
    row = lambda w: pl.BlockSpec((tm, w), lambda i: (i, 0))
    first, second = _two_stream(n_ctx_tiles, tm)
    cache_spec = pl.BlockSpec((1, 1, N_HEADS, seq_ctx, HEAD_DIM),
                              lambda i: (jnp.minimum(i, n_ctx_tiles - 1), 0, 0, 0, 0))
    cache_shape = jax.ShapeDtypeStruct((nb_ctx, 1, N_HEADS, seq_ctx, HEAD_DIM), F32)
    cast_specs = [_cast_spec(w, n_steps) for w in late_weights]
    return pl.pallas_call(
        functools.partial(_ffn1_kernel, n_ctx_tiles=n_ctx_tiles),
        grid=(n_steps,),
        in_specs=[first(D_MODEL), second(D_MODEL), _mod_spec(n_ctx_tiles, tm, seq_lat),
                  _const_spec(norm_g.shape), _const_spec(wgu.shape), _const_spec(wo.shape),
                  _const_spec(win.shape)] + cast_specs,
        out_specs=[row(D_MODEL), row(3 * D_ATTN), row(D_LRU), row(D_LRU), row(2 * D_MODEL),
                   cache_spec, cache_spec] + cast_specs,
        out_shape=[jax.ShapeDtypeStruct((t, D_MODEL), F32),
                   jax.ShapeDtypeStruct((t, 3 * D_ATTN), BF16),
                   jax.ShapeDtypeStruct((t, D_LRU), F32),
                   jax.ShapeDtypeStruct((t, D_LRU), F32),
                   jax.ShapeDtypeStruct((t, 2 * D_MODEL), F32),
                   cache_shape, cache_shape] + [jax.ShapeDtypeStruct(w.shape, BF16) for w in late_weights],
        scratch_shapes=[pltpu.VMEM((tm, D_FF), BF16), pltpu.VMEM((tm, D_MODEL), BF16)],
        compiler_params=_params(),
        name="ffn1_inproj",
    )(xp, xs, mods3, norm_g, wgu, wo, win, *late_weights)


def _ffn2_kernel(x1_ref, attnp_ref, attns_ref, lrup_ref, lrus_ref, gates_ref, mod_ref, ng_ref, fg_ref,
                 wba_ref, wbl_ref, wout_ref, wgu_ref, wo_ref, yp_ref, ys_ref, a_ref, h_ref, *, n_ctx_tiles):
    is_ctx = pl.program_id(0) < n_ctx_tiles
    attn = jnp.where(is_ctx, attnp_ref[...], attns_ref[...])
    lru = jnp.where(is_ctx, lrup_ref[...], lrus_ref[...])
    m = mod_ref[0]
    gt = gates_ref[...]
    mm = (_sigmoid(gt[:, :D_MODEL]) * _dot(attn, wba_ref[...])
          + _sigmoid(gt[:, D_MODEL:]) * _dot(lru, wbl_ref[...]))
    x2 = x1_ref[...] + m[5:6] * _dot(mm.astype(BF16), wout_ref[...])
    h_ref[...] = _rms_mod(x2, ng_ref[2:3], m[6:7], m[7:8]).astype(BF16)
    x3 = x2 + (0.5 * m[8:9]) * _swiglu(h_ref, a_ref, wgu_ref, wo_ref)
    y = x3 * lax.rsqrt(jnp.mean(x3 * x3, axis=-1, keepdims=True) + EPS)
    y = y * fg_ref[...]
    ctx_tile = jnp.broadcast_to(is_ctx, y.shape)
    pltpu.store(yp_ref, y, mask=ctx_tile)
    pltpu.store(ys_ref, y, mask=jnp.logical_not(ctx_tile))


def _ffn2(x1, attnp, attns, lrup, lrus, gates, mods3, norm_g, final_g, wba, wbl, wout, wgu, wo, tm, seq_lat):
    t = x1.shape[0]
    t_ctx = attnp.shape[0]
    n_ctx_tiles = t_ctx // tm
    row = lambda w: pl.BlockSpec((tm, w), lambda i: (i, 0))
    first, second = _two_stream(n_ctx_tiles, tm)
    return pl.pallas_call(
        functools.partial(_ffn2_kernel, n_ctx_tiles=n_ctx_tiles),
        grid=(t // tm,),
        in_specs=[row(D_MODEL), first(D_ATTN), second(D_ATTN), first(D_LRU), second(D_LRU), row(2 * D_MODEL),
                  _mod_spec(n_ctx_tiles, tm, seq_lat),
                  _const_spec(norm_g.shape), _const_spec(final_g.shape),
                  _const_spec(wba.shape), _const_spec(wbl.shape), _const_spec(wout.shape),
                  _const_spec(wgu.shape), _const_spec(wo.shape)],
        out_specs=[first(D_MODEL), second(D_MODEL)],
        out_shape=[jax.ShapeDtypeStruct((t_ctx, D_MODEL), F32), jax.ShapeDtypeStruct((t - t_ctx, D_MODEL), F32)],
        scratch_shapes=[pltpu.VMEM((tm, D_FF), BF16), pltpu.VMEM((tm, D_MODEL), BF16)],
        compiler_params=_params(),
        name="merge_ffn2",
    )(x1, attnp, attns, lrup, lrus, gates, mods3, norm_g, final_g, wba, wbl, wout, wgu, wo)


def _softmax_pv(scores, values, value_dots=None):
    value_dots = value_dots or [_dot] * len(values)
    m = scores[0].max(axis=-1, keepdims=True)
    for s in scores[1:]:
        m = jnp.maximum(m, s.max(axis=-1, keepdims=True))
    l = None
    o = None
    for s, v, dot in zip(scores, values, value_dots):
        p = jnp.exp(s - m)
        ls = p.sum(axis=-1, keepdims=True)
        os_ = dot(p.astype(BF16), v)
        l = ls if l is None else l + ls
        o = os_ if o is None else o + os_
    return o / l


def _ctx_attn_kernel(qkv_ref, o_ref):
    seq = qkv_ref.shape[0]
    lane = lax.broadcasted_iota(jnp.int32, (seq, LANES), 1)
    first = lane < HEAD_DIM
    for hp in range(D_ATTN // LANES):
        q = qkv_ref[:, hp * LANES:(hp + 1) * LANES]
        k = qkv_ref[:, D_ATTN + hp * LANES:D_ATTN + (hp + 1) * LANES]
        v = qkv_ref[:, 2 * D_ATTN + hp * LANES:2 * D_ATTN + (hp + 1) * LANES]
        qm = jnp.concatenate([jnp.where(first, q, jnp.zeros_like(q)), jnp.where(first, jnp.zeros_like(q), q)], axis=0)
        o = _softmax_pv([_dot_nt(qm, k)], [v])
        o_ref[:, hp * LANES:(hp + 1) * LANES] = jnp.where(first, o[:seq], o[seq:]).astype(BF16)


def _ctx_attention(qkv, nb, seq):
    return pl.pallas_call(
        _ctx_attn_kernel,
        grid=(nb,),
        in_specs=[pl.BlockSpec((seq, 3 * D_ATTN), lambda b: (b, 0))],
        out_specs=pl.BlockSpec((seq, D_ATTN), lambda b: (b, 0)),
        out_shape=jax.ShapeDtypeStruct((nb * seq, D_ATTN), BF16),
        compiler_params=_params(),
        name="ctx_attention",
    )(qkv)


def _na_groups(rows):
    kh = min(WIN_H, rows)
    r = np.arange(rows)
    rs = np.clip(r - kh // 2, 0, rows - kh)
    groups = []
    for g in range(rows // ROWS_PER_GROUP):
        qr = r[g * ROWS_PER_GROUP:(g + 1) * ROWS_PER_GROUP]
        k0, k1 = int(rs[qr].min()), int(rs[qr].max()) + kh
        if (k1 - k0) % 2:
            if k1 < rows:
                k1 += 1
            else:
                k0 -= 1
        groups.append((k0, k1))
    return groups, rs, kh


def _na_rel_rows(rpb):
    edge = GRID_W - WIN_W
    v = jnp.concatenate([rpb[..., WIN_W - 1:],
                         jnp.repeat(rpb[..., -1:], edge, axis=-1),
                         jnp.zeros(rpb.shape[:-1] + (1,), F32),
                         jnp.repeat(rpb[..., :1], edge, axis=-1),
                         rpb[..., :WIN_W - 1]], axis=-1).astype(F32)
    return jnp.pad(v, ((0, 0), (0, 1), (0, 0)))


def _na_build_bias(rel_ref, bias_ref, groups, rs, kh):
    shape = (GRID_W, LANES)
    qc = lax.broadcasted_iota(jnp.int32, shape, 0)
    lane = lax.broadcasted_iota(jnp.int32, shape, 1)
    left = lane < GRID_W
    kc = jnp.where(left, lane, lane - GRID_W)
    cs = jnp.clip(qc - WIN_W // 2, 0, GRID_W - WIN_W)
    in_win = (kc >= cs) & (kc < cs + WIN_W)
    neg = jnp.full(shape, NEG_INF, F32)
    for hh in range(2):
        def tile(r, j, half, hh=hh):
            row = jnp.broadcast_to(rel_ref[hh, j - r + WIN_H - 1:j - r + WIN_H, :], shape)
            return pltpu.roll(row, half * GRID_W, 1, stride=1, stride_axis=0)

        boff = 0
        for g, (k0, k1) in enumerate(groups):
            for i in range(ROWS_PER_GROUP):
                r = g * ROWS_PER_GROUP + i
                for jp in range((k1 - k0) // 2):
                    j = k0 + 2 * jp
                    a_ok = rs[r] <= j < rs[r] + kh
                    b_ok = rs[r] <= j + 1 < rs[r] + kh
                    if a_ok and b_ok:
                        t = jnp.where(in_win, jnp.where(left, tile(r, j, 0), tile(r, j + 1, 1)), neg)
                    elif a_ok:
                        t = jnp.where(in_win & left, tile(r, j, 0), neg)
                    elif b_ok:
                        t = jnp.where(in_win & jnp.logical_not(left), tile(r, j + 1, 1), neg)
                    else:
                        t = neg
                    bias_ref[hh, i * GRID_W:(i + 1) * GRID_W, boff + jp * LANES:boff + (jp + 1) * LANES] = t
            boff += (k1 - k0) * GRID_W


def _na_attn_kernel(q_ref, k_ref, v_ref, kc_ref, vc_ref, rel_ref, o_ref, bias_ref, *, groups, rs, kh):
    @pl.when(pl.program_id(1) == 0)
    def _():
        _na_build_bias(rel_ref, bias_ref, groups, rs, kh)

    gq = ROWS_PER_GROUP * GRID_W
    lane = lax.broadcasted_iota(jnp.int32, (gq, LANES), 1)
    first = lane < HEAD_DIM

    past = kc_ref.shape[-1]
    kc = kc_ref[0, 0].reshape(LANES, past).astype(BF16)
    vc = vc_ref[0, 0].reshape(LANES, past).astype(BF16)
    for g, (k0, k1) in enumerate(groups):
        q = q_ref[g * gq:(g + 1) * gq, :]
        k = k_ref[k0 * GRID_W:k1 * GRID_W, :]
        v = v_ref[k0 * GRID_W:k1 * GRID_W, :]
        boff = sum((b - a) * GRID_W for a, b in groups[:g])
        qm = jnp.concatenate([jnp.where(first, q, jnp.zeros_like(q)), jnp.where(first, jnp.zeros_like(q), q)], axis=0)
        bias = jnp.concatenate([bias_ref[hh, :, boff:boff + (k1 - k0) * GRID_W] for hh in range(2)], axis=0)
        o = _softmax_pv([_dot_nt(qm, k) + bias, _dot(qm, kc)], [v, vc], [_dot, _dot_nt])
        o_ref[g * gq:(g + 1) * gq, :] = jnp.where(first, o[:gq], o[gq:]).astype(BF16)


def _na_attention(qkv, kc, vc, rel, first_seq, nb, seq):
    t = nb * seq
    past = kc.shape[4]
    npair = D_ATTN // LANES
    groups, rs, kh = _na_groups(seq // GRID_W)
    width = sum(k1 - k0 for k0, k1 in groups) * GRID_W
    tok = lambda off: pl.BlockSpec((seq, LANES), lambda hp, b: (first_seq + b, off + hp))
    ctx = pl.BlockSpec((1, 1, 2, HEAD_DIM, past), lambda hp, b: (b, 0, hp, 0, 0))
    return pl.pallas_call(
        functools.partial(_na_attn_kernel, groups=groups, rs=[int(x) for x in rs], kh=kh),
        grid=(npair, nb),
        in_specs=[tok(0), tok(npair), tok(2 * npair), ctx, ctx,
                  pl.BlockSpec((2,) + rel.shape[1:], lambda hp, b: (hp, 0, 0))],
        out_specs=pl.BlockSpec((seq, LANES), lambda hp, b: (b, hp)),
        out_shape=jax.ShapeDtypeStruct((t, D_ATTN), BF16),
        scratch_shapes=[pltpu.VMEM((2, ROWS_PER_GROUP * GRID_W, width), F32)],
        compiler_params=_params(2),
        name="na_attention",
    )(qkv, qkv, qkv, kc, vc, rel)


def _gelu_tanh(x):
    c = float(np.sqrt(2.0 / np.pi))
    t = jnp.tanh(x * (c + (c * 0.044715) * (x * x)))
    return x * (0.5 * t + 0.5)


def _build_blockdiag(wa_ref, wi_ref, wbd_ref):
    per = LRU_TILE // LRU_BLOCK
    wbd_ref[...] = jnp.zeros(wbd_ref.shape, BF16)
    zero = jnp.zeros((LRU_BLOCK, LRU_BLOCK), F32)
    for d in range(2):
        for t in range(N_LRU_TILES):
            for p in range(per):
                rows = slice(p * LRU_BLOCK, (p + 1) * LRU_BLOCK)
                for half, w_ref in enumerate((wa_ref, wi_ref)):
                    blk = 0.5 * w_ref[d, t * per + p]
                    piece = jnp.concatenate([blk, zero] if p % 2 == 0 else [zero, blk], axis=1)
                    c0 = half * LRU_TILE + (p // 2) * LANES
                    wbd_ref[d, t, rows, c0:c0 + LANES] = piece.astype(BF16)


def _lru_kernel(xl_ref, gl_ref, h0_ref, cw_ref, cb_ref, wa_ref, wi_ref, ba_ref, bi_ref, coef_ref,
                y_ref, hfin_ref, xpad, xc, a_buf, hf_buf, hb_buf, wbd_ref, *, seq):
    pitch = seq // N_SEG + SEG_PAD
    lp = N_SEG * pitch
    halo = SUBLANES
    chunk = LANES
    n_chunks = seq // chunk
    left = CONV_W // 2

    @pl.when(pl.program_id(0) == 0)
    def _():
        _build_blockdiag(wa_ref, wi_ref, wbd_ref)

    for k in range(N_SLABS):
        cols = slice(k * LANES, (k + 1) * LANES)
        xpad[k, 0:halo, :] = jnp.zeros((halo, LANES), F32)
        xpad[k, halo + seq:2 * halo + seq, :] = jnp.zeros((halo, LANES), F32)
        xpad[k, halo:halo + seq, :] = xl_ref[:, cols]

    def conv_body(i, carry):
        r0 = pl.multiple_of(i * chunk, chunk)
        for k in range(N_SLABS):
            cols = slice(k * LANES, (k + 1) * LANES)
            out = cb_ref[:, cols]
            for j in range(CONV_W):
                out = out + xpad[k, pl.ds(r0 + halo + j - left, chunk), :] * cw_ref[j:j + 1, cols]
            xc[k, pl.ds(r0, chunk), :] = out
        return carry

    lax.fori_loop(0, n_chunks, conv_body, 0)

    a_buf[:, seq:lp, :] = jnp.ones((N_SLABS, lp - seq, LANES), F32)
    hf_buf[:, seq:lp, :] = jnp.zeros((N_SLABS, lp - seq, LANES), F32)
    hb_buf[:, seq:lp, :] = jnp.zeros((N_SLABS, lp - seq, LANES), F32)

    row = lax.broadcasted_iota(jnp.int32, (N_SEG, LANES), 0)
    for d, h_buf in enumerate((hf_buf, hb_buf)):
        def gates_body(i, carry, d=d, h_buf=h_buf):
            rows = pl.ds(pl.multiple_of(i * chunk, chunk), chunk)
            per = LRU_TILE // LANES
            for c in range(N_LRU_TILES):
                xs = [xc[c * per + half, rows, :] for half in range(per)]
                pre = _dot(jnp.concatenate(xs, axis=1).astype(BF16), wbd_ref[d, c])
                for half in range(per):
                    k = c * per + half
                    cols = slice(k * LANES, (k + 1) * LANES)
                    tr = jnp.tanh(pre[:, half * LANES:(half + 1) * LANES] + ba_ref[d:d + 1, cols])
                    ti = jnp.tanh(pre[:, LRU_TILE + half * LANES:LRU_TILE + (half + 1) * LANES]
                                  + bi_ref[d:d + 1, cols])
                    a = jnp.exp2(coef_ref[d:d + 1, cols] * tr + coef_ref[d:d + 1, cols])
                    y = (1.0 - a) * (1.0 + a)
                    mult = y * lax.rsqrt(jnp.maximum(y, F32_TINY))
                    a_buf[k, rows, :] = a
                    h_buf[k, rows, :] = mult * ((0.5 * ti + 0.5) * xs[half])
            return carry

        lax.fori_loop(0, n_chunks, gates_body, 0, unroll=min(4, n_chunks))

        base0 = jnp.int32(0 if d == 0 else pitch - SCAN_UNROLL)
        base_step = SCAN_UNROLL if d == 0 else -SCAN_UNROLL

        def seg_rows(base, u, d=d):
            return pl.ds(base + (u if d == 0 else SCAN_UNROLL - 1 - u), N_SEG, stride=pitch)

        def totals_body(_, carry, h_buf=h_buf):
            base, hs, ps = carry[0], list(carry[1]), list(carry[2])
            for u in range(SCAN_UNROLL):
                idx = seg_rows(base, u)
                for k in range(N_SLABS):
                    a = a_buf[k, idx, :]
                    hs[k] = a * hs[k] + h_buf[k, idx, :]
                    ps[k] = ps[k] * a
            return base + base_step, tuple(hs), tuple(ps)

        zeros = tuple(jnp.zeros((N_SEG, LANES), F32) for _ in range(N_SLABS))
        ones = tuple(jnp.ones((N_SEG, LANES), F32) for _ in range(N_SLABS))
        _, h_end, p_end = lax.fori_loop(0, pitch // SCAN_UNROLL, totals_body, (base0, zeros, ones))

        order = range(N_SEG) if d == 0 else range(N_SEG - 1, -1, -1)
        starts = []
        for k in range(N_SLABS):
            c = h0_ref[0, d:d + 1, k * LANES:(k + 1) * LANES]
            cm = jnp.zeros((N_SEG, LANES), F32)
            for s in order:
                cm = jnp.where(row == s, c, cm)
                c = p_end[k][s:s + 1, :] * c + h_end[k][s:s + 1, :]
            starts.append(cm)
            hfin_ref[0, d:d + 1, k * LANES:(k + 1) * LANES] = c

        def scan_body(_, carry, h_buf=h_buf):
            base, hs = carry[0], list(carry[1])
            for u in range(SCAN_UNROLL):
                idx = seg_rows(base, u)
                for k in range(N_SLABS):
                    hs[k] = a_buf[k, idx, :] * hs[k] + h_buf[k, idx, :]
                    h_buf[k, idx, :] = hs[k]
            return base + base_step, tuple(hs)

        lax.fori_loop(0, pitch // SCAN_UNROLL, scan_body, (base0, tuple(starts)))

    def out_body(i, carry):
        r0 = pl.multiple_of(i * chunk, chunk)
        for k in range(N_SLABS):
            cols = slice(k * LANES, (k + 1) * LANES)
            h = hf_buf[k, pl.ds(r0, chunk), :] + hb_buf[k, pl.ds(r0, chunk), :]
            y_ref[pl.ds(r0, chunk), cols] = (h * _gelu_tanh(gl_ref[pl.ds(r0, chunk), cols])).astype(BF16)
        return carry

    lax.fori_loop(0, n_chunks, out_body, 0)


def _lru(xl, gl, h0, conv_w, conv_b, wa, wi, ba, bi, coef, first_seq, nb, seq):
    t = nb * seq
    lp = N_SEG * (seq // N_SEG + SEG_PAD)
    tok_in = pl.BlockSpec((seq, D_LRU), lambda b: (first_seq + b, 0))
    tok = pl.BlockSpec((seq, D_LRU), lambda b: (b, 0))
    state = pl.BlockSpec((1, 2, D_LRU), lambda b: (b, 0, 0))
    scan_buf = pltpu.VMEM((N_SLABS, lp, LANES), F32)
    return pl.pallas_call(
        functools.partial(_lru_kernel, seq=seq),
        grid=(nb,),
        in_specs=[tok_in, tok_in, state, _const_spec(conv_w.shape), _const_spec(conv_b.shape),
                  _const_spec(wa.shape), _const_spec(wi.shape), _const_spec(ba.shape), _const_spec(bi.shape),
                  _const_spec(coef.shape)],
        out_specs=[tok, state],
        out_shape=[jax.ShapeDtypeStruct((t, D_LRU), BF16), jax.ShapeDtypeStruct((nb, 2, D_LRU), F32)],
        scratch_shapes=[pltpu.VMEM((N_SLABS, seq + 2 * SUBLANES, LANES), F32),
                        pltpu.VMEM((N_SLABS, seq, LANES), F32), scan_buf, scan_buf, scan_buf,
                        pltpu.VMEM((2, N_LRU_TILES, LRU_TILE, 2 * LRU_TILE), BF16)],
        compiler_params=_params(),
        name="rglru",
    )(xl, gl, h0, conv_w, conv_b, wa, wi, ba, bi, coef)


def kernel(x_prompt, x_sample, cache_k, cache_v, state_lru, c, c_ctx, w_mod, b_mod, norm_g, ffn1_w_in, ffn1_w_out,
           w_in, rpb, conv_w, conv_b, lru_wa, lru_ba, lru_wi, lru_bi, lru_lambda, w_br_attn, w_br_lru, w_out,
           ffn2_w_in, ffn2_w_out, final_g):
    assert w_mod.shape[0] == 1, "single trunk layer"
    nb_ctx, seq_ctx, _ = x_prompt.shape
    nb_lat, seq_lat, _ = x_sample.shape

    n_cond = 1 + nb_lat
    cond = jnp.concatenate([c_ctx[None, :], c, jnp.zeros((2 * SUBLANES - n_cond, D_MODEL), F32)], axis=0)
    mods3 = _mods(cond, w_mod[0], b_mod[0]).reshape(2 * SUBLANES, N_MOD, D_MODEL)

    wgu1, wo1, win = ffn1_w_in[0].astype(BF16), ffn1_w_out[0].astype(BF16), w_in[0].astype(BF16)
    late_weights = (w_br_attn[0], w_br_lru[0], w_out[0], ffn2_w_in[0], ffn2_w_out[0])
    coef = (0.5 * float(np.log2(np.e))) * (-LRU_C * jax.nn.softplus(-lru_lambda[0]))
    ba = 0.5 * lru_ba[0]
    bi = 0.5 * lru_bi[0]
    ng = norm_g[0]
    fg = final_g.reshape(1, D_MODEL)
    cb = conv_b[0].reshape(1, D_LRU)

    t_ctx = nb_ctx * seq_ctx
    assert t_ctx % seq_lat == 0, "latent sequences must start on a latent-sequence-sized row block"
    xp = x_prompt.reshape(t_ctx, D_MODEL)
    xs = x_sample.reshape(nb_lat * seq_lat, D_MODEL)

    x1, qkv, xl, gl, gates, new_k, new_v, wba, wbl, wout, wgu2, wo2 = _ffn1(
        xp, xs, mods3, ng, wgu1, wo1, win, late_weights, nb_ctx, seq_ctx, seq_lat)

    attnp = _ctx_attention(qkv, nb_ctx, seq_ctx)
    lrup, h_fin = _lru(xl, gl, jnp.zeros((nb_ctx, 2, D_LRU), F32), conv_w[0], cb, lru_wa[0], lru_wi[0],
                       ba, bi, coef, 0, nb_ctx, seq_ctx)

    rel = _na_rel_rows(rpb[0])
    first_lat = t_ctx // seq_lat
    attns = _na_attention(qkv, cache_k.swapaxes(3, 4), cache_v.swapaxes(3, 4), rel, first_lat, nb_lat, seq_lat)
    lrus, _ = _lru(xl, gl, state_lru[:, 0], conv_w[0], cb, lru_wa[0], lru_wi[0], ba, bi, coef,
                   first_lat, nb_lat, seq_lat)

    yp, ys = _ffn2(x1, attnp, attns, lrup, lrus, gates, mods3, ng, fg, wba, wbl, wout, wgu2, wo2, 512, seq_lat)
    return (yp.reshape(nb_ctx, seq_ctx, D_MODEL), ys.reshape(nb_lat, seq_lat, D_MODEL),
            new_k, new_v, h_fin[:, None])
```

```python
import functools

import numpy as np
import jax
import jax.numpy as jnp
from jax import lax
from jax.experimental import pallas as pl
from jax.experimental.pallas import tpu as pltpu

F32 = jnp.float32
BF16 = jnp.bfloat16

D_MODEL = 1024
N_HEADS = 8
HEAD_DIM = 64
D_ATTN = N_HEADS * HEAD_DIM
GRID_W = 64
WIN_H = 8
WIN_W = 16
D_LRU = 1024
LRU_BLOCKS = 16
LRU_BLOCK = D_LRU // LRU_BLOCKS
CONV_W = 4
LRU_C = 8.0
D_FF = 2816
N_MOD = 9
EPS = 1e-6
NEG_INF = -1e30
F32_TINY = float(np.finfo(np.float32).tiny)
ATTN_SCALE = HEAD_DIM ** -0.5

LANES = 128
SUBLANES = 8
MXU_DIM = 256
VMEM_LIMIT_BYTES = 56 * 1024 * 1024

FF_CHUNK = MXU_DIM
N_FF_CHUNKS = D_FF // FF_CHUNK
LRU_TILE = MXU_DIM
N_LRU_TILES = D_LRU // LRU_TILE
N_SLABS = D_LRU // LANES
N_SEG = SUBLANES
SEG_PAD = 4
SCAN_UNROLL = 12
ROWS_PER_GROUP = 4


def _sigmoid(x):
    return 0.5 * jnp.tanh(0.5 * x) + 0.5


def _rms_mod(x, g, shift, scale):
    y = x * lax.rsqrt(jnp.mean(x * x, axis=-1, keepdims=True) + EPS)
    y = y * g
    return y * (1 + scale) + shift


def _dot(a, b):
    return jnp.dot(a, b, preferred_element_type=F32)


def _dot_nt(a, b):
    return lax.dot_general(a, b, (((1,), (1,)), ((), ())), preferred_element_type=F32)


def _params(n_axes=1):
    return pltpu.CompilerParams(dimension_semantics=("arbitrary",) * n_axes,
                                vmem_limit_bytes=VMEM_LIMIT_BYTES)


def _const_spec(shape):
    nd = len(shape)
    return pl.BlockSpec(shape, lambda *_: (0,) * nd, pipeline_mode=pl.Buffered(1))


def _mods_kernel(cond_ref, w_ref, b_ref, o_ref):
    c = cond_ref[...]
    s = (c * _sigmoid(c)).astype(BF16)
    o_ref[...] = _dot(s, w_ref[...].astype(BF16)) + b_ref[...]


def _mods(cond, w_mod, b_mod):
    n = w_mod.shape[1]
    bn = 3 * D_MODEL
    return pl.pallas_call(
        _mods_kernel,
        grid=(n // bn,),
        in_specs=[pl.BlockSpec(cond.shape, lambda j: (0, 0)),
                  pl.BlockSpec((D_MODEL, bn), lambda j: (0, j)),
                  pl.BlockSpec((1, bn), lambda j: (0, j))],
        out_specs=pl.BlockSpec((cond.shape[0], bn), lambda j: (0, j)),
        out_shape=jax.ShapeDtypeStruct((cond.shape[0], n), F32),
        compiler_params=_params(),
        name="mods",
    )(cond, w_mod, b_mod.reshape(1, n))


def _swiglu(h_ref, a_ref, wgu_ref, wo_ref):
    h = h_ref[...]
    for j in range(N_FF_CHUNKS):
        cols = slice(j * FF_CHUNK, (j + 1) * FF_CHUNK)
        g = _dot(h, wgu_ref[:, cols])
        u = _dot(h, wgu_ref[:, D_FF + j * FF_CHUNK:D_FF + (j + 1) * FF_CHUNK])
        a_ref[:, cols] = ((g * _sigmoid(g)) * u).astype(BF16)
    return _dot(a_ref[...], wo_ref[...])


BF16_ROWS = 2 * SUBLANES


def _cast_spec(w, n_steps):
    rows = BF16_ROWS * pl.cdiv(w.shape[0], BF16_ROWS * n_steps)
    n_chunks = w.shape[0] // rows
    assert n_chunks * rows == w.shape[0] and n_chunks <= n_steps
    return pl.BlockSpec((rows, w.shape[1]), lambda i: (jnp.minimum(i, n_chunks - 1), 0))


def _two_stream(n_first, tm_rows):
    first = lambda w: pl.BlockSpec((tm_rows, w), lambda i: (jnp.minimum(i, n_first - 1), 0))
    second = lambda w: pl.BlockSpec((tm_rows, w), lambda i: (jnp.maximum(i - n_first, 0), 0))
    return first, second


def _mod_rows(mods_ref, n_ctx_tiles, tiles_per_seq):
    i = pl.program_id(0)
    row = jnp.where(i < n_ctx_tiles, 0, 1 + jnp.maximum(i - n_ctx_tiles, 0) // tiles_per_seq)
    m = mods_ref[pl.ds(row, 1), :]
    return [m[:, k * D_MODEL:(k + 1) * D_MODEL] for k in range(N_MOD)]


N_LATE_WEIGHTS = 5
def _ffn1_kernel(xp_ref, xs_ref, mod_ref, ng_ref, wgu_ref, wo_ref, win_ref, *rest, n_ctx_tiles, tiles_per_seq):
    late_f32 = rest[:N_LATE_WEIGHTS]
    x1_ref, qkv_ref, xl_ref, gl_ref, gates_ref, newk_ref, newv_ref = rest[N_LATE_WEIGHTS:N_LATE_WEIGHTS + 7]
    late_bf16 = rest[N_LATE_WEIGHTS + 7:2 * N_LATE_WEIGHTS + 7]
    a_ref, h_ref = rest[2 * N_LATE_WEIGHTS + 7:]

    for src, dst in zip(late_f32, late_bf16):
        dst[...] = src[...].astype(BF16)

    is_ctx = pl.program_id(0) < n_ctx_tiles
    x = jnp.where(is_ctx, xp_ref[...], xs_ref[...])
    m = _mod_rows(mod_ref, n_ctx_tiles, tiles_per_seq)
    h_ref[...] = _rms_mod(x, ng_ref[0:1], m[0], m[1]).astype(BF16)
    x1 = x + (0.5 * m[2]) * _swiglu(h_ref, a_ref, wgu_ref, wo_ref)
    x1_ref[...] = x1
    h2 = _rms_mod(x1, ng_ref[1:2], m[3], m[4]).astype(BF16)
    c0, c1, c2, c3 = 3 * D_ATTN, 3 * D_ATTN + D_LRU, 3 * D_ATTN + 2 * D_LRU, 3 * D_ATTN + 2 * D_LRU + 2 * D_MODEL
    qkv = _dot(h2, win_ref[:, 0:c0])
    qkv_ref[:, :D_ATTN] = (qkv[:, :D_ATTN] * ATTN_SCALE).astype(BF16)
    qkv_ref[:, D_ATTN:] = qkv[:, D_ATTN:].astype(BF16)
    ctx_block = jnp.broadcast_to(is_ctx, newk_ref.shape)
    for ref, off in ((newk_ref, D_ATTN), (newv_ref, 2 * D_ATTN)):
        heads = [qkv[:, off + h * HEAD_DIM:off + (h + 1) * HEAD_DIM] for h in range(N_HEADS)]
        pltpu.store(ref, jnp.stack(heads, axis=0)[None, None], mask=ctx_block)
    xl_ref[...] = _dot(h2, win_ref[:, c0:c1])
    gl_ref[...] = _dot(h2, win_ref[:, c1:c2])
    gates_ref[...] = _dot(h2, win_ref[:, c2:c3])


def _ffn1(xp, xs, mods, norm_g, wgu, wo, win, late_weights, nb_ctx, seq_ctx, seq_lat):
    tm = seq_ctx
    n_ctx_tiles = xp.shape[0] // tm
    t = xp.shape[0] + xs.shape[0]
    n_steps = t // tm
    assert len(late_weights) == N_LATE_WEIGHTS
    row = lambda w: pl.BlockSpec((tm, w), lambda i: (i, 0))
    first, second = _two_stream(n_ctx_tiles, tm)
    cache_spec = pl.BlockSpec((1, 1, N_HEADS, seq_ctx, HEAD_DIM),
                              lambda i: (jnp.minimum(i, n_ctx_tiles - 1), 0, 0, 0, 0))
    cache_shape = jax.ShapeDtypeStruct((nb_ctx, 1, N_HEADS, seq_ctx, HEAD_DIM), F32)
    cast_specs = [_cast_spec(w, n_steps) for w in late_weights]
    return pl.pallas_call(
        functools.partial(_ffn1_kernel, n_ctx_tiles=n_ctx_tiles, tiles_per_seq=seq_lat // tm),
        grid=(n_steps,),
        in_specs=[first(D_MODEL), second(D_MODEL), _const_spec(mods.shape),
                  _const_spec(norm_g.shape), _const_spec(wgu.shape), _const_spec(wo.shape),
                  _const_spec(win.shape)] + cast_specs,
        out_specs=[row(D_MODEL), row(3 * D_ATTN), row(D_LRU), row(D_LRU), row(2 * D_MODEL),
                   cache_spec, cache_spec] + cast_specs,
        out_shape=[jax.ShapeDtypeStruct((t, D_MODEL), F32),
                   jax.ShapeDtypeStruct((t, 3 * D_ATTN), BF16),
                   jax.ShapeDtypeStruct((t, D_LRU), F32),
                   jax.ShapeDtypeStruct((t, D_LRU), F32),
                   jax.ShapeDtypeStruct((t, 2 * D_MODEL), F32),
                   cache_shape, cache_shape] + [jax.ShapeDtypeStruct(w.shape, BF16) for w in late_weights],
        scratch_shapes=[pltpu.VMEM((tm, D_FF), BF16), pltpu.VMEM((tm, D_MODEL), BF16)],
        compiler_params=_params(),
        name="ffn1_inproj",
    )(xp, xs, mods, norm_g, wgu, wo, win, *late_weights)


def _ffn2_kernel(x1_ref, attnp_ref, attns_ref, lrup_ref, lrus_ref, gates_ref, mod_ref, ng_ref, fg_ref,
                 wba_ref, wbl_ref, wout_ref, wgu_ref, wo_ref, yp_ref, ys_ref, a_ref, h_ref,
                 *, n_ctx_tiles, tiles_per_seq):
    is_ctx = pl.program_id(0) < n_ctx_tiles
    m = _mod_rows(mod_ref, n_ctx_tiles, tiles_per_seq)
    attn = jnp.where(is_ctx, attnp_ref[...], attns_ref[...])
    lru = jnp.where(is_ctx, lrup_ref[...], lrus_ref[...])
    gt = gates_ref[...]
    mm = (_sigmoid(gt[:, :D_MODEL]) * _dot(attn, wba_ref[...])
          + _sigmoid(gt[:, D_MODEL:]) * _dot(lru, wbl_ref[...]))
    x2 = x1_ref[...] + m[5] * _dot(mm.astype(BF16), wout_ref[...])
    h_ref[...] = _rms_mod(x2, ng_ref[2:3], m[6], m[7]).astype(BF16)
    x3 = x2 + (0.5 * m[8]) * _swiglu(h_ref, a_ref, wgu_ref, wo_ref)
    y = x3 * lax.rsqrt(jnp.mean(x3 * x3, axis=-1, keepdims=True) + EPS)
    y = y * fg_ref[...]
    ctx_tile = jnp.broadcast_to(is_ctx, y.shape)
    pltpu.store(yp_ref, y, mask=ctx_tile)
    pltpu.store(ys_ref, y, mask=jnp.logical_not(ctx_tile))


def _ffn2(x1, attnp, attns, lrup, lrus, gates, mods, norm_g, final_g, wba, wbl, wout, wgu, wo, tm, seq_lat):
    t = x1.shape[0]
    t_ctx = attnp.shape[0]
    n_ctx_tiles = t_ctx // tm
    row = lambda w: pl.BlockSpec((tm, w), lambda i: (i, 0))
    first, second = _two_stream(n_ctx_tiles, tm)
    return pl.pallas_call(
        functools.partial(_ffn2_kernel, n_ctx_tiles=n_ctx_tiles, tiles_per_seq=seq_lat // tm),
        grid=(t // tm,),
        in_specs=[row(D_MODEL), first(D_ATTN), second(D_ATTN), first(D_LRU), second(D_LRU), row(2 * D_MODEL),
                  _const_spec(mods.shape),
                  _const_spec(norm_g.shape), _const_spec(final_g.shape),
                  _const_spec(wba.shape), _const_spec(wbl.shape), _const_spec(wout.shape),
                  _const_spec(wgu.shape), _const_spec(wo.shape)],
        out_specs=[first(D_MODEL), second(D_MODEL)],
        out_shape=[jax.ShapeDtypeStruct((t_ctx, D_MODEL), F32), jax.ShapeDtypeStruct((t - t_ctx, D_MODEL), F32)],
        scratch_shapes=[pltpu.VMEM((tm, D_FF), BF16), pltpu.VMEM((tm, D_MODEL), BF16)],
        compiler_params=_params(),
        name="merge_ffn2",
    )(x1, attnp, attns, lrup, lrus, gates, mods, norm_g, final_g, wba, wbl, wout, wgu, wo)


def _softmax_pv(scores, values, value_dots=None):
    value_dots = value_dots or [_dot] * len(values)
    m = scores[0].max(axis=-1, keepdims=True)
    for s in scores[1:]:
        m = jnp.maximum(m, s.max(axis=-1, keepdims=True))
    l = None
    o = None
    for s, v, dot in zip(scores, values, value_dots):
        p = jnp.exp(s - m)
        ls = p.sum(axis=-1, keepdims=True)
        os_ = dot(p.astype(BF16), v)
        l = ls if l is None else l + ls
        o = os_ if o is None else o + os_
    return o / l


def _ctx_attn_kernel(qkv_ref, o_ref):
    seq = qkv_ref.shape[0]
    lane = lax.broadcasted_iota(jnp.int32, (seq, LANES), 1)
    first = lane < HEAD_DIM
    for hp in range(D_ATTN // LANES):
        q = qkv_ref[:, hp * LANES:(hp + 1) * LANES]
        k = qkv_ref[:, D_ATTN + hp * LANES:D_ATTN + (hp + 1) * LANES]
        v = qkv_ref[:, 2 * D_ATTN + hp * LANES:2 * D_ATTN + (hp + 1) * LANES]
        qm = jnp.concatenate([jnp.where(first, q, jnp.zeros_like(q)), jnp.where(first, jnp.zeros_like(q), q)], axis=0)
        o = _softmax_pv([_dot_nt(qm, k)], [v])
        o_ref[:, hp * LANES:(hp + 1) * LANES] = jnp.where(first, o[:seq], o[seq:]).astype(BF16)


def _ctx_attention(qkv, nb, seq):
    return pl.pallas_call(
        _ctx_attn_kernel,
        grid=(nb,),
        in_specs=[pl.BlockSpec((seq, 3 * D_ATTN), lambda b: (b, 0))],
        out_specs=pl.BlockSpec((seq, D_ATTN), lambda b: (b, 0)),
        out_shape=jax.ShapeDtypeStruct((nb * seq, D_ATTN), BF16),
        compiler_params=_params(),
        name="ctx_attention",
    )(qkv)


def _na_groups(rows):
    kh = min(WIN_H, rows)
    r = np.arange(rows)
    rs = np.clip(r - kh // 2, 0, rows - kh)
    groups = []
    for g in range(rows // ROWS_PER_GROUP):
        qr = r[g * ROWS_PER_GROUP:(g + 1) * ROWS_PER_GROUP]
        k0, k1 = int(rs[qr].min()), int(rs[qr].max()) + kh
        if (k1 - k0) % 2:
            if k1 < rows:
                k1 += 1
            else:
                k0 -= 1
        groups.append((k0, k1))
    return groups, rs, kh


def _na_rel_rows(rpb):
    edge = GRID_W - WIN_W
    v = jnp.concatenate([rpb[..., WIN_W - 1:],
                         jnp.repeat(rpb[..., -1:], edge, axis=-1),
                         jnp.zeros(rpb.shape[:-1] + (1,), F32),
                         jnp.repeat(rpb[..., :1], edge, axis=-1),
                         rpb[..., :WIN_W - 1]], axis=-1).astype(F32)
    return jnp.pad(v, ((0, 0), (0, 1), (0, 0)))


def _na_build_bias(rel_ref, bias_ref, groups, rs, kh):
    shape = (GRID_W, LANES)
    qc = lax.broadcasted_iota(jnp.int32, shape, 0)
    lane = lax.broadcasted_iota(jnp.int32, shape, 1)
    left = lane < GRID_W
    kc = jnp.where(left, lane, lane - GRID_W)
    cs = jnp.clip(qc - WIN_W // 2, 0, GRID_W - WIN_W)
    in_win = (kc >= cs) & (kc < cs + WIN_W)
    neg = jnp.full(shape, NEG_INF, F32)
    for hh in range(2):
        def tile(r, j, half, hh=hh):
            row = jnp.broadcast_to(rel_ref[hh, j - r + WIN_H - 1:j - r + WIN_H, :], shape)
            return pltpu.roll(row, half * GRID_W, 1, stride=1, stride_axis=0)

        boff = 0
        for g, (k0, k1) in enumerate(groups):
            for i in range(ROWS_PER_GROUP):
                r = g * ROWS_PER_GROUP + i
                for jp in range((k1 - k0) // 2):
                    j = k0 + 2 * jp
                    a_ok = rs[r] <= j < rs[r] + kh
                    b_ok = rs[r] <= j + 1 < rs[r] + kh
                    if a_ok and b_ok:
                        t = jnp.where(in_win, jnp.where(left, tile(r, j, 0), tile(r, j + 1, 1)), neg)
                    elif a_ok:
                        t = jnp.where(in_win & left, tile(r, j, 0), neg)
                    elif b_ok:
                        t = jnp.where(in_win & jnp.logical_not(left), tile(r, j + 1, 1), neg)
                    else:
                        t = neg
                    bias_ref[hh, i * GRID_W:(i + 1) * GRID_W, boff + jp * LANES:boff + (jp + 1) * LANES] = t
            boff += (k1 - k0) * GRID_W


def _na_attn_kernel(q_ref, k_ref, v_ref, kc_ref, vc_ref, rel_ref, o_ref, bias_ref, *, groups, rs, kh):
    @pl.when(pl.program_id(1) == 0)
    def _():
        _na_build_bias(rel_ref, bias_ref, groups, rs, kh)

    gq = ROWS_PER_GROUP * GRID_W
    lane = lax.broadcasted_iota(jnp.int32, (gq, LANES), 1)
    first = lane < HEAD_DIM

    past = kc_ref.shape[-1]
    kc = kc_ref[0, 0].reshape(LANES, past).astype(BF16)
    vc = vc_ref[0, 0].reshape(LANES, past).astype(BF16)
    for g, (k0, k1) in enumerate(groups):
        q = q_ref[g * gq:(g + 1) * gq, :]
        k = k_ref[k0 * GRID_W:k1 * GRID_W, :]
        v = v_ref[k0 * GRID_W:k1 * GRID_W, :]
        boff = sum((b - a) * GRID_W for a, b in groups[:g])
        qm = jnp.concatenate([jnp.where(first, q, jnp.zeros_like(q)), jnp.where(first, jnp.zeros_like(q), q)], axis=0)
        bias = jnp.concatenate([bias_ref[hh, :, boff:boff + (k1 - k0) * GRID_W] for hh in range(2)], axis=0)
        o = _softmax_pv([_dot_nt(qm, k) + bias, _dot(qm, kc)], [v, vc], [_dot, _dot_nt])
        o_ref[g * gq:(g + 1) * gq, :] = jnp.where(first, o[:gq], o[gq:]).astype(BF16)


def _na_attention(qkv, kc, vc, rel, first_seq, nb, seq):
    t = nb * seq
    past = kc.shape[4]
    npair = D_ATTN // LANES
    groups, rs, kh = _na_groups(seq // GRID_W)
    width = sum(k1 - k0 for k0, k1 in groups) * GRID_W
    tok = lambda off: pl.BlockSpec((seq, LANES), lambda hp, b: (first_seq + b, off + hp))
    ctx = pl.BlockSpec((1, 1, 2, HEAD_DIM, past), lambda hp, b: (b, 0, hp, 0, 0))
    return pl.pallas_call(
        functools.partial(_na_attn_kernel, groups=groups, rs=[int(x) for x in rs], kh=kh),
        grid=(npair, nb),
        in_specs=[tok(0), tok(npair), tok(2 * npair), ctx, ctx,
                  pl.BlockSpec((2,) + rel.shape[1:], lambda hp, b: (hp, 0, 0))],
        out_specs=pl.BlockSpec((seq, LANES), lambda hp, b: (b, hp)),
        out_shape=jax.ShapeDtypeStruct((t, D_ATTN), BF16),
        scratch_shapes=[pltpu.VMEM((2, ROWS_PER_GROUP * GRID_W, width), F32)],
        compiler_params=_params(2),
        name="na_attention",
    )(qkv, qkv, qkv, kc, vc, rel)


def _gelu_tanh(x):
    c = float(np.sqrt(2.0 / np.pi))
    t = jnp.tanh(x * (c + (c * 0.044715) * (x * x)))
    return x * (0.5 * t + 0.5)


def _build_blockdiag(wa_ref, wi_ref, wbd_ref):
    per = LRU_TILE // LRU_BLOCK
    wbd_ref[...] = jnp.zeros(wbd_ref.shape, BF16)
    zero = jnp.zeros((LRU_BLOCK, LRU_BLOCK), F32)
    for d in range(2):
        for t in range(N_LRU_TILES):
            for p in range(per):
                rows = slice(p * LRU_BLOCK, (p + 1) * LRU_BLOCK)
                for half, w_ref in enumerate((wa_ref, wi_ref)):
                    blk = 0.5 * w_ref[d, t * per + p]
                    piece = jnp.concatenate([blk, zero] if p % 2 == 0 else [zero, blk], axis=1)
                    c0 = half * LRU_TILE + (p // 2) * LANES
                    wbd_ref[d, t, rows, c0:c0 + LANES] = piece.astype(BF16)


def _lru_kernel(xl_ref, gl_ref, h0_ref, cw_ref, cb_ref, wa_ref, wi_ref, ba_ref, bi_ref, coef_ref,
                y_ref, hfin_ref, xpad, xc, a_buf, hf_buf, hb_buf, wbd_ref, *, seq):
    pitch = seq // N_SEG + SEG_PAD
    lp = N_SEG * pitch
    halo = SUBLANES
    chunk = LANES
    n_chunks = seq // chunk
    left = CONV_W // 2

    @pl.when(pl.program_id(0) == 0)
    def _():
        _build_blockdiag(wa_ref, wi_ref, wbd_ref)

    for k in range(N_SLABS):
        cols = slice(k * LANES, (k + 1) * LANES)
        xpad[k, 0:halo, :] = jnp.zeros((halo, LANES), F32)
        xpad[k, halo + seq:2 * halo + seq, :] = jnp.zeros((halo, LANES), F32)
        xpad[k, halo:halo + seq, :] = xl_ref[:, cols]

    def conv_body(i, carry):
        r0 = pl.multiple_of(i * chunk, chunk)
        for k in range(N_SLABS):
            cols = slice(k * LANES, (k + 1) * LANES)
            out = cb_ref[:, cols]
            for j in range(CONV_W):
                out = out + xpad[k, pl.ds(r0 + halo + j - left, chunk), :] * cw_ref[j:j + 1, cols]
            xc[k, pl.ds(r0, chunk), :] = out
        return carry

    lax.fori_loop(0, n_chunks, conv_body, 0)

    a_buf[:, seq:lp, :] = jnp.ones((N_SLABS, lp - seq, LANES), F32)
    hf_buf[:, seq:lp, :] = jnp.zeros((N_SLABS, lp - seq, LANES), F32)
    hb_buf[:, seq:lp, :] = jnp.zeros((N_SLABS, lp - seq, LANES), F32)

    row = lax.broadcasted_iota(jnp.int32, (N_SEG, LANES), 0)
    for d, h_buf in enumerate((hf_buf, hb_buf)):
        def gates_body(i, carry, d=d, h_buf=h_buf):
            rows = pl.ds(pl.multiple_of(i * chunk, chunk), chunk)
            per = LRU_TILE // LANES
            for c in range(N_LRU_TILES):
                xs = [xc[c * per + half, rows, :] for half in range(per)]
                pre = _dot(jnp.concatenate(xs, axis=1).astype(BF16), wbd_ref[d, c])
                for half in range(per):
                    k = c * per + half
                    cols = slice(k * LANES, (k + 1) * LANES)
                    tr = jnp.tanh(pre[:, half * LANES:(half + 1) * LANES] + ba_ref[d:d + 1, cols])
                    ti = jnp.tanh(pre[:, LRU_TILE + half * LANES:LRU_TILE + (half + 1) * LANES]
                                  + bi_ref[d:d + 1, cols])
                    a = jnp.exp2(coef_ref[d:d + 1, cols] * tr + coef_ref[d:d + 1, cols])
                    y = (1.0 - a) * (1.0 + a)
                    mult = y * lax.rsqrt(jnp.maximum(y, F32_TINY))
                    a_buf[k, rows, :] = a
                    h_buf[k, rows, :] = mult * ((0.5 * ti + 0.5) * xs[half])
            return carry

        lax.fori_loop(0, n_chunks, gates_body, 0, unroll=min(4, n_chunks))

        base0 = jnp.int32(0 if d == 0 else pitch - SCAN_UNROLL)
        base_step = SCAN_UNROLL if d == 0 else -SCAN_UNROLL

        def seg_rows(base, u, d=d):
            return pl.ds(base + (u if d == 0 else SCAN_UNROLL - 1 - u), N_SEG, stride=pitch)

        def totals_body(_, carry, h_buf=h_buf):
            base, hs, ps = carry[0], list(carry[1]), list(carry[2])
            for u in range(SCAN_UNROLL):
                idx = seg_rows(base, u)
                for k in range(N_SLABS):
                    a = a_buf[k, idx, :]
                    hs[k] = a * hs[k] + h_buf[k, idx, :]
                    ps[k] = ps[k] * a
            return base + base_step, tuple(hs), tuple(ps)

        zeros = tuple(jnp.zeros((N_SEG, LANES), F32) for _ in range(N_SLABS))
        ones = tuple(jnp.ones((N_SEG, LANES), F32) for _ in range(N_SLABS))
        _, h_end, p_end = lax.fori_loop(0, pitch // SCAN_UNROLL, totals_body, (base0, zeros, ones))

        order = range(N_SEG) if d == 0 else range(N_SEG - 1, -1, -1)
        starts = []
        for k in range(N_SLABS):
            c = h0_ref[0, d:d + 1, k * LANES:(k + 1) * LANES]
            cm = jnp.zeros((N_SEG, LANES), F32)
            for s in order:
                cm = jnp.where(row == s, c, cm)
                c = p_end[k][s:s + 1, :] * c + h_end[k][s:s + 1, :]
            starts.append(cm)
            hfin_ref[0, d:d + 1, k * LANES:(k + 1) * LANES] = c

        def scan_body(_, carry, h_buf=h_buf):
            base, hs = carry[0], list(carry[1])
            for u in range(SCAN_UNROLL):
                idx = seg_rows(base, u)
                for k in range(N_SLABS):
                    hs[k] = a_buf[k, idx, :] * hs[k] + h_buf[k, idx, :]
                    h_buf[k, idx, :] = hs[k]
            return base + base_step, tuple(hs)

        lax.fori_loop(0, pitch // SCAN_UNROLL, scan_body, (base0, tuple(starts)))

    def out_body(i, carry):
        r0 = pl.multiple_of(i * chunk, chunk)
        for k in range(N_SLABS):
            cols = slice(k * LANES, (k + 1) * LANES)
            h = hf_buf[k, pl.ds(r0, chunk), :] + hb_buf[k, pl.ds(r0, chunk), :]
            y_ref[pl.ds(r0, chunk), cols] = (h * _gelu_tanh(gl_ref[pl.ds(r0, chunk), cols])).astype(BF16)
        return carry

    lax.fori_loop(0, n_chunks, out_body, 0)


def _lru(xl, gl, h0, conv_w, conv_b, wa, wi, ba, bi, coef, first_seq, nb, seq):
    t = nb * seq
    lp = N_SEG * (seq // N_SEG + SEG_PAD)
    tok_in = pl.BlockSpec((seq, D_LRU), lambda b: (first_seq + b, 0))
    tok = pl.BlockSpec((seq, D_LRU), lambda b: (b, 0))
    state = pl.BlockSpec((1, 2, D_LRU), lambda b: (b, 0, 0))
    scan_buf = pltpu.VMEM((N_SLABS, lp, LANES), F32)
    return pl.pallas_call(
        functools.partial(_lru_kernel, seq=seq),
        grid=(nb,),
        in_specs=[tok_in, tok_in, state, _const_spec(conv_w.shape), _const_spec(conv_b.shape),
                  _const_spec(wa.shape), _const_spec(wi.shape), _const_spec(ba.shape), _const_spec(bi.shape),
                  _const_spec(coef.shape)],
        out_specs=[tok, state],
        out_shape=[jax.ShapeDtypeStruct((t, D_LRU), BF16), jax.ShapeDtypeStruct((nb, 2, D_LRU), F32)],
        scratch_shapes=[pltpu.VMEM((N_SLABS, seq + 2 * SUBLANES, LANES), F32),
                        pltpu.VMEM((N_SLABS, seq, LANES), F32), scan_buf, scan_buf, scan_buf,
                        pltpu.VMEM((2, N_LRU_TILES, LRU_TILE, 2 * LRU_TILE), BF16)],
        compiler_params=_params(),
        name="rglru",
    )(xl, gl, h0, conv_w, conv_b, wa, wi, ba, bi, coef)


def kernel(x_prompt, x_sample, cache_k, cache_v, state_lru, c, c_ctx, w_mod, b_mod, norm_g, ffn1_w_in, ffn1_w_out,
           w_in, rpb, conv_w, conv_b, lru_wa, lru_ba, lru_wi, lru_bi, lru_lambda, w_br_attn, w_br_lru, w_out,
           ffn2_w_in, ffn2_w_out, final_g):
    assert w_mod.shape[0] == 1, "single trunk layer"
    nb_ctx, seq_ctx, _ = x_prompt.shape
    nb_lat, seq_lat, _ = x_sample.shape

    n_cond = 1 + nb_lat
    cond = jnp.concatenate([c_ctx[None, :], c, jnp.zeros((2 * SUBLANES - n_cond, D_MODEL), F32)], axis=0)
    mods = _mods(cond, w_mod[0], b_mod[0])

    wgu1, wo1, win = ffn1_w_in[0].astype(BF16), ffn1_w_out[0].astype(BF16), w_in[0].astype(BF16)
    late_weights = (w_br_attn[0], w_br_lru[0], w_out[0], ffn2_w_in[0], ffn2_w_out[0])
    coef = (0.5 * float(np.log2(np.e))) * (-LRU_C * jax.nn.softplus(-lru_lambda[0]))
    ba = 0.5 * lru_ba[0]
    bi = 0.5 * lru_bi[0]
    ng = norm_g[0]
    fg = final_g.reshape(1, D_MODEL)
    cb = conv_b[0].reshape(1, D_LRU)

    t_ctx = nb_ctx * seq_ctx
    assert t_ctx % seq_lat == 0, "latent sequences must start on a latent-sequence-sized row block"
    xp = x_prompt.reshape(t_ctx, D_MODEL)
    xs = x_sample.reshape(nb_lat * seq_lat, D_MODEL)

    x1, qkv, xl, gl, gates, new_k, new_v, wba, wbl, wout, wgu2, wo2 = _ffn1(
        xp, xs, mods, ng, wgu1, wo1, win, late_weights, nb_ctx, seq_ctx, seq_lat)

    attnp = _ctx_attention(qkv, nb_ctx, seq_ctx)
    lrup, h_fin = _lru(xl, gl, jnp.zeros((nb_ctx, 2, D_LRU), F32), conv_w[0], cb, lru_wa[0], lru_wi[0],
                       ba, bi, coef, 0, nb_ctx, seq_ctx)

    rel = _na_rel_rows(rpb[0])
    first_lat = t_ctx // seq_lat
    attns = _na_attention(qkv, cache_k.swapaxes(3, 4), cache_v.swapaxes(3, 4), rel, first_lat, nb_lat, seq_lat)
    lrus, _ = _lru(xl, gl, state_lru[:, 0], conv_w[0], cb, lru_wa[0], lru_wi[0], ba, bi, coef,
                   first_lat, nb_lat, seq_lat)

    yp, ys = _ffn2(x1, attnp, attns, lrup, lrus, gates, mods, ng, fg, wba, wbl, wout, wgu2, wo2, 512, seq_lat)
    return (yp.reshape(nb_ctx, seq_ctx, D_MODEL), ys.reshape(nb_lat, seq_lat, D_MODEL),
            new_k, new_v, h_fin[:, None])
```

```python
import functools

import numpy as np
import jax
import jax.numpy as jnp
from jax import lax
from jax.experimental import pallas as pl
from jax.experimental.pallas import tpu as pltpu

F32 = jnp.float32
BF16 = jnp.bfloat16

D_MODEL = 1024
N_HEADS = 8
HEAD_DIM = 64
D_ATTN = N_HEADS * HEAD_DIM
GRID_W = 64
WIN_H = 8
WIN_W = 16
D_LRU = 1024
LRU_BLOCKS = 16
LRU_BLOCK = D_LRU // LRU_BLOCKS
CONV_W = 4
LRU_C = 8.0
D_FF = 2816
N_MOD = 9
EPS = 1e-6
NEG_INF = -1e30
F32_TINY = float(np.finfo(np.float32).tiny)
ATTN_SCALE = HEAD_DIM ** -0.5

LANES = 128
SUBLANES = 8
MXU_DIM = 256
VMEM_LIMIT_BYTES = 56 * 1024 * 1024

FF_CHUNK = MXU_DIM
N_FF_CHUNKS = D_FF // FF_CHUNK
LRU_TILE = MXU_DIM
N_LRU_TILES = D_LRU // LRU_TILE
N_SLABS = D_LRU // LANES
N_SEG = SUBLANES
SEG_PAD = 4
SCAN_UNROLL = 12
ROWS_PER_GROUP = 4


def _sigmoid(x):
    return 0.5 * jnp.tanh(0.5 * x) + 0.5


def _rms_mod(x, g, shift, scale):
    y = x * lax.rsqrt(jnp.mean(x * x, axis=-1, keepdims=True) + EPS)
    y = y * g
    return y * (1 + scale) + shift


def _dot(a, b):
    return jnp.dot(a, b, preferred_element_type=F32)


def _dot_nt(a, b):
    return lax.dot_general(a, b, (((1,), (1,)), ((), ())), preferred_element_type=F32)


def _params(n_axes=1):
    return pltpu.CompilerParams(dimension_semantics=("arbitrary",) * n_axes,
                                vmem_limit_bytes=VMEM_LIMIT_BYTES)


def _const_spec(shape):
    nd = len(shape)
    return pl.BlockSpec(shape, lambda *_: (0,) * nd, pipeline_mode=pl.Buffered(1))


def _mods_kernel(cond_ref, w_ref, b_ref, o_ref):
    c = cond_ref[...]
    s = (c * _sigmoid(c)).astype(BF16)
    o_ref[...] = _dot(s, w_ref[...].astype(BF16)) + b_ref[...]


def _mods(cond, w_mod, b_mod):
    n = w_mod.shape[1]
    bn = 3 * D_MODEL
    return pl.pallas_call(
        _mods_kernel,
        grid=(n // bn,),
        in_specs=[pl.BlockSpec(cond.shape, lambda j: (0, 0)),
                  pl.BlockSpec((D_MODEL, bn), lambda j: (0, j)),
                  pl.BlockSpec((1, bn), lambda j: (0, j))],
        out_specs=pl.BlockSpec((cond.shape[0], bn), lambda j: (0, j)),
        out_shape=jax.ShapeDtypeStruct((cond.shape[0], n), F32),
        compiler_params=_params(),
        name="mods",
    )(cond, w_mod, b_mod.reshape(1, n))


def _swiglu(h_ref, a_ref, wgu_ref, wo_ref):
    h = h_ref[...]
    for j in range(N_FF_CHUNKS):
        cols = slice(j * FF_CHUNK, (j + 1) * FF_CHUNK)
        g = _dot(h, wgu_ref[:, cols])
        u = _dot(h, wgu_ref[:, D_FF + j * FF_CHUNK:D_FF + (j + 1) * FF_CHUNK])
        a_ref[:, cols] = ((g * _sigmoid(g)) * u).astype(BF16)
    return _dot(a_ref[...], wo_ref[...])


BF16_ROWS = 2 * SUBLANES


def _cast_spec(w, n_steps):
    rows = BF16_ROWS * pl.cdiv(w.shape[0], BF16_ROWS * n_steps)
    n_chunks = w.shape[0] // rows
    assert n_chunks * rows == w.shape[0] and n_chunks <= n_steps
    return pl.BlockSpec((rows, w.shape[1]), lambda i: (jnp.minimum(i, n_chunks - 1), 0))


def _two_stream(n_first, tm_rows):
    first = lambda w: pl.BlockSpec((tm_rows, w), lambda i: (jnp.minimum(i, n_first - 1), 0))
    second = lambda w: pl.BlockSpec((tm_rows, w), lambda i: (jnp.maximum(i - n_first, 0), 0))
    return first, second


def _mod_rows(mods_ref, i, n_ctx_tiles, tiles_per_seq):
    row = jnp.where(i < n_ctx_tiles, 0, 1 + jnp.maximum(i - n_ctx_tiles, 0) // tiles_per_seq)
    m = mods_ref[pl.ds(row, 1), :]
    return [m[:, k * D_MODEL:(k + 1) * D_MODEL] for k in range(N_MOD)]


N_LATE_WEIGHTS = 5


def _ffn1_kernel(xp_ref, xs_ref, mod_ref, ng_ref, wgu_ref, wo_ref, win_ref, *rest, n_ctx_tiles, tiles_per_seq):
    late_f32 = rest[:N_LATE_WEIGHTS]
    x1_ref, qkv_ref, xl_ref, gl_ref, gates_ref, newk_ref, newv_ref = rest[N_LATE_WEIGHTS:N_LATE_WEIGHTS + 7]
    late_bf16 = rest[N_LATE_WEIGHTS + 7:2 * N_LATE_WEIGHTS + 7]
    a_ref, h_ref = rest[2 * N_LATE_WEIGHTS + 7:]

    for src, dst in zip(late_f32, late_bf16):
        dst[...] = src[...].astype(BF16)

    is_ctx = pl.program_id(0) < n_ctx_tiles
    x = jnp.where(is_ctx, xp_ref[...], xs_ref[...])
    m = _mod_rows(mod_ref, pl.program_id(0), n_ctx_tiles, tiles_per_seq)
    h_ref[...] = _rms_mod(x, ng_ref[0:1], m[0], m[1]).astype(BF16)
    x1 = x + (0.5 * m[2]) * _swiglu(h_ref, a_ref, wgu_ref, wo_ref)
    x1_ref[...] = x1
    h2 = _rms_mod(x1, ng_ref[1:2], m[3], m[4]).astype(BF16)
    c0, c1, c2, c3 = 3 * D_ATTN, 3 * D_ATTN + D_LRU, 3 * D_ATTN + 2 * D_LRU, 3 * D_ATTN + 2 * D_LRU + 2 * D_MODEL
    qkv = _dot(h2, win_ref[:, 0:c0])
    qkv_ref[:, :D_ATTN] = (qkv[:, :D_ATTN] * ATTN_SCALE).astype(BF16)
    qkv_ref[:, D_ATTN:] = qkv[:, D_ATTN:].astype(BF16)
    ctx_block = jnp.broadcast_to(is_ctx, newk_ref.shape)
    for ref, off in ((newk_ref, D_ATTN), (newv_ref, 2 * D_ATTN)):
        heads = [qkv[:, off + h * HEAD_DIM:off + (h + 1) * HEAD_DIM] for h in range(N_HEADS)]
        pltpu.store(ref, jnp.stack(heads, axis=0)[None, None], mask=ctx_block)
    xl_ref[...] = _dot(h2, win_ref[:, c0:c1])
    gl_ref[...] = _dot(h2, win_ref[:, c1:c2])
    gates_ref[...] = _dot(h2, win_ref[:, c2:c3])


def _ffn1(xp, xs, mods, norm_g, wgu, wo, win, late_weights, nb_ctx, seq_ctx, seq_lat):
    tm = seq_ctx
    n_ctx_tiles = xp.shape[0] // tm
    t = xp.shape[0] + xs.shape[0]
    n_steps = t // tm
    assert len(late_weights) == N_LATE_WEIGHTS
    row = lambda w: pl.BlockSpec((tm, w), lambda i: (i, 0))
    first, second = _two_stream(n_ctx_tiles, tm)
    cache_spec = pl.BlockSpec((1, 1, N_HEADS, seq_ctx, HEAD_DIM),
                              lambda i: (jnp.minimum(i, n_ctx_tiles - 1), 0, 0, 0, 0))
    cache_shape = jax.ShapeDtypeStruct((nb_ctx, 1, N_HEADS, seq_ctx, HEAD_DIM), F32)
    cast_specs = [_cast_spec(w, n_steps) for w in late_weights]
    return pl.pallas_call(
        functools.partial(_ffn1_kernel, n_ctx_tiles=n_ctx_tiles, tiles_per_seq=seq_lat // tm),
        grid=(n_steps,),
        in_specs=[first(D_MODEL), second(D_MODEL), _const_spec(mods.shape),
                  _const_spec(norm_g.shape), _const_spec(wgu.shape), _const_spec(wo.shape),
                  _const_spec(win.shape)] + cast_specs,
        out_specs=[row(D_MODEL), row(3 * D_ATTN), row(D_LRU), row(D_LRU), row(2 * D_MODEL),
                   cache_spec, cache_spec] + cast_specs,
        out_shape=[jax.ShapeDtypeStruct((t, D_MODEL), F32),
                   jax.ShapeDtypeStruct((t, 3 * D_ATTN), BF16),
                   jax.ShapeDtypeStruct((t, D_LRU), F32),
                   jax.ShapeDtypeStruct((t, D_LRU), F32),
                   jax.ShapeDtypeStruct((t, 2 * D_MODEL), F32),
                   cache_shape, cache_shape] + [jax.ShapeDtypeStruct(w.shape, BF16) for w in late_weights],
        scratch_shapes=[pltpu.VMEM((tm, D_FF), BF16), pltpu.VMEM((tm, D_MODEL), BF16)],
        compiler_params=_params(),
        name="ffn1_inproj",
    )(xp, xs, mods, norm_g, wgu, wo, win, *late_weights)


def _ffn2_kernel(x1_ref, attnp_ref, attns_ref, lrup_ref, lrus_ref, gates_ref, mod_ref, ng_ref, fg_ref,
                 wba_ref, wbl_ref, wout_ref, wgu_ref, wo_ref, yp_ref, ys_ref, a_ref, h_ref,
                 *, n_ctx_tiles, tiles_per_seq):
    is_ctx = pl.program_id(0) < n_ctx_tiles
    m = _mod_rows(mod_ref, pl.program_id(0), n_ctx_tiles, tiles_per_seq)
    attn = jnp.where(is_ctx, attnp_ref[...], attns_ref[...])
    lru = jnp.where(is_ctx, lrup_ref[...], lrus_ref[...])
    gt = gates_ref[...]
    mm = (_sigmoid(gt[:, :D_MODEL]) * _dot(attn, wba_ref[...])
          + _sigmoid(gt[:, D_MODEL:]) * _dot(lru, wbl_ref[...]))
    x2 = x1_ref[...] + m[5] * _dot(mm.astype(BF16), wout_ref[...])
    h_ref[...] = _rms_mod(x2, ng_ref[2:3], m[6], m[7]).astype(BF16)
    x3 = x2 + (0.5 * m[8]) * _swiglu(h_ref, a_ref, wgu_ref, wo_ref)
    y = x3 * lax.rsqrt(jnp.mean(x3 * x3, axis=-1, keepdims=True) + EPS)
    y = y * fg_ref[...]
    ctx_tile = jnp.broadcast_to(is_ctx, y.shape)
    pltpu.store(yp_ref, y, mask=ctx_tile)
    pltpu.store(ys_ref, y, mask=jnp.logical_not(ctx_tile))


def _ffn2(x1, attnp, attns, lrup, lrus, gates, mods, norm_g, final_g, wba, wbl, wout, wgu, wo, tm, seq_lat):
    t = x1.shape[0]
    t_ctx = attnp.shape[0]
    n_ctx_tiles = t_ctx // tm
    row = lambda w: pl.BlockSpec((tm, w), lambda i: (i, 0))
    first, second = _two_stream(n_ctx_tiles, tm)
    return pl.pallas_call(
        functools.partial(_ffn2_kernel, n_ctx_tiles=n_ctx_tiles, tiles_per_seq=seq_lat // tm),
        grid=(t // tm,),
        in_specs=[row(D_MODEL), first(D_ATTN), second(D_ATTN), first(D_LRU), second(D_LRU), row(2 * D_MODEL),
                  _const_spec(mods.shape),
                  _const_spec(norm_g.shape), _const_spec(final_g.shape),
                  _const_spec(wba.shape), _const_spec(wbl.shape), _const_spec(wout.shape),
                  _const_spec(wgu.shape), _const_spec(wo.shape)],
        out_specs=[first(D_MODEL), second(D_MODEL)],
        out_shape=[jax.ShapeDtypeStruct((t_ctx, D_MODEL), F32), jax.ShapeDtypeStruct((t - t_ctx, D_MODEL), F32)],
        scratch_shapes=[pltpu.VMEM((tm, D_FF), BF16), pltpu.VMEM((tm, D_MODEL), BF16)],
        compiler_params=_params(),
        name="merge_ffn2",
    )(x1, attnp, attns, lrup, lrus, gates, mods, norm_g, final_g, wba, wbl, wout, wgu, wo)


def _softmax_pv(scores, values, value_dots=None):
    value_dots = value_dots or [_dot] * len(values)
    m = scores[0].max(axis=-1, keepdims=True)
    for s in scores[1:]:
        m = jnp.maximum(m, s.max(axis=-1, keepdims=True))
    l = None
    o = None
    for s, v, dot in zip(scores, values, value_dots):
        p = jnp.exp(s - m)
        ls = p.sum(axis=-1, keepdims=True)
        os_ = dot(p.astype(BF16), v)
        l = ls if l is None else l + ls
        o = os_ if o is None else o + os_
    return o / l


def _ctx_attn_kernel(qkv_ref, o_ref):
    seq = qkv_ref.shape[0]
    lane = lax.broadcasted_iota(jnp.int32, (seq, LANES), 1)
    first = lane < HEAD_DIM
    for hp in range(D_ATTN // LANES):
        q = qkv_ref[:, hp * LANES:(hp + 1) * LANES]
        k = qkv_ref[:, D_ATTN + hp * LANES:D_ATTN + (hp + 1) * LANES]
        v = qkv_ref[:, 2 * D_ATTN + hp * LANES:2 * D_ATTN + (hp + 1) * LANES]
        qm = jnp.concatenate([jnp.where(first, q, jnp.zeros_like(q)), jnp.where(first, jnp.zeros_like(q), q)], axis=0)
        o = _softmax_pv([_dot_nt(qm, k)], [v])
        o_ref[:, hp * LANES:(hp + 1) * LANES] = jnp.where(first, o[:seq], o[seq:]).astype(BF16)


def _ctx_attention(qkv, nb, seq):
    return pl.pallas_call(
        _ctx_attn_kernel,
        grid=(nb,),
        in_specs=[pl.BlockSpec((seq, 3 * D_ATTN), lambda b: (b, 0))],
        out_specs=pl.BlockSpec((seq, D_ATTN), lambda b: (b, 0)),
        out_shape=jax.ShapeDtypeStruct((nb * seq, D_ATTN), BF16),
        compiler_params=_params(),
        name="ctx_attention",
    )(qkv)


def _na_groups(rows):
    kh = min(WIN_H, rows)
    r = np.arange(rows)
    rs = np.clip(r - kh // 2, 0, rows - kh)
    groups = []
    for g in range(rows // ROWS_PER_GROUP):
        qr = r[g * ROWS_PER_GROUP:(g + 1) * ROWS_PER_GROUP]
        k0, k1 = int(rs[qr].min()), int(rs[qr].max()) + kh
        if (k1 - k0) % 2:
            if k1 < rows:
                k1 += 1
            else:
                k0 -= 1
        groups.append((k0, k1))
    return groups, rs, kh


def _na_rel_rows(rpb):
    edge = GRID_W - WIN_W
    v = jnp.concatenate([rpb[..., WIN_W - 1:],
                         jnp.repeat(rpb[..., -1:], edge, axis=-1),
                         jnp.zeros(rpb.shape[:-1] + (1,), F32),
                         jnp.repeat(rpb[..., :1], edge, axis=-1),
                         rpb[..., :WIN_W - 1]], axis=-1).astype(F32)
    return jnp.pad(v, ((0, 0), (0, 1), (0, 0)))


def _na_build_bias(rel_ref, bias_ref, groups, rs, kh):
    shape = (GRID_W, LANES)
    qc = lax.broadcasted_iota(jnp.int32, shape, 0)
    lane = lax.broadcasted_iota(jnp.int32, shape, 1)
    left = lane < GRID_W
    kc = jnp.where(left, lane, lane - GRID_W)
    cs = jnp.clip(qc - WIN_W // 2, 0, GRID_W - WIN_W)
    in_win = (kc >= cs) & (kc < cs + WIN_W)
    neg = jnp.full(shape, NEG_INF, F32)
    for hh in range(2):
        @functools.cache
        def pair_tile(dr, a_ok, b_ok, hh=hh):
            def rolled(d, half):
                row = jnp.broadcast_to(rel_ref[hh, d + WIN_H - 1:d + WIN_H, :], shape)
                return pltpu.roll(row, half * GRID_W, 1, stride=1, stride_axis=0)

            if a_ok and b_ok:
                return jnp.where(in_win, jnp.where(left, rolled(dr, 0), rolled(dr + 1, 1)), neg)
            if a_ok:
                return jnp.where(in_win & left, rolled(dr, 0), neg)
            if b_ok:
                return jnp.where(in_win & jnp.logical_not(left), rolled(dr + 1, 1), neg)
            return neg

        boff = 0
        for g, (k0, k1) in enumerate(groups):
            for i in range(ROWS_PER_GROUP):
                r = g * ROWS_PER_GROUP + i
                for jp in range((k1 - k0) // 2):
                    j = k0 + 2 * jp
                    a_ok = rs[r] <= j < rs[r] + kh
                    b_ok = rs[r] <= j + 1 < rs[r] + kh
                    t = pair_tile(j - r if (a_ok or b_ok) else 0, a_ok, b_ok)
                    bias_ref[hh, i * GRID_W:(i + 1) * GRID_W, boff + jp * LANES:boff + (jp + 1) * LANES] = t
            boff += (k1 - k0) * GRID_W


def _na_attn_kernel(q_ref, k_ref, v_ref, kc_ref, vc_ref, rel_ref, o_ref, bias_ref, *, groups, rs, kh):
    @pl.when(pl.program_id(1) == 0)
    def _():
        _na_build_bias(rel_ref, bias_ref, groups, rs, kh)

    gq = ROWS_PER_GROUP * GRID_W
    lane = lax.broadcasted_iota(jnp.int32, (gq, LANES), 1)
    first = lane < HEAD_DIM

    past = kc_ref.shape[-1]
    kc = kc_ref[0, 0].reshape(LANES, past).astype(BF16)
    vc = vc_ref[0, 0].reshape(LANES, past).astype(BF16)
    for g, (k0, k1) in enumerate(groups):
        q = q_ref[g * gq:(g + 1) * gq, :]
        k = k_ref[k0 * GRID_W:k1 * GRID_W, :]
        v = v_ref[k0 * GRID_W:k1 * GRID_W, :]
        boff = sum((b - a) * GRID_W for a, b in groups[:g])
        qm = jnp.concatenate([jnp.where(first, q, jnp.zeros_like(q)), jnp.where(first, jnp.zeros_like(q), q)], axis=0)
        bias = jnp.concatenate([bias_ref[hh, :, boff:boff + (k1 - k0) * GRID_W] for hh in range(2)], axis=0)
        o = _softmax_pv([_dot_nt(qm, k) + bias, _dot(qm, kc)], [v, vc], [_dot, _dot_nt])
        o_ref[g * gq:(g + 1) * gq, :] = jnp.where(first, o[:gq], o[gq:]).astype(BF16)


def _na_attention(qkv, kc, vc, rel, first_seq, nb, seq):
    t = nb * seq
    past = kc.shape[4]
    npair = D_ATTN // LANES
    groups, rs, kh = _na_groups(seq // GRID_W)
    width = sum(k1 - k0 for k0, k1 in groups) * GRID_W
    tok = lambda off: pl.BlockSpec((seq, LANES), lambda hp, b: (first_seq + b, off + hp))
    ctx = pl.BlockSpec((1, 1, 2, HEAD_DIM, past), lambda hp, b: (b, 0, hp, 0, 0))
    return pl.pallas_call(
        functools.partial(_na_attn_kernel, groups=groups, rs=[int(x) for x in rs], kh=kh),
        grid=(npair, nb),
        in_specs=[tok(0), tok(npair), tok(2 * npair), ctx, ctx,
                  pl.BlockSpec((2,) + rel.shape[1:], lambda hp, b: (hp, 0, 0))],
        out_specs=pl.BlockSpec((seq, LANES), lambda hp, b: (b, hp)),
        out_shape=jax.ShapeDtypeStruct((t, D_ATTN), BF16),
        scratch_shapes=[pltpu.VMEM((2, ROWS_PER_GROUP * GRID_W, width), F32)],
        compiler_params=_params(2),
        name="na_attention",
    )(qkv, qkv, qkv, kc, vc, rel)


def _gelu_tanh(x):
    c = float(np.sqrt(2.0 / np.pi))
    t = jnp.tanh(x * (c + (c * 0.044715) * (x * x)))
    return x * (0.5 * t + 0.5)


def _build_blockdiag(wa_ref, wi_ref, wbd_ref):
    per = LRU_TILE // LRU_BLOCK
    wbd_ref[...] = jnp.zeros(wbd_ref.shape, BF16)
    zero = jnp.zeros((LRU_BLOCK, LRU_BLOCK), F32)
    for d in range(2):
        for t in range(N_LRU_TILES):
            for p in range(per):
                rows = slice(p * LRU_BLOCK, (p + 1) * LRU_BLOCK)
                for half, w_ref in enumerate((wa_ref, wi_ref)):
                    blk = 0.5 * w_ref[d, t * per + p]
                    piece = jnp.concatenate([blk, zero] if p % 2 == 0 else [zero, blk], axis=1)
                    c0 = half * LRU_TILE + (p // 2) * LANES
                    wbd_ref[d, t, rows, c0:c0 + LANES] = piece.astype(BF16)


def _lru_kernel(xl_ref, gl_ref, h0_ref, cw_ref, cb_ref, wa_ref, wi_ref, ba_ref, bi_ref, coef_ref,
                y_ref, hfin_ref, xpad, xc, a_buf, hf_buf, hb_buf, wbd_ref, *, seq):
    pitch = seq // N_SEG + SEG_PAD
    lp = N_SEG * pitch
    halo = SUBLANES
    chunk = LANES
    n_chunks = seq // chunk
    left = CONV_W // 2

    @pl.when(pl.program_id(0) == 0)
    def _():
        _build_blockdiag(wa_ref, wi_ref, wbd_ref)

    for k in range(N_SLABS):
        cols = slice(k * LANES, (k + 1) * LANES)
        xpad[k, 0:halo, :] = jnp.zeros((halo, LANES), F32)
        xpad[k, halo + seq:2 * halo + seq, :] = jnp.zeros((halo, LANES), F32)
        xpad[k, halo:halo + seq, :] = xl_ref[:, cols]

    def conv_body(i, carry):
        r0 = pl.multiple_of(i * chunk, chunk)
        for k in range(N_SLABS):
            cols = slice(k * LANES, (k + 1) * LANES)
            out = cb_ref[:, cols]
            for j in range(CONV_W):
                out = out + xpad[k, pl.ds(r0 + halo + j - left, chunk), :] * cw_ref[j:j + 1, cols]
            xc[k, pl.ds(r0, chunk), :] = out
        return carry

    lax.fori_loop(0, n_chunks, conv_body, 0)

    a_buf[:, seq:lp, :] = jnp.ones((N_SLABS, lp - seq, LANES), F32)
    hf_buf[:, seq:lp, :] = jnp.zeros((N_SLABS, lp - seq, LANES), F32)
    hb_buf[:, seq:lp, :] = jnp.zeros((N_SLABS, lp - seq, LANES), F32)

    row = lax.broadcasted_iota(jnp.int32, (N_SEG, LANES), 0)
    for d, h_buf in enumerate((hf_buf, hb_buf)):
        def gates_body(i, carry, d=d, h_buf=h_buf):
            rows = pl.ds(pl.multiple_of(i * chunk, chunk), chunk)
            per = LRU_TILE // LANES
            for c in range(N_LRU_TILES):
                xs = [xc[c * per + half, rows, :] for half in range(per)]
                pre = _dot(jnp.concatenate(xs, axis=1).astype(BF16), wbd_ref[d, c])
                for half in range(per):
                    k = c * per + half
                    cols = slice(k * LANES, (k + 1) * LANES)
                    tr = jnp.tanh(pre[:, half * LANES:(half + 1) * LANES] + ba_ref[d:d + 1, cols])
                    ti = jnp.tanh(pre[:, LRU_TILE + half * LANES:LRU_TILE + (half + 1) * LANES]
                                  + bi_ref[d:d + 1, cols])
                    a = jnp.exp2(coef_ref[d:d + 1, cols] * tr + coef_ref[d:d + 1, cols])
                    y = (1.0 - a) * (1.0 + a)
                    mult = y * lax.rsqrt(jnp.maximum(y, F32_TINY))
                    a_buf[k, rows, :] = a
                    h_buf[k, rows, :] = mult * ((0.5 * ti + 0.5) * xs[half])
            return carry

        lax.fori_loop(0, n_chunks, gates_body, 0, unroll=min(8, n_chunks))

        base0 = jnp.int32(0 if d == 0 else pitch - SCAN_UNROLL)
        base_step = SCAN_UNROLL if d == 0 else -SCAN_UNROLL

        def seg_rows(base, u, d=d):
            return pl.ds(base + (u if d == 0 else SCAN_UNROLL - 1 - u), N_SEG, stride=pitch)

        def totals_body(_, carry, h_buf=h_buf):
            base, hs, ps = carry[0], list(carry[1]), list(carry[2])
            for u in range(SCAN_UNROLL):
                idx = seg_rows(base, u)
                for k in range(N_SLABS):
                    a = a_buf[k, idx, :]
                    hs[k] = a * hs[k] + h_buf[k, idx, :]
                    ps[k] = ps[k] * a
            return base + base_step, tuple(hs), tuple(ps)

        zeros = tuple(jnp.zeros((N_SEG, LANES), F32) for _ in range(N_SLABS))
        ones = tuple(jnp.ones((N_SEG, LANES), F32) for _ in range(N_SLABS))
        _, h_end, p_end = lax.fori_loop(0, pitch // SCAN_UNROLL, totals_body, (base0, zeros, ones))

        order = range(N_SEG) if d == 0 else range(N_SEG - 1, -1, -1)
        starts = []
        for k in range(N_SLABS):
            c = h0_ref[0, d:d + 1, k * LANES:(k + 1) * LANES]
            cm = jnp.zeros((N_SEG, LANES), F32)
            for s in order:
                cm = jnp.where(row == s, c, cm)
                c = p_end[k][s:s + 1, :] * c + h_end[k][s:s + 1, :]
            starts.append(cm)
            hfin_ref[0, d:d + 1, k * LANES:(k + 1) * LANES] = c

        def scan_body(_, carry, h_buf=h_buf):
            base, hs = carry[0], list(carry[1])
            for u in range(SCAN_UNROLL):
                idx = seg_rows(base, u)
                for k in range(N_SLABS):
                    hs[k] = a_buf[k, idx, :] * hs[k] + h_buf[k, idx, :]
                    h_buf[k, idx, :] = hs[k]
            return base + base_step, tuple(hs)

        lax.fori_loop(0, pitch // SCAN_UNROLL, scan_body, (base0, tuple(starts)))

    def out_body(i, carry):
        r0 = pl.multiple_of(i * chunk, chunk)
        for k in range(N_SLABS):
            cols = slice(k * LANES, (k + 1) * LANES)
            h = hf_buf[k, pl.ds(r0, chunk), :] + hb_buf[k, pl.ds(r0, chunk), :]
            y_ref[pl.ds(r0, chunk), cols] = (h * _gelu_tanh(gl_ref[pl.ds(r0, chunk), cols])).astype(BF16)
        return carry

    lax.fori_loop(0, n_chunks, out_body, 0)


def _lru(xl, gl, h0, conv_w, conv_b, wa, wi, ba, bi, coef, first_seq, nb, seq):
    t = nb * seq
    lp = N_SEG * (seq // N_SEG + SEG_PAD)
    tok_in = pl.BlockSpec((seq, D_LRU), lambda b: (first_seq + b, 0))
    tok = pl.BlockSpec((seq, D_LRU), lambda b: (b, 0))
    state = pl.BlockSpec((1, 2, D_LRU), lambda b: (b, 0, 0))
    scan_buf = pltpu.VMEM((N_SLABS, lp, LANES), F32)
    return pl.pallas_call(
        functools.partial(_lru_kernel, seq=seq),
        grid=(nb,),
        in_specs=[tok_in, tok_in, state, _const_spec(conv_w.shape), _const_spec(conv_b.shape),
                  _const_spec(wa.shape), _const_spec(wi.shape), _const_spec(ba.shape), _const_spec(bi.shape),
                  _const_spec(coef.shape)],
        out_specs=[tok, state],
        out_shape=[jax.ShapeDtypeStruct((t, D_LRU), BF16), jax.ShapeDtypeStruct((nb, 2, D_LRU), F32)],
        scratch_shapes=[pltpu.VMEM((N_SLABS, seq + 2 * SUBLANES, LANES), F32),
                        pltpu.VMEM((N_SLABS, seq, LANES), F32), scan_buf, scan_buf, scan_buf,
                        pltpu.VMEM((2, N_LRU_TILES, LRU_TILE, 2 * LRU_TILE), BF16)],
        compiler_params=_params(),
        name="rglru",
    )(xl, gl, h0, conv_w, conv_b, wa, wi, ba, bi, coef)


def kernel(x_prompt, x_sample, cache_k, cache_v, state_lru, c, c_ctx, w_mod, b_mod, norm_g, ffn1_w_in, ffn1_w_out,
           w_in, rpb, conv_w, conv_b, lru_wa, lru_ba, lru_wi, lru_bi, lru_lambda, w_br_attn, w_br_lru, w_out,
           ffn2_w_in, ffn2_w_out, final_g):
    assert w_mod.shape[0] == 1, "single trunk layer"
    nb_ctx, seq_ctx, _ = x_prompt.shape
    nb_lat, seq_lat, _ = x_sample.shape

    n_cond = 1 + nb_lat
    cond = jnp.concatenate([c_ctx[None, :], c, jnp.zeros((2 * SUBLANES - n_cond, D_MODEL), F32)], axis=0)
    mods = _mods(cond, w_mod[0], b_mod[0])

    wgu1, wo1, win = ffn1_w_in[0].astype(BF16), ffn1_w_out[0].astype(BF16), w_in[0].astype(BF16)
    late_weights = (w_br_attn[0], w_br_lru[0], w_out[0], ffn2_w_in[0], ffn2_w_out[0])
    coef = (0.5 * float(np.log2(np.e))) * (-LRU_C * jax.nn.softplus(-lru_lambda[0]))
    ba = 0.5 * lru_ba[0]
    bi = 0.5 * lru_bi[0]
    ng = norm_g[0]
    fg = final_g.reshape(1, D_MODEL)
    cb = conv_b[0].reshape(1, D_LRU)

    t_ctx = nb_ctx * seq_ctx
    assert t_ctx % seq_lat == 0, "latent sequences must start on a latent-sequence-sized row block"
    xp = x_prompt.reshape(t_ctx, D_MODEL)
    xs = x_sample.reshape(nb_lat * seq_lat, D_MODEL)

    x1, qkv, xl, gl, gates, new_k, new_v, wba, wbl, wout, wgu2, wo2 = _ffn1(
        xp, xs, mods, ng, wgu1, wo1, win, late_weights, nb_ctx, seq_ctx, seq_lat)

    attnp = _ctx_attention(qkv, nb_ctx, seq_ctx)
    lrup, h_fin = _lru(xl, gl, jnp.zeros((nb_ctx, 2, D_LRU), F32), conv_w[0], cb, lru_wa[0], lru_wi[0],
                       ba, bi, coef, 0, nb_ctx, seq_ctx)

    rel = _na_rel_rows(rpb[0])
    first_lat = t_ctx // seq_lat
    attns = _na_attention(qkv, cache_k.swapaxes(3, 4), cache_v.swapaxes(3, 4), rel, first_lat, nb_lat, seq_lat)
    lrus, _ = _lru(xl, gl, state_lru[:, 0], conv_w[0], cb, lru_wa[0], lru_wi[0], ba, bi, coef,
                   first_lat, nb_lat, seq_lat)

    yp, ys = _ffn2(x1, attnp, attns, lrup, lrus, gates, mods, ng, fg, wba, wbl, wout, wgu2, wo2, 512, seq_lat)
    return (yp.reshape(nb_ctx, seq_ctx, D_MODEL), ys.reshape(nb_lat, seq_lat, D_MODEL),
            new_k, new_v, h_fin[:, None])
```

```python
import functools

import numpy as np
import jax
import jax.numpy as jnp
from jax import lax
from jax.experimental import pallas as pl
from jax.experimental.pallas import tpu as pltpu

F32 = jnp.float32
BF16 = jnp.bfloat16

D_MODEL = 1024
N_HEADS = 8
HEAD_DIM = 64
D_ATTN = N_HEADS * HEAD_DIM
GRID_W = 64
WIN_H = 8
WIN_W = 16
D_LRU = 1024
LRU_BLOCKS = 16
LRU_BLOCK = D_LRU // LRU_BLOCKS
CONV_W = 4
LRU_C = 8.0
D_FF = 2816
N_MOD = 9
EPS = 1e-6
NEG_INF = -1e30
F32_TINY = float(np.finfo(np.float32).tiny)
ATTN_SCALE = HEAD_DIM ** -0.5

LANES = 128
SUBLANES = 8
MXU_DIM = 256
VMEM_LIMIT_BYTES = 56 * 1024 * 1024

FF_CHUNK = MXU_DIM
N_FF_CHUNKS = D_FF // FF_CHUNK
LRU_TILE = MXU_DIM
N_LRU_TILES = D_LRU // LRU_TILE
N_SLABS = D_LRU // LANES
N_SEG = SUBLANES
SEG_PAD = 4
SCAN_UNROLL = 12
ROWS_PER_GROUP = 4


def _sigmoid(x):
    return 0.5 * jnp.tanh(0.5 * x) + 0.5


def _rms_mod(x, g, shift, scale):
    y = x * lax.rsqrt(jnp.mean(x * x, axis=-1, keepdims=True) + EPS)
    y = y * g
    return y * (1 + scale) + shift


def _dot(a, b):
    return jnp.dot(a, b, preferred_element_type=F32)


def _dot_nt(a, b):
    return lax.dot_general(a, b, (((1,), (1,)), ((), ())), preferred_element_type=F32)


def _params(n_axes=1):
    return pltpu.CompilerParams(dimension_semantics=("arbitrary",) * n_axes,
                                vmem_limit_bytes=VMEM_LIMIT_BYTES)


def _const_spec(shape):
    nd = len(shape)
    return pl.BlockSpec(shape, lambda *_: (0,) * nd, pipeline_mode=pl.Buffered(1))


MODS_K_CHUNK = 128


def _mods_kernel(cond_ref, w_ref, b_ref, o_ref):
    @pl.when(pl.program_id(0) == 0)
    def _():
        o_ref[...] = jnp.broadcast_to(b_ref[...], o_ref.shape)

    c = cond_ref[...]
    s = (c * _sigmoid(c)).astype(BF16)
    o_ref[...] += _dot(s, w_ref[...].astype(BF16))


def _mods(cond, w_mod, b_mod):
    k, n = w_mod.shape
    return pl.pallas_call(
        _mods_kernel,
        grid=(k // MODS_K_CHUNK,),
        in_specs=[pl.BlockSpec((cond.shape[0], MODS_K_CHUNK), lambda j: (0, j)),
                  pl.BlockSpec((MODS_K_CHUNK, n), lambda j: (j, 0)),
                  pl.BlockSpec((1, n), lambda j: (0, 0))],
        out_specs=pl.BlockSpec((cond.shape[0], n), lambda j: (0, 0)),
        out_shape=jax.ShapeDtypeStruct((cond.shape[0], n), F32),
        compiler_params=_params(),
        name="mods",
    )(cond, w_mod, b_mod.reshape(1, n))


def _swiglu(h_ref, a_ref, wgu_ref, wo_ref):
    h = h_ref[...]
    for j in range(N_FF_CHUNKS):
        cols = slice(j * FF_CHUNK, (j + 1) * FF_CHUNK)
        g = _dot(h, wgu_ref[:, cols])
        u = _dot(h, wgu_ref[:, D_FF + j * FF_CHUNK:D_FF + (j + 1) * FF_CHUNK])
        a_ref[:, cols] = ((g * _sigmoid(g)) * u).astype(BF16)
    return _dot(a_ref[...], wo_ref[...])


BF16_ROWS = 2 * SUBLANES


def _cast_spec(w, n_steps):
    rows = BF16_ROWS * pl.cdiv(w.shape[0], BF16_ROWS * n_steps)
    n_chunks = w.shape[0] // rows
    assert n_chunks * rows == w.shape[0] and n_chunks <= n_steps
    return pl.BlockSpec((rows, w.shape[1]), lambda i: (jnp.minimum(i, n_chunks - 1), 0))


def _two_stream(n_first, tm_rows):
    first = lambda w: pl.BlockSpec((tm_rows, w), lambda i: (jnp.minimum(i, n_first - 1), 0))
    second = lambda w: pl.BlockSpec((tm_rows, w), lambda i: (jnp.maximum(i - n_first, 0), 0))
    return first, second


def _mod_rows(mods_ref, i, n_ctx_tiles, tiles_per_seq):
    row = jnp.where(i < n_ctx_tiles, 0, 1 + jnp.maximum(i - n_ctx_tiles, 0) // tiles_per_seq)
    m = mods_ref[pl.ds(row, 1), :]
    return [m[:, k * D_MODEL:(k + 1) * D_MODEL] for k in range(N_MOD)]


N_LATE_WEIGHTS = 5


def _ffn1_kernel(xp_ref, xs_ref, mod_ref, ng_ref, wgu_ref, wo_ref, win_ref, *rest, n_ctx_tiles, tiles_per_seq):
    late_f32 = rest[:N_LATE_WEIGHTS]
    x1_ref, qkv_ref, xl_ref, gl_ref, gates_ref, newk_ref, newv_ref = rest[N_LATE_WEIGHTS:N_LATE_WEIGHTS + 7]
    late_bf16 = rest[N_LATE_WEIGHTS + 7:2 * N_LATE_WEIGHTS + 7]
    a_ref, h_ref = rest[2 * N_LATE_WEIGHTS + 7:]

    for src, dst in zip(late_f32, late_bf16):
        dst[...] = src[...].astype(BF16)

    is_ctx = pl.program_id(0) < n_ctx_tiles
    x = jnp.where(is_ctx, xp_ref[...], xs_ref[...])
    m = _mod_rows(mod_ref, pl.program_id(0), n_ctx_tiles, tiles_per_seq)
    h_ref[...] = _rms_mod(x, ng_ref[0:1], m[0], m[1]).astype(BF16)
    x1 = x + (0.5 * m[2]) * _swiglu(h_ref, a_ref, wgu_ref, wo_ref)
    x1_ref[...] = x1
    h2 = _rms_mod(x1, ng_ref[1:2], m[3], m[4]).astype(BF16)
    c0, c1, c2, c3 = 3 * D_ATTN, 3 * D_ATTN + D_LRU, 3 * D_ATTN + 2 * D_LRU, 3 * D_ATTN + 2 * D_LRU + 2 * D_MODEL
    qkv = _dot(h2, win_ref[:, 0:c0])
    qkv_ref[:, :D_ATTN] = (qkv[:, :D_ATTN] * ATTN_SCALE).astype(BF16)
    qkv_ref[:, D_ATTN:] = qkv[:, D_ATTN:].astype(BF16)
    ctx_block = jnp.broadcast_to(is_ctx, newk_ref.shape)
    for ref, off in ((newk_ref, D_ATTN), (newv_ref, 2 * D_ATTN)):
        heads = [qkv[:, off + h * HEAD_DIM:off + (h + 1) * HEAD_DIM] for h in range(N_HEADS)]
        pltpu.store(ref, jnp.stack(heads, axis=0)[None, None], mask=ctx_block)
    xl_ref[...] = _dot(h2, win_ref[:, c0:c1])
    gl_ref[...] = _dot(h2, win_ref[:, c1:c2])
    gates_ref[...] = _dot(h2, win_ref[:, c2:c3])


def _ffn1(xp, xs, mods, norm_g, wgu, wo, win, late_weights, nb_ctx, seq_ctx, seq_lat):
    tm = seq_ctx
    n_ctx_tiles = xp.shape[0] // tm
    t = xp.shape[0] + xs.shape[0]
    n_steps = t // tm
    assert len(late_weights) == N_LATE_WEIGHTS
    row = lambda w: pl.BlockSpec((tm, w), lambda i: (i, 0))
    first, second = _two_stream(n_ctx_tiles, tm)
    cache_spec = pl.BlockSpec((1, 1, N_HEADS, seq_ctx, HEAD_DIM),
                              lambda i: (jnp.minimum(i, n_ctx_tiles - 1), 0, 0, 0, 0))
    cache_shape = jax.ShapeDtypeStruct((nb_ctx, 1, N_HEADS, seq_ctx, HEAD_DIM), F32)
    cast_specs = [_cast_spec(w, n_steps) for w in late_weights]
    return pl.pallas_call(
        functools.partial(_ffn1_kernel, n_ctx_tiles=n_ctx_tiles, tiles_per_seq=seq_lat // tm),
        grid=(n_steps,),
        in_specs=[first(D_MODEL), second(D_MODEL), _const_spec(mods.shape),
                  _const_spec(norm_g.shape), _const_spec(wgu.shape), _const_spec(wo.shape),
                  _const_spec(win.shape)] + cast_specs,
        out_specs=[row(D_MODEL), row(3 * D_ATTN), row(D_LRU), row(D_LRU), row(2 * D_MODEL),
                   cache_spec, cache_spec] + cast_specs,
        out_shape=[jax.ShapeDtypeStruct((t, D_MODEL), F32),
                   jax.ShapeDtypeStruct((t, 3 * D_ATTN), BF16),
                   jax.ShapeDtypeStruct((t, D_LRU), F32),
                   jax.ShapeDtypeStruct((t, D_LRU), F32),
                   jax.ShapeDtypeStruct((t, 2 * D_MODEL), F32),
                   cache_shape, cache_shape] + [jax.ShapeDtypeStruct(w.shape, BF16) for w in late_weights],
        scratch_shapes=[pltpu.VMEM((tm, D_FF), BF16), pltpu.VMEM((tm, D_MODEL), BF16)],
        compiler_params=_params(),
        name="ffn1_inproj",
    )(xp, xs, mods, norm_g, wgu, wo, win, *late_weights)


def _ffn2_kernel(x1_ref, attnp_ref, attns_ref, lrup_ref, lrus_ref, gates_ref, mod_ref, ng_ref, fg_ref,
                 wba_ref, wbl_ref, wout_ref, wgu_ref, wo_ref, yp_ref, ys_ref, a_ref, h_ref,
                 *, n_ctx_tiles, tiles_per_seq):
    is_ctx = pl.program_id(0) < n_ctx_tiles
    m = _mod_rows(mod_ref, pl.program_id(0), n_ctx_tiles, tiles_per_seq)
    attn = jnp.where(is_ctx, attnp_ref[...], attns_ref[...])
    lru = jnp.where(is_ctx, lrup_ref[...], lrus_ref[...])
    gt = gates_ref[...]
    mm = (_sigmoid(gt[:, :D_MODEL]) * _dot(attn, wba_ref[...])
          + _sigmoid(gt[:, D_MODEL:]) * _dot(lru, wbl_ref[...]))
    x2 = x1_ref[...] + m[5] * _dot(mm.astype(BF16), wout_ref[...])
    h_ref[...] = _rms_mod(x2, ng_ref[2:3], m[6], m[7]).astype(BF16)
    x3 = x2 + (0.5 * m[8]) * _swiglu(h_ref, a_ref, wgu_ref, wo_ref)
    y = x3 * lax.rsqrt(jnp.mean(x3 * x3, axis=-1, keepdims=True) + EPS)
    y = y * fg_ref[...]
    ctx_tile = jnp.broadcast_to(is_ctx, y.shape)
    pltpu.store(yp_ref, y, mask=ctx_tile)
    pltpu.store(ys_ref, y, mask=jnp.logical_not(ctx_tile))


def _ffn2(x1, attnp, attns, lrup, lrus, gates, mods, norm_g, final_g, wba, wbl, wout, wgu, wo, tm, seq_lat):
    t = x1.shape[0]
    t_ctx = attnp.shape[0]
    n_ctx_tiles = t_ctx // tm
    row = lambda w: pl.BlockSpec((tm, w), lambda i: (i, 0))
    first, second = _two_stream(n_ctx_tiles, tm)
    return pl.pallas_call(
        functools.partial(_ffn2_kernel, n_ctx_tiles=n_ctx_tiles, tiles_per_seq=seq_lat // tm),
        grid=(t // tm,),
        in_specs=[row(D_MODEL), first(D_ATTN), second(D_ATTN), first(D_LRU), second(D_LRU), row(2 * D_MODEL),
                  _const_spec(mods.shape),
                  _const_spec(norm_g.shape), _const_spec(final_g.shape),
                  _const_spec(wba.shape), _const_spec(wbl.shape), _const_spec(wout.shape),
                  _const_spec(wgu.shape), _const_spec(wo.shape)],
        out_specs=[first(D_MODEL), second(D_MODEL)],
        out_shape=[jax.ShapeDtypeStruct((t_ctx, D_MODEL), F32), jax.ShapeDtypeStruct((t - t_ctx, D_MODEL), F32)],
        scratch_shapes=[pltpu.VMEM((tm, D_FF), BF16), pltpu.VMEM((tm, D_MODEL), BF16)],
        compiler_params=_params(),
        name="merge_ffn2",
    )(x1, attnp, attns, lrup, lrus, gates, mods, norm_g, final_g, wba, wbl, wout, wgu, wo)


def _softmax_pv(scores, values, value_dots=None):
    value_dots = value_dots or [_dot] * len(values)
    m = scores[0].max(axis=-1, keepdims=True)
    for s in scores[1:]:
        m = jnp.maximum(m, s.max(axis=-1, keepdims=True))
    l = None
    o = None
    for s, v, dot in zip(scores, values, value_dots):
        p = jnp.exp(s - m)
        ls = p.sum(axis=-1, keepdims=True)
        os_ = dot(p.astype(BF16), v)
        l = ls if l is None else l + ls
        o = os_ if o is None else o + os_
    return o / l


def _ctx_attn_kernel(qkv_ref, o_ref):
    seq = qkv_ref.shape[0]
    lane = lax.broadcasted_iota(jnp.int32, (seq, LANES), 1)
    first = lane < HEAD_DIM
    for hp in range(D_ATTN // LANES):
        q = qkv_ref[:, hp * LANES:(hp + 1) * LANES]
        k = qkv_ref[:, D_ATTN + hp * LANES:D_ATTN + (hp + 1) * LANES]
        v = qkv_ref[:, 2 * D_ATTN + hp * LANES:2 * D_ATTN + (hp + 1) * LANES]
        qm = jnp.concatenate([jnp.where(first, q, jnp.zeros_like(q)), jnp.where(first, jnp.zeros_like(q), q)], axis=0)
        o = _softmax_pv([_dot_nt(qm, k)], [v])
        o_ref[:, hp * LANES:(hp + 1) * LANES] = jnp.where(first, o[:seq], o[seq:]).astype(BF16)


def _ctx_attention(qkv, nb, seq):
    return pl.pallas_call(
        _ctx_attn_kernel,
        grid=(nb,),
        in_specs=[pl.BlockSpec((seq, 3 * D_ATTN), lambda b: (b, 0))],
        out_specs=pl.BlockSpec((seq, D_ATTN), lambda b: (b, 0)),
        out_shape=jax.ShapeDtypeStruct((nb * seq, D_ATTN), BF16),
        compiler_params=_params(),
        name="ctx_attention",
    )(qkv)


def _na_groups(rows):
    kh = min(WIN_H, rows)
    r = np.arange(rows)
    rs = np.clip(r - kh // 2, 0, rows - kh)
    groups = []
    for g in range(rows // ROWS_PER_GROUP):
        qr = r[g * ROWS_PER_GROUP:(g + 1) * ROWS_PER_GROUP]
        k0, k1 = int(rs[qr].min()), int(rs[qr].max()) + kh
        if (k1 - k0) % 2:
            if k1 < rows:
                k1 += 1
            else:
                k0 -= 1
        groups.append((k0, k1))
    return groups, rs, kh


def _na_rel_rows(rpb):
    edge = GRID_W - WIN_W
    v = jnp.concatenate([rpb[..., WIN_W - 1:],
                         jnp.repeat(rpb[..., -1:], edge, axis=-1),
                         jnp.zeros(rpb.shape[:-1] + (1,), F32),
                         jnp.repeat(rpb[..., :1], edge, axis=-1),
                         rpb[..., :WIN_W - 1]], axis=-1).astype(F32)
    return jnp.pad(v, ((0, 0), (0, 1), (0, 0)))


def _na_build_bias(rel_ref, bias_ref, groups, rs, kh):
    shape = (GRID_W, LANES)
    qc = lax.broadcasted_iota(jnp.int32, shape, 0)
    lane = lax.broadcasted_iota(jnp.int32, shape, 1)
    left = lane < GRID_W
    kc = jnp.where(left, lane, lane - GRID_W)
    cs = jnp.clip(qc - WIN_W // 2, 0, GRID_W - WIN_W)
    in_win = (kc >= cs) & (kc < cs + WIN_W)
    neg = jnp.full(shape, NEG_INF, F32)
    for hh in range(2):
        @functools.cache
        def pair_tile(dr, a_ok, b_ok, hh=hh):
            def rolled(d, half):
                row = jnp.broadcast_to(rel_ref[hh, d + WIN_H - 1:d + WIN_H, :], shape)
                return pltpu.roll(row, half * GRID_W, 1, stride=1, stride_axis=0)

            if a_ok and b_ok:
                return jnp.where(in_win, jnp.where(left, rolled(dr, 0), rolled(dr + 1, 1)), neg)
            if a_ok:
                return jnp.where(in_win & left, rolled(dr, 0), neg)
            if b_ok:
                return jnp.where(in_win & jnp.logical_not(left), rolled(dr + 1, 1), neg)
            return neg

        boff = 0
        for g, (k0, k1) in enumerate(groups):
            for i in range(ROWS_PER_GROUP):
                r = g * ROWS_PER_GROUP + i
                for jp in range((k1 - k0) // 2):
                    j = k0 + 2 * jp
                    a_ok = rs[r] <= j < rs[r] + kh
                    b_ok = rs[r] <= j + 1 < rs[r] + kh
                    t = pair_tile(j - r if (a_ok or b_ok) else 0, a_ok, b_ok)
                    bias_ref[hh, i * GRID_W:(i + 1) * GRID_W, boff + jp * LANES:boff + (jp + 1) * LANES] = t
            boff += (k1 - k0) * GRID_W


def _na_attn_kernel(q_ref, k_ref, v_ref, kc_ref, vc_ref, rel_ref, o_ref, bias_ref, *, groups, rs, kh):
    @pl.when(pl.program_id(1) == 0)
    def _():
        _na_build_bias(rel_ref, bias_ref, groups, rs, kh)

    gq = ROWS_PER_GROUP * GRID_W
    lane = lax.broadcasted_iota(jnp.int32, (gq, LANES), 1)
    first = lane < HEAD_DIM

    past = kc_ref.shape[-1]
    kc = kc_ref[0, 0].reshape(LANES, past).astype(BF16)
    vc = vc_ref[0, 0].reshape(LANES, past).astype(BF16)
    for g, (k0, k1) in enumerate(groups):
        q = q_ref[g * gq:(g + 1) * gq, :]
        k = k_ref[k0 * GRID_W:k1 * GRID_W, :]
        v = v_ref[k0 * GRID_W:k1 * GRID_W, :]
        boff = sum((b - a) * GRID_W for a, b in groups[:g])
        qm = jnp.concatenate([jnp.where(first, q, jnp.zeros_like(q)), jnp.where(first, jnp.zeros_like(q), q)], axis=0)
        bias = jnp.concatenate([bias_ref[hh, :, boff:boff + (k1 - k0) * GRID_W] for hh in range(2)], axis=0)
        o = _softmax_pv([_dot_nt(qm, k) + bias, _dot(qm, kc)], [v, vc], [_dot, _dot_nt])
        o_ref[g * gq:(g + 1) * gq, :] = jnp.where(first, o[:gq], o[gq:]).astype(BF16)


def _na_attention(qkv, kc, vc, rel, first_seq, nb, seq):
    t = nb * seq
    past = kc.shape[4]
    npair = D_ATTN // LANES
    groups, rs, kh = _na_groups(seq // GRID_W)
    width = sum(k1 - k0 for k0, k1 in groups) * GRID_W
    tok = lambda off: pl.BlockSpec((seq, LANES), lambda hp, b: (first_seq + b, off + hp))
    ctx = pl.BlockSpec((1, 1, 2, HEAD_DIM, past), lambda hp, b: (b, 0, hp, 0, 0))
    return pl.pallas_call(
        functools.partial(_na_attn_kernel, groups=groups, rs=[int(x) for x in rs], kh=kh),
        grid=(npair, nb),
        in_specs=[tok(0), tok(npair), tok(2 * npair), ctx, ctx,
                  pl.BlockSpec((2,) + rel.shape[1:], lambda hp, b: (hp, 0, 0))],
        out_specs=pl.BlockSpec((seq, LANES), lambda hp, b: (b, hp)),
        out_shape=jax.ShapeDtypeStruct((t, D_ATTN), BF16),
        scratch_shapes=[pltpu.VMEM((2, ROWS_PER_GROUP * GRID_W, width), F32)],
        compiler_params=_params(2),
        name="na_attention",
    )(qkv, qkv, qkv, kc, vc, rel)


def _gelu_tanh(x):
    c = float(np.sqrt(2.0 / np.pi))
    t = jnp.tanh(x * (c + (c * 0.044715) * (x * x)))
    return x * (0.5 * t + 0.5)


def _build_blockdiag(wa_ref, wi_ref, wbd_ref):
    per = LRU_TILE // LRU_BLOCK
    wbd_ref[...] = jnp.zeros(wbd_ref.shape, BF16)
    zero = jnp.zeros((LRU_BLOCK, LRU_BLOCK), F32)
    for d in range(2):
        for t in range(N_LRU_TILES):
            for p in range(per):
                rows = slice(p * LRU_BLOCK, (p + 1) * LRU_BLOCK)
                for half, w_ref in enumerate((wa_ref, wi_ref)):
                    blk = 0.5 * w_ref[d, t * per + p]
                    piece = jnp.concatenate([blk, zero] if p % 2 == 0 else [zero, blk], axis=1)
                    c0 = half * LRU_TILE + (p // 2) * LANES
                    wbd_ref[d, t, rows, c0:c0 + LANES] = piece.astype(BF16)


def _lru_kernel(xl_ref, gl_ref, h0_ref, cw_ref, cb_ref, wa_ref, wi_ref, ba_ref, bi_ref, coef_ref,
                y_ref, hfin_ref, xpad, xc, a_buf, hf_buf, hb_buf, wbd_ref, *, seq):
    pitch = seq // N_SEG + SEG_PAD
    lp = N_SEG * pitch
    halo = SUBLANES
    chunk = LANES
    n_chunks = seq // chunk
    left = CONV_W // 2

    @pl.when(pl.program_id(0) == 0)
    def _():
        _build_blockdiag(wa_ref, wi_ref, wbd_ref)

    for k in range(N_SLABS):
        cols = slice(k * LANES, (k + 1) * LANES)
        xpad[k, 0:halo, :] = jnp.zeros((halo, LANES), F32)
        xpad[k, halo + seq:2 * halo + seq, :] = jnp.zeros((halo, LANES), F32)
        xpad[k, halo:halo + seq, :] = xl_ref[:, cols]

    def conv_body(i, carry):
        r0 = pl.multiple_of(i * chunk, chunk)
        for k in range(N_SLABS):
            cols = slice(k * LANES, (k + 1) * LANES)
            out = cb_ref[:, cols]
            for j in range(CONV_W):
                out = out + xpad[k, pl.ds(r0 + halo + j - left, chunk), :] * cw_ref[j:j + 1, cols]
            xc[k, pl.ds(r0, chunk), :] = out
        return carry

    lax.fori_loop(0, n_chunks, conv_body, 0)

    a_buf[:, seq:lp, :] = jnp.ones((N_SLABS, lp - seq, LANES), F32)
    hf_buf[:, seq:lp, :] = jnp.zeros((N_SLABS, lp - seq, LANES), F32)
    hb_buf[:, seq:lp, :] = jnp.zeros((N_SLABS, lp - seq, LANES), F32)

    row = lax.broadcasted_iota(jnp.int32, (N_SEG, LANES), 0)
    for d, h_buf in enumerate((hf_buf, hb_buf)):
        def gates_body(i, carry, d=d, h_buf=h_buf):
            rows = pl.ds(pl.multiple_of(i * chunk, chunk), chunk)
            per = LRU_TILE // LANES
            for c in range(N_LRU_TILES):
                xs = [xc[c * per + half, rows, :] for half in range(per)]
                pre = _dot(jnp.concatenate(xs, axis=1).astype(BF16), wbd_ref[d, c])
                for half in range(per):
                    k = c * per + half
                    cols = slice(k * LANES, (k + 1) * LANES)
                    tr = jnp.tanh(pre[:, half * LANES:(half + 1) * LANES] + ba_ref[d:d + 1, cols])
                    ti = jnp.tanh(pre[:, LRU_TILE + half * LANES:LRU_TILE + (half + 1) * LANES]
                                  + bi_ref[d:d + 1, cols])
                    a = jnp.exp2(coef_ref[d:d + 1, cols] * tr + coef_ref[d:d + 1, cols])
                    y = (1.0 - a) * (1.0 + a)
                    mult = y * lax.rsqrt(jnp.maximum(y, F32_TINY))
                    a_buf[k, rows, :] = a
                    h_buf[k, rows, :] = mult * ((0.5 * ti + 0.5) * xs[half])
            return carry

        lax.fori_loop(0, n_chunks, gates_body, 0, unroll=min(8, n_chunks))

        base0 = jnp.int32(0 if d == 0 else pitch - SCAN_UNROLL)
        base_step = SCAN_UNROLL if d == 0 else -SCAN_UNROLL

        def seg_rows(base, u, d=d):
            return pl.ds(base + (u if d == 0 else SCAN_UNROLL - 1 - u), N_SEG, stride=pitch)

        def totals_body(_, carry, h_buf=h_buf):
            base, hs, ps = carry[0], list(carry[1]), list(carry[2])
            for u in range(SCAN_UNROLL):
                idx = seg_rows(base, u)
                for k in range(N_SLABS):
                    a = a_buf[k, idx, :]
                    hs[k] = a * hs[k] + h_buf[k, idx, :]
                    ps[k] = ps[k] * a
            return base + base_step, tuple(hs), tuple(ps)

        zeros = tuple(jnp.zeros((N_SEG, LANES), F32) for _ in range(N_SLABS))
        ones = tuple(jnp.ones((N_SEG, LANES), F32) for _ in range(N_SLABS))
        _, h_end, p_end = lax.fori_loop(0, pitch // SCAN_UNROLL, totals_body, (base0, zeros, ones))

        order = range(N_SEG) if d == 0 else range(N_SEG - 1, -1, -1)
        starts = []
        for k in range(N_SLABS):
            c = h0_ref[0, d:d + 1, k * LANES:(k + 1) * LANES]
            cm = jnp.zeros((N_SEG, LANES), F32)
            for s in order:
                cm = jnp.where(row == s, c, cm)
                c = p_end[k][s:s + 1, :] * c + h_end[k][s:s + 1, :]
            starts.append(cm)
            hfin_ref[0, d:d + 1, k * LANES:(k + 1) * LANES] = c

        def scan_body(_, carry, h_buf=h_buf):
            base, hs = carry[0], list(carry[1])
            for u in range(SCAN_UNROLL):
                idx = seg_rows(base, u)
                for k in range(N_SLABS):
                    hs[k] = a_buf[k, idx, :] * hs[k] + h_buf[k, idx, :]
                    h_buf[k, idx, :] = hs[k]
            return base + base_step, tuple(hs)

        lax.fori_loop(0, pitch // SCAN_UNROLL, scan_body, (base0, tuple(starts)))

    def out_body(i, carry):
        r0 = pl.multiple_of(i * chunk, chunk)
        for k in range(N_SLABS):
            cols = slice(k * LANES, (k + 1) * LANES)
            h = hf_buf[k, pl.ds(r0, chunk), :] + hb_buf[k, pl.ds(r0, chunk), :]
            y_ref[pl.ds(r0, chunk), cols] = (h * _gelu_tanh(gl_ref[pl.ds(r0, chunk), cols])).astype(BF16)
        return carry

    lax.fori_loop(0, n_chunks, out_body, 0)


def _lru(xl, gl, h0, conv_w, conv_b, wa, wi, ba, bi, coef, first_seq, nb, seq):
    t = nb * seq
    lp = N_SEG * (seq // N_SEG + SEG_PAD)
    tok_in = pl.BlockSpec((seq, D_LRU), lambda b: (first_seq + b, 0))
    tok = pl.BlockSpec((seq, D_LRU), lambda b: (b, 0))
    state = pl.BlockSpec((1, 2, D_LRU), lambda b: (b, 0, 0))
    scan_buf = pltpu.VMEM((N_SLABS, lp, LANES), F32)
    return pl.pallas_call(
        functools.partial(_lru_kernel, seq=seq),
        grid=(nb,),
        in_specs=[tok_in, tok_in, state, _const_spec(conv_w.shape), _const_spec(conv_b.shape),
                  _const_spec(wa.shape), _const_spec(wi.shape), _const_spec(ba.shape), _const_spec(bi.shape),
                  _const_spec(coef.shape)],
        out_specs=[tok, state],
        out_shape=[jax.ShapeDtypeStruct((t, D_LRU), BF16), jax.ShapeDtypeStruct((nb, 2, D_LRU), F32)],
        scratch_shapes=[pltpu.VMEM((N_SLABS, seq + 2 * SUBLANES, LANES), F32),
                        pltpu.VMEM((N_SLABS, seq, LANES), F32), scan_buf, scan_buf, scan_buf,
                        pltpu.VMEM((2, N_LRU_TILES, LRU_TILE, 2 * LRU_TILE), BF16)],
        compiler_params=_params(),
        name="rglru",
    )(xl, gl, h0, conv_w, conv_b, wa, wi, ba, bi, coef)


def kernel(x_prompt, x_sample, cache_k, cache_v, state_lru, c, c_ctx, w_mod, b_mod, norm_g, ffn1_w_in, ffn1_w_out,
           w_in, rpb, conv_w, conv_b, lru_wa, lru_ba, lru_wi, lru_bi, lru_lambda, w_br_attn, w_br_lru, w_out,
           ffn2_w_in, ffn2_w_out, final_g):
    assert w_mod.shape[0] == 1, "single trunk layer"
    nb_ctx, seq_ctx, _ = x_prompt.shape
    nb_lat, seq_lat, _ = x_sample.shape

    n_cond = 1 + nb_lat
    cond = jnp.concatenate([c_ctx[None, :], c, jnp.zeros((2 * SUBLANES - n_cond, D_MODEL), F32)], axis=0)
    mods = _mods(cond, w_mod[0], b_mod[0])

    wgu1, wo1, win = ffn1_w_in[0].astype(BF16), ffn1_w_out[0].astype(BF16), w_in[0].astype(BF16)
    late_weights = (w_br_attn[0], w_br_lru[0], w_out[0], ffn2_w_in[0], ffn2_w_out[0])
    coef = (0.5 * float(np.log2(np.e))) * (-LRU_C * jax.nn.softplus(-lru_lambda[0]))
    ba = 0.5 * lru_ba[0]
    bi = 0.5 * lru_bi[0]
    ng = norm_g[0]
    fg = final_g.reshape(1, D_MODEL)
    cb = conv_b[0].reshape(1, D_LRU)

    t_ctx = nb_ctx * seq_ctx
    assert t_ctx % seq_lat == 0, "latent sequences must start on a latent-sequence-sized row block"
    xp = x_prompt.reshape(t_ctx, D_MODEL)
    xs = x_sample.reshape(nb_lat * seq_lat, D_MODEL)

    x1, qkv, xl, gl, gates, new_k, new_v, wba, wbl, wout, wgu2, wo2 = _ffn1(
        xp, xs, mods, ng, wgu1, wo1, win, late_weights, nb_ctx, seq_ctx, seq_lat)

    attnp = _ctx_attention(qkv, nb_ctx, seq_ctx)
    lrup, h_fin = _lru(xl, gl, jnp.zeros((nb_ctx, 2, D_LRU), F32), conv_w[0], cb, lru_wa[0], lru_wi[0],
                       ba, bi, coef, 0, nb_ctx, seq_ctx)

    rel = _na_rel_rows(rpb[0])
    first_lat = t_ctx // seq_lat
    attns = _na_attention(qkv, cache_k.swapaxes(3, 4), cache_v.swapaxes(3, 4), rel, first_lat, nb_lat, seq_lat)
    lrus, _ = _lru(xl, gl, state_lru[:, 0], conv_w[0], cb, lru_wa[0], lru_wi[0], ba, bi, coef,
                   first_lat, nb_lat, seq_lat)

    yp, ys = _ffn2(x1, attnp, attns, lrup, lrus, gates, mods, ng, fg, wba, wbl, wout, wgu2, wo2, 512, seq_lat)
    return (yp.reshape(nb_ctx, seq_ctx, D_MODEL), ys.reshape(nb_lat, seq_lat, D_MODEL),
            new_k, new_v, h_fin[:, None])
```

```python
import functools

import numpy as np
import jax
import jax.numpy as jnp
from jax import lax
from jax.experimental import pallas as pl
from jax.experimental.pallas import tpu as pltpu

F32 = jnp.float32
BF16 = jnp.bfloat16

D_MODEL = 1024
N_HEADS = 8
HEAD_DIM = 64
D_ATTN = N_HEADS * HEAD_DIM
GRID_W = 64
WIN_H = 8
WIN_W = 16
D_LRU = 1024
LRU_BLOCKS = 16
LRU_BLOCK = D_LRU // LRU_BLOCKS
CONV_W = 4
LRU_C = 8.0
D_FF = 2816
N_MOD = 9
EPS = 1e-6
NEG_INF = -1e30
F32_TINY = float(np.finfo(np.float32).tiny)
ATTN_SCALE = HEAD_DIM ** -0.5

LANES = 128
SUBLANES = 8
MXU_DIM = 256
VMEM_LIMIT_BYTES = 56 * 1024 * 1024

FF_CHUNK = MXU_DIM
N_FF_CHUNKS = D_FF // FF_CHUNK
LRU_TILE = MXU_DIM
N_LRU_TILES = D_LRU // LRU_TILE
N_SLABS = D_LRU // LANES
N_SEG = SUBLANES
SEG_PAD = 4
SCAN_UNROLL = 12
ROWS_PER_GROUP = 4
FFN2_TILE = 2 * MXU_DIM
CTX_SEQS_PER_STEP = 4
NA_SEQS_PER_STEP = 4


def _sigmoid(x):
    return 0.5 * jnp.tanh(0.5 * x) + 0.5


def _rms_mod(x, g, shift, scale):
    y = x * lax.rsqrt(jnp.mean(x * x, axis=-1, keepdims=True) + EPS)
    y = y * g
    return y * (1 + scale) + shift


def _dot(a, b):
    return jnp.dot(a, b, preferred_element_type=F32)


def _dot_nt(a, b):
    return lax.dot_general(a, b, (((1,), (1,)), ((), ())), preferred_element_type=F32)


def _params(n_axes=1):
    return pltpu.CompilerParams(dimension_semantics=("arbitrary",) * n_axes,
                                vmem_limit_bytes=VMEM_LIMIT_BYTES)


def _const_spec(shape):
    nd = len(shape)
    return pl.BlockSpec(shape, lambda *_: (0,) * nd, pipeline_mode=pl.Buffered(1))


MODS_K_CHUNK = 128


def _mods_kernel(cond_ref, w_ref, b_ref, *rest):
    n_cast = (len(rest) - 1) // 2
    cast_f32, o_ref, cast_bf16 = rest[:n_cast], rest[n_cast], rest[n_cast + 1:]

    for src, dst in zip(cast_f32, cast_bf16):
        dst[...] = src[...].astype(BF16)

    @pl.when(pl.program_id(0) == 0)
    def _():
        o_ref[...] = jnp.broadcast_to(b_ref[...], o_ref.shape)

    c = cond_ref[...]
    s = (c * _sigmoid(c)).astype(BF16)
    o_ref[...] += _dot(s, w_ref[...].astype(BF16))


def _mods(cond, w_mod, b_mod, cast_weights):
    k, n = w_mod.shape
    n_steps = k // MODS_K_CHUNK
    cast_specs = [_cast_spec(w, n_steps) for w in cast_weights]
    return pl.pallas_call(
        _mods_kernel,
        grid=(n_steps,),
        in_specs=[pl.BlockSpec((cond.shape[0], MODS_K_CHUNK), lambda j: (0, j)),
                  pl.BlockSpec((MODS_K_CHUNK, n), lambda j: (j, 0)),
                  pl.BlockSpec((1, n), lambda j: (0, 0))] + cast_specs,
        out_specs=[pl.BlockSpec((cond.shape[0], n), lambda j: (0, 0))] + cast_specs,
        out_shape=[jax.ShapeDtypeStruct((cond.shape[0], n), F32)]
        + [jax.ShapeDtypeStruct(w.shape, BF16) for w in cast_weights],
        compiler_params=_params(),
        name="mods",
    )(cond, w_mod, b_mod.reshape(1, n), *cast_weights)


def _swiglu(h_ref, a_ref, wgu_ref, wo_ref):
    h = h_ref[...]
    for j in range(N_FF_CHUNKS):
        cols = slice(j * FF_CHUNK, (j + 1) * FF_CHUNK)
        g = _dot(h, wgu_ref[:, cols])
        u = _dot(h, wgu_ref[:, D_FF + j * FF_CHUNK:D_FF + (j + 1) * FF_CHUNK])
        a_ref[:, cols] = ((g * _sigmoid(g)) * u).astype(BF16)
    return _dot(a_ref[...], wo_ref[...])


BF16_ROWS = 2 * SUBLANES


def _cast_spec(w, n_steps):
    rows = BF16_ROWS * pl.cdiv(w.shape[0], BF16_ROWS * n_steps)
    n_chunks = w.shape[0] // rows
    assert n_chunks * rows == w.shape[0] and n_chunks <= n_steps
    return pl.BlockSpec((rows, w.shape[1]), lambda i: (jnp.minimum(i, n_chunks - 1), 0))


def _two_stream(n_first, tm_rows):
    first = lambda w: pl.BlockSpec((tm_rows, w), lambda i: (jnp.minimum(i, n_first - 1), 0))
    second = lambda w: pl.BlockSpec((tm_rows, w), lambda i: (jnp.maximum(i - n_first, 0), 0))
    return first, second


def _mod_rows(mods_ref, i, n_ctx_tiles, tiles_per_seq):
    row = jnp.where(i < n_ctx_tiles, 0, 1 + jnp.maximum(i - n_ctx_tiles, 0) // tiles_per_seq)
    m = mods_ref[pl.ds(row, 1), :]
    return [m[:, k * D_MODEL:(k + 1) * D_MODEL] for k in range(N_MOD)]


N_LATE_WEIGHTS = 5


def _ffn1_kernel(xp_ref, xs_ref, mod_ref, ng_ref, wgu_ref, wo_ref, win_ref, *rest, n_ctx_tiles, tiles_per_seq):
    late_f32 = rest[:N_LATE_WEIGHTS]
    x1_ref, qkv_ref, xl_ref, gl_ref, gates_ref, newk_ref, newv_ref = rest[N_LATE_WEIGHTS:N_LATE_WEIGHTS + 7]
    late_bf16 = rest[N_LATE_WEIGHTS + 7:2 * N_LATE_WEIGHTS + 7]
    a_ref, h_ref = rest[2 * N_LATE_WEIGHTS + 7:]

    for src, dst in zip(late_f32, late_bf16):
        dst[...] = src[...].astype(BF16)

    is_ctx = pl.program_id(0) < n_ctx_tiles
    x = jnp.where(is_ctx, xp_ref[...], xs_ref[...])
    m = _mod_rows(mod_ref, pl.program_id(0), n_ctx_tiles, tiles_per_seq)
    h_ref[...] = _rms_mod(x, ng_ref[0:1], m[0], m[1]).astype(BF16)
    x1 = x + (0.5 * m[2]) * _swiglu(h_ref, a_ref, wgu_ref, wo_ref)
    x1_ref[...] = x1
    h2 = _rms_mod(x1, ng_ref[1:2], m[3], m[4]).astype(BF16)
    c0, c1, c2, c3 = 3 * D_ATTN, 3 * D_ATTN + D_LRU, 3 * D_ATTN + 2 * D_LRU, 3 * D_ATTN + 2 * D_LRU + 2 * D_MODEL
    qkv = _dot(h2, win_ref[:, 0:c0])
    qkv_ref[:, :D_ATTN] = (qkv[:, :D_ATTN] * ATTN_SCALE).astype(BF16)
    qkv_ref[:, D_ATTN:] = qkv[:, D_ATTN:].astype(BF16)
    ctx_block = jnp.broadcast_to(is_ctx, newk_ref.shape)
    for ref, off in ((newk_ref, D_ATTN), (newv_ref, 2 * D_ATTN)):
        heads = [qkv[:, off + h * HEAD_DIM:off + (h + 1) * HEAD_DIM] for h in range(N_HEADS)]
        pltpu.store(ref, jnp.stack(heads, axis=0)[None, None], mask=ctx_block)
    xl_ref[...] = _dot(h2, win_ref[:, c0:c1])
    gl_ref[...] = _dot(h2, win_ref[:, c1:c2])
    gates_ref[...] = _dot(h2, win_ref[:, c2:c3])


def _ffn1(xp, xs, mods, norm_g, wgu, wo, win, late_weights, nb_ctx, seq_ctx, seq_lat):
    tm = seq_ctx
    n_ctx_tiles = xp.shape[0] // tm
    t = xp.shape[0] + xs.shape[0]
    n_steps = t // tm
    assert len(late_weights) == N_LATE_WEIGHTS
    row = lambda w: pl.BlockSpec((tm, w), lambda i: (i, 0))
    first, second = _two_stream(n_ctx_tiles, tm)
    cache_spec = pl.BlockSpec((1, 1, N_HEADS, seq_ctx, HEAD_DIM),
                              lambda i: (jnp.minimum(i, n_ctx_tiles - 1), 0, 0, 0, 0))
    cache_shape = jax.ShapeDtypeStruct((nb_ctx, 1, N_HEADS, seq_ctx, HEAD_DIM), F32)
    cast_specs = [_cast_spec(w, n_steps) for w in late_weights]
    return pl.pallas_call(
        functools.partial(_ffn1_kernel, n_ctx_tiles=n_ctx_tiles, tiles_per_seq=seq_lat // tm),
        grid=(n_steps,),
        in_specs=[first(D_MODEL), second(D_MODEL), _const_spec(mods.shape),
                  _const_spec(norm_g.shape), _const_spec(wgu.shape), _const_spec(wo.shape),
                  _const_spec(win.shape)] + cast_specs,
        out_specs=[row(D_MODEL), row(3 * D_ATTN), row(D_LRU), row(D_LRU), row(2 * D_MODEL),
                   cache_spec, cache_spec] + cast_specs,
        out_shape=[jax.ShapeDtypeStruct((t, D_MODEL), F32),
                   jax.ShapeDtypeStruct((t, 3 * D_ATTN), BF16),
                   jax.ShapeDtypeStruct((t, D_LRU), F32),
                   jax.ShapeDtypeStruct((t, D_LRU), F32),
                   jax.ShapeDtypeStruct((t, 2 * D_MODEL), F32),
                   cache_shape, cache_shape] + [jax.ShapeDtypeStruct(w.shape, BF16) for w in late_weights],
        scratch_shapes=[pltpu.VMEM((tm, D_FF), BF16), pltpu.VMEM((tm, D_MODEL), BF16)],
        compiler_params=_params(),
        name="ffn1_inproj",
    )(xp, xs, mods, norm_g, wgu, wo, win, *late_weights)


def _ffn2_kernel(x1_ref, attnp_ref, attns_ref, lrup_ref, lrus_ref, gates_ref, mod_ref, ng_ref, fg_ref,
                 wba_ref, wbl_ref, wout_ref, wgu_ref, wo_ref, yp_ref, ys_ref, a_ref, h_ref,
                 *, n_ctx_tiles, tiles_per_seq):
    is_ctx = pl.program_id(0) < n_ctx_tiles
    m = _mod_rows(mod_ref, pl.program_id(0), n_ctx_tiles, tiles_per_seq)
    attn = jnp.where(is_ctx, attnp_ref[...], attns_ref[...])
    lru = jnp.where(is_ctx, lrup_ref[...], lrus_ref[...])
    gt = gates_ref[...]
    mm = (_sigmoid(gt[:, :D_MODEL]) * _dot(attn, wba_ref[...])
          + _sigmoid(gt[:, D_MODEL:]) * _dot(lru, wbl_ref[...]))
    x2 = x1_ref[...] + m[5] * _dot(mm.astype(BF16), wout_ref[...])
    h_ref[...] = _rms_mod(x2, ng_ref[2:3], m[6], m[7]).astype(BF16)
    x3 = x2 + (0.5 * m[8]) * _swiglu(h_ref, a_ref, wgu_ref, wo_ref)
    y = x3 * lax.rsqrt(jnp.mean(x3 * x3, axis=-1, keepdims=True) + EPS)
    y = y * fg_ref[...]
    ctx_tile = jnp.broadcast_to(is_ctx, y.shape)
    pltpu.store(yp_ref, y, mask=ctx_tile)
    pltpu.store(ys_ref, y, mask=jnp.logical_not(ctx_tile))


def _ffn2(x1, attnp, attns, lrup, lrus, gates, mods, norm_g, final_g, wba, wbl, wout, wgu, wo, tm, seq_lat):
    t = x1.shape[0]
    t_ctx = attnp.shape[0]
    n_ctx_tiles = t_ctx // tm
    row = lambda w: pl.BlockSpec((tm, w), lambda i: (i, 0))
    first, second = _two_stream(n_ctx_tiles, tm)
    return pl.pallas_call(
        functools.partial(_ffn2_kernel, n_ctx_tiles=n_ctx_tiles, tiles_per_seq=seq_lat // tm),
        grid=(t // tm,),
        in_specs=[row(D_MODEL), first(D_ATTN), second(D_ATTN), first(D_LRU), second(D_LRU), row(2 * D_MODEL),
                  _const_spec(mods.shape),
                  _const_spec(norm_g.shape), _const_spec(final_g.shape),
                  _const_spec(wba.shape), _const_spec(wbl.shape), _const_spec(wout.shape),
                  _const_spec(wgu.shape), _const_spec(wo.shape)],
        out_specs=[first(D_MODEL), second(D_MODEL)],
        out_shape=[jax.ShapeDtypeStruct((t_ctx, D_MODEL), F32), jax.ShapeDtypeStruct((t - t_ctx, D_MODEL), F32)],
        scratch_shapes=[pltpu.VMEM((tm, D_FF), BF16), pltpu.VMEM((tm, D_MODEL), BF16)],
        compiler_params=_params(),
        name="merge_ffn2",
    )(x1, attnp, attns, lrup, lrus, gates, mods, norm_g, final_g, wba, wbl, wout, wgu, wo)


def _softmax_pv(scores, values, value_dots=None):
    value_dots = value_dots or [_dot] * len(values)
    m = scores[0].max(axis=-1, keepdims=True)
    for s in scores[1:]:
        m = jnp.maximum(m, s.max(axis=-1, keepdims=True))
    l = None
    o = None
    for s, v, dot in zip(scores, values, value_dots):
        p = jnp.exp(s - m)
        ls = p.sum(axis=-1, keepdims=True)
        os_ = dot(p.astype(BF16), v)
        l = ls if l is None else l + ls
        o = os_ if o is None else o + os_
    return o / l


def _ctx_attn_kernel(qkv_ref, o_ref):
    seq = qkv_ref.shape[0] // CTX_SEQS_PER_STEP
    lane = lax.broadcasted_iota(jnp.int32, (seq, LANES), 1)
    first = lane < HEAD_DIM
    for sq in range(CTX_SEQS_PER_STEP):
        rows = slice(sq * seq, (sq + 1) * seq)
        for hp in range(D_ATTN // LANES):
            q = qkv_ref[rows, hp * LANES:(hp + 1) * LANES]
            k = qkv_ref[rows, D_ATTN + hp * LANES:D_ATTN + (hp + 1) * LANES]
            v = qkv_ref[rows, 2 * D_ATTN + hp * LANES:2 * D_ATTN + (hp + 1) * LANES]
            qm = jnp.concatenate([jnp.where(first, q, jnp.zeros_like(q)), jnp.where(first, jnp.zeros_like(q), q)],
                                 axis=0)
            o = _softmax_pv([_dot_nt(qm, k)], [v])
            o_ref[rows, hp * LANES:(hp + 1) * LANES] = jnp.where(first, o[:seq], o[seq:]).astype(BF16)


def _ctx_attention(qkv, nb, seq):
    assert nb % CTX_SEQS_PER_STEP == 0
    blk = seq * CTX_SEQS_PER_STEP
    return pl.pallas_call(
        _ctx_attn_kernel,
        grid=(nb // CTX_SEQS_PER_STEP,),
        in_specs=[pl.BlockSpec((blk, 3 * D_ATTN), lambda b: (b, 0))],
        out_specs=pl.BlockSpec((blk, D_ATTN), lambda b: (b, 0)),
        out_shape=jax.ShapeDtypeStruct((nb * seq, D_ATTN), BF16),
        compiler_params=_params(),
        name="ctx_attention",
    )(qkv)


def _na_groups(rows):
    kh = min(WIN_H, rows)
    r = np.arange(rows)
    rs = np.clip(r - kh // 2, 0, rows - kh)
    groups = []
    for g in range(rows // ROWS_PER_GROUP):
        qr = r[g * ROWS_PER_GROUP:(g + 1) * ROWS_PER_GROUP]
        k0, k1 = int(rs[qr].min()), int(rs[qr].max()) + kh
        if (k1 - k0) % 2:
            if k1 < rows:
                k1 += 1
            else:
                k0 -= 1
        groups.append((k0, k1))
    return groups, rs, kh


def _na_rel_rows(rpb):
    edge = GRID_W - WIN_W
    v = jnp.concatenate([rpb[..., WIN_W - 1:],
                         jnp.repeat(rpb[..., -1:], edge, axis=-1),
                         jnp.zeros(rpb.shape[:-1] + (1,), F32),
                         jnp.repeat(rpb[..., :1], edge, axis=-1),
                         rpb[..., :WIN_W - 1]], axis=-1).astype(F32)
    return jnp.pad(v, ((0, 0), (0, 1), (0, 0)))


def _na_build_bias(rel_ref, bias_ref, groups, rs, kh):
    shape = (GRID_W, LANES)
    qc = lax.broadcasted_iota(jnp.int32, shape, 0)
    lane = lax.broadcasted_iota(jnp.int32, shape, 1)
    left = lane < GRID_W
    kc = jnp.where(left, lane, lane - GRID_W)
    cs = jnp.clip(qc - WIN_W // 2, 0, GRID_W - WIN_W)
    in_win = (kc >= cs) & (kc < cs + WIN_W)
    neg = jnp.full(shape, NEG_INF, F32)
    for hh in range(2):
        @functools.cache
        def pair_tile(dr, a_ok, b_ok, hh=hh):
            def rolled(d, half):
                row = jnp.broadcast_to(rel_ref[hh, d + WIN_H - 1:d + WIN_H, :], shape)
                return pltpu.roll(row, half * GRID_W, 1, stride=1, stride_axis=0)

            if a_ok and b_ok:
                return jnp.where(in_win, jnp.where(left, rolled(dr, 0), rolled(dr + 1, 1)), neg)
            if a_ok:
                return jnp.where(in_win & left, rolled(dr, 0), neg)
            if b_ok:
                return jnp.where(in_win & jnp.logical_not(left), rolled(dr + 1, 1), neg)
            return neg

        boff = 0
        for g, (k0, k1) in enumerate(groups):
            for i in range(ROWS_PER_GROUP):
                r = g * ROWS_PER_GROUP + i
                for jp in range((k1 - k0) // 2):
                    j = k0 + 2 * jp
                    a_ok = rs[r] <= j < rs[r] + kh
                    b_ok = rs[r] <= j + 1 < rs[r] + kh
                    t = pair_tile(j - r if (a_ok or b_ok) else 0, a_ok, b_ok)
                    bias_ref[hh, i * GRID_W:(i + 1) * GRID_W, boff + jp * LANES:boff + (jp + 1) * LANES] = t
            boff += (k1 - k0) * GRID_W


def _na_attn_kernel(q_ref, k_ref, v_ref, kc_ref, vc_ref, rel_ref, o_ref, bias_ref, *, groups, rs, kh):
    @pl.when(pl.program_id(1) == 0)
    def _():
        _na_build_bias(rel_ref, bias_ref, groups, rs, kh)

    gq = ROWS_PER_GROUP * GRID_W
    lane = lax.broadcasted_iota(jnp.int32, (gq, LANES), 1)
    first = lane < HEAD_DIM

    past = kc_ref.shape[-1]
    seq = q_ref.shape[0] // NA_SEQS_PER_STEP
    for sq in range(NA_SEQS_PER_STEP):
        kc = kc_ref[sq, 0].reshape(LANES, past).astype(BF16)
        vc = vc_ref[sq, 0].reshape(LANES, past).astype(BF16)
        r0 = sq * seq
        for g, (k0, k1) in enumerate(groups):
            q = q_ref[r0 + g * gq:r0 + (g + 1) * gq, :]
            k = k_ref[r0 + k0 * GRID_W:r0 + k1 * GRID_W, :]
            v = v_ref[r0 + k0 * GRID_W:r0 + k1 * GRID_W, :]
            boff = sum((b - a) * GRID_W for a, b in groups[:g])
            qm = jnp.concatenate([jnp.where(first, q, jnp.zeros_like(q)), jnp.where(first, jnp.zeros_like(q), q)],
                                 axis=0)
            bias = jnp.concatenate([bias_ref[hh, :, boff:boff + (k1 - k0) * GRID_W] for hh in range(2)], axis=0)
            o = _softmax_pv([_dot_nt(qm, k) + bias, _dot(qm, kc)], [v, vc], [_dot, _dot_nt])
            o_ref[r0 + g * gq:r0 + (g + 1) * gq, :] = jnp.where(first, o[:gq], o[gq:]).astype(BF16)


def _na_attention(qkv, kc, vc, rel, first_seq, nb, seq):
    t = nb * seq
    past = kc.shape[4]
    npair = D_ATTN // LANES
    groups, rs, kh = _na_groups(seq // GRID_W)
    width = sum(k1 - k0 for k0, k1 in groups) * GRID_W
    nsq = NA_SEQS_PER_STEP
    assert nb % nsq == 0 and first_seq % nsq == 0
    tok = lambda off: pl.BlockSpec((nsq * seq, LANES), lambda hp, b: (first_seq // nsq + b, off + hp))
    ctx = pl.BlockSpec((nsq, 1, 2, HEAD_DIM, past), lambda hp, b: (b, 0, hp, 0, 0))
    return pl.pallas_call(
        functools.partial(_na_attn_kernel, groups=groups, rs=[int(x) for x in rs], kh=kh),
        grid=(npair, nb // nsq),
        in_specs=[tok(0), tok(npair), tok(2 * npair), ctx, ctx,
                  pl.BlockSpec((2,) + rel.shape[1:], lambda hp, b: (hp, 0, 0))],
        out_specs=pl.BlockSpec((nsq * seq, LANES), lambda hp, b: (b, hp)),
        out_shape=jax.ShapeDtypeStruct((t, D_ATTN), BF16),
        scratch_shapes=[pltpu.VMEM((2, ROWS_PER_GROUP * GRID_W, width), F32)],
        compiler_params=_params(2),
        name="na_attention",
    )(qkv, qkv, qkv, kc, vc, rel)


def _gelu_tanh(x):
    c = float(np.sqrt(2.0 / np.pi))
    t = jnp.tanh(x * (c + (c * 0.044715) * (x * x)))
    return x * (0.5 * t + 0.5)


def _build_blockdiag(wa_ref, wi_ref, wbd_ref):
    per = LRU_TILE // LRU_BLOCK
    wbd_ref[...] = jnp.zeros(wbd_ref.shape, BF16)
    zero = jnp.zeros((LRU_BLOCK, LRU_BLOCK), F32)
    for d in range(2):
        for t in range(N_LRU_TILES):
            for p in range(per):
                rows = slice(p * LRU_BLOCK, (p + 1) * LRU_BLOCK)
                for half, w_ref in enumerate((wa_ref, wi_ref)):
                    blk = 0.5 * w_ref[d, t * per + p]
                    piece = jnp.concatenate([blk, zero] if p % 2 == 0 else [zero, blk], axis=1)
                    c0 = half * LRU_TILE + (p // 2) * LANES
                    wbd_ref[d, t, rows, c0:c0 + LANES] = piece.astype(BF16)


def _lru_kernel(xl_ref, gl_ref, h0_ref, cw_ref, cb_ref, wa_ref, wi_ref, ba_ref, bi_ref, coef_ref,
                y_ref, hfin_ref, xpad, xc, a_buf, hf_buf, hb_buf, wbd_ref, *, seq):
    pitch = seq // N_SEG + SEG_PAD
    lp = N_SEG * pitch
    halo = SUBLANES
    chunk = LANES
    n_chunks = seq // chunk
    left = CONV_W // 2

    @pl.when(pl.program_id(0) == 0)
    def _():
        _build_blockdiag(wa_ref, wi_ref, wbd_ref)

    for k in range(N_SLABS):
        cols = slice(k * LANES, (k + 1) * LANES)
        xpad[k, 0:halo, :] = jnp.zeros((halo, LANES), F32)
        xpad[k, halo + seq:2 * halo + seq, :] = jnp.zeros((halo, LANES), F32)
        xpad[k, halo:halo + seq, :] = xl_ref[:, cols]

    def conv_body(i, carry):
        r0 = pl.multiple_of(i * chunk, chunk)
        for k in range(N_SLABS):
            cols = slice(k * LANES, (k + 1) * LANES)
            out = cb_ref[:, cols]
            for j in range(CONV_W):
                out = out + xpad[k, pl.ds(r0 + halo + j - left, chunk), :] * cw_ref[j:j + 1, cols]
            xc[k, pl.ds(r0, chunk), :] = out
        return carry

    lax.fori_loop(0, n_chunks, conv_body, 0)

    a_buf[:, seq:lp, :] = jnp.ones((N_SLABS, lp - seq, LANES), F32)
    hf_buf[:, seq:lp, :] = jnp.zeros((N_SLABS, lp - seq, LANES), F32)
    hb_buf[:, seq:lp, :] = jnp.zeros((N_SLABS, lp - seq, LANES), F32)

    row = lax.broadcasted_iota(jnp.int32, (N_SEG, LANES), 0)
    for d, h_buf in enumerate((hf_buf, hb_buf)):
        def gates_body(i, carry, d=d, h_buf=h_buf):
            rows = pl.ds(pl.multiple_of(i * chunk, chunk), chunk)
            per = LRU_TILE // LANES
            for c in range(N_LRU_TILES):
                xs = [xc[c * per + half, rows, :] for half in range(per)]
                pre = _dot(jnp.concatenate(xs, axis=1).astype(BF16), wbd_ref[d, c])
                for half in range(per):
                    k = c * per + half
                    cols = slice(k * LANES, (k + 1) * LANES)
                    tr = jnp.tanh(pre[:, half * LANES:(half + 1) * LANES] + ba_ref[d:d + 1, cols])
                    ti = jnp.tanh(pre[:, LRU_TILE + half * LANES:LRU_TILE + (half + 1) * LANES]
                                  + bi_ref[d:d + 1, cols])
                    a = jnp.exp2(coef_ref[d:d + 1, cols] * tr + coef_ref[d:d + 1, cols])
                    y = (1.0 - a) * (1.0 + a)
                    mult = y * lax.rsqrt(jnp.maximum(y, F32_TINY))
                    a_buf[k, rows, :] = a
                    h_buf[k, rows, :] = mult * ((0.5 * ti + 0.5) * xs[half])
            return carry

        lax.fori_loop(0, n_chunks, gates_body, 0, unroll=min(8, n_chunks))

        base0 = jnp.int32(0 if d == 0 else pitch - SCAN_UNROLL)
        base_step = SCAN_UNROLL if d == 0 else -SCAN_UNROLL

        def seg_rows(base, u, d=d):
            return pl.ds(base + (u if d == 0 else SCAN_UNROLL - 1 - u), N_SEG, stride=pitch)

        def totals_body(_, carry, h_buf=h_buf):
            base, hs, ps = carry[0], list(carry[1]), list(carry[2])
            for u in range(SCAN_UNROLL):
                idx = seg_rows(base, u)
                for k in range(N_SLABS):
                    a = a_buf[k, idx, :]
                    hs[k] = a * hs[k] + h_buf[k, idx, :]
                    ps[k] = ps[k] * a
            return base + base_step, tuple(hs), tuple(ps)

        zeros = tuple(jnp.zeros((N_SEG, LANES), F32) for _ in range(N_SLABS))
        ones = tuple(jnp.ones((N_SEG, LANES), F32) for _ in range(N_SLABS))
        _, h_end, p_end = lax.fori_loop(0, pitch // SCAN_UNROLL, totals_body, (base0, zeros, ones))

        order = range(N_SEG) if d == 0 else range(N_SEG - 1, -1, -1)
        starts = []
        for k in range(N_SLABS):
            c = h0_ref[0, d:d + 1, k * LANES:(k + 1) * LANES]
            cm = jnp.zeros((N_SEG, LANES), F32)
            for s in order:
                cm = jnp.where(row == s, c, cm)
                c = p_end[k][s:s + 1, :] * c + h_end[k][s:s + 1, :]
            starts.append(cm)
            hfin_ref[0, d:d + 1, k * LANES:(k + 1) * LANES] = c

        def scan_body(_, carry, h_buf=h_buf):
            base, hs = carry[0], list(carry[1])
            for u in range(SCAN_UNROLL):
                idx = seg_rows(base, u)
                for k in range(N_SLABS):
                    hs[k] = a_buf[k, idx, :] * hs[k] + h_buf[k, idx, :]
                    h_buf[k, idx, :] = hs[k]
            return base + base_step, tuple(hs)

        lax.fori_loop(0, pitch // SCAN_UNROLL, scan_body, (base0, tuple(starts)))

    def out_body(i, carry):
        r0 = pl.multiple_of(i * chunk, chunk)
        for k in range(N_SLABS):
            cols = slice(k * LANES, (k + 1) * LANES)
            h = hf_buf[k, pl.ds(r0, chunk), :] + hb_buf[k, pl.ds(r0, chunk), :]
            y_ref[pl.ds(r0, chunk), cols] = (h * _gelu_tanh(gl_ref[pl.ds(r0, chunk), cols])).astype(BF16)
        return carry

    lax.fori_loop(0, n_chunks, out_body, 0)


def _lru(xl, gl, h0, conv_w, conv_b, wa, wi, ba, bi, coef, first_seq, nb, seq):
    t = nb * seq
    lp = N_SEG * (seq // N_SEG + SEG_PAD)
    tok_in = pl.BlockSpec((seq, D_LRU), lambda b: (first_seq + b, 0))
    tok = pl.BlockSpec((seq, D_LRU), lambda b: (b, 0))
    state = pl.BlockSpec((1, 2, D_LRU), lambda b: (b, 0, 0))
    scan_buf = pltpu.VMEM((N_SLABS, lp, LANES), F32)
    return pl.pallas_call(
        functools.partial(_lru_kernel, seq=seq),
        grid=(nb,),
        in_specs=[tok_in, tok_in, state, _const_spec(conv_w.shape), _const_spec(conv_b.shape),
                  _const_spec(wa.shape), _const_spec(wi.shape), _const_spec(ba.shape), _const_spec(bi.shape),
                  _const_spec(coef.shape)],
        out_specs=[tok, state],
        out_shape=[jax.ShapeDtypeStruct((t, D_LRU), BF16), jax.ShapeDtypeStruct((nb, 2, D_LRU), F32)],
        scratch_shapes=[pltpu.VMEM((N_SLABS, seq + 2 * SUBLANES, LANES), F32),
                        pltpu.VMEM((N_SLABS, seq, LANES), F32), scan_buf, scan_buf, scan_buf,
                        pltpu.VMEM((2, N_LRU_TILES, LRU_TILE, 2 * LRU_TILE), BF16)],
        compiler_params=_params(),
        name="rglru",
    )(xl, gl, h0, conv_w, conv_b, wa, wi, ba, bi, coef)


def kernel(x_prompt, x_sample, cache_k, cache_v, state_lru, c, c_ctx, w_mod, b_mod, norm_g, ffn1_w_in, ffn1_w_out,
           w_in, rpb, conv_w, conv_b, lru_wa, lru_ba, lru_wi, lru_bi, lru_lambda, w_br_attn, w_br_lru, w_out,
           ffn2_w_in, ffn2_w_out, final_g):
    assert w_mod.shape[0] == 1, "single trunk layer"
    nb_ctx, seq_ctx, _ = x_prompt.shape
    nb_lat, seq_lat, _ = x_sample.shape

    n_cond = 1 + nb_lat
    cond = jnp.concatenate([c_ctx[None, :], c, jnp.zeros((2 * SUBLANES - n_cond, D_MODEL), F32)], axis=0)
    mods, wgu1, wo1, win = _mods(cond, w_mod[0], b_mod[0], (ffn1_w_in[0], ffn1_w_out[0], w_in[0]))

    late_weights = (w_br_attn[0], w_br_lru[0], w_out[0], ffn2_w_in[0], ffn2_w_out[0])
    coef = (0.5 * float(np.log2(np.e))) * (-LRU_C * jax.nn.softplus(-lru_lambda[0]))
    ba = 0.5 * lru_ba[0]
    bi = 0.5 * lru_bi[0]
    ng = norm_g[0]
    fg = final_g.reshape(1, D_MODEL)
    cb = conv_b[0].reshape(1, D_LRU)

    t_ctx = nb_ctx * seq_ctx
    assert t_ctx % seq_lat == 0, "latent sequences must start on a latent-sequence-sized row block"
    xp = x_prompt.reshape(t_ctx, D_MODEL)
    xs = x_sample.reshape(nb_lat * seq_lat, D_MODEL)

    x1, qkv, xl, gl, gates, new_k, new_v, wba, wbl, wout, wgu2, wo2 = _ffn1(
        xp, xs, mods, ng, wgu1, wo1, win, late_weights, nb_ctx, seq_ctx, seq_lat)

    attnp = _ctx_attention(qkv, nb_ctx, seq_ctx)
    lrup, h_fin = _lru(xl, gl, jnp.zeros((nb_ctx, 2, D_LRU), F32), conv_w[0], cb, lru_wa[0], lru_wi[0],
                       ba, bi, coef, 0, nb_ctx, seq_ctx)

    rel = _na_rel_rows(rpb[0])
    first_lat = t_ctx // seq_lat
    attns = _na_attention(qkv, cache_k.swapaxes(3, 4), cache_v.swapaxes(3, 4), rel, first_lat, nb_lat, seq_lat)
    lrus, _ = _lru(xl, gl, state_lru[:, 0], conv_w[0], cb, lru_wa[0], lru_wi[0], ba, bi, coef,
                   first_lat, nb_lat, seq_lat)

    yp, ys = _ffn2(x1, attnp, attns, lrup, lrus, gates, mods, ng, fg, wba, wbl, wout, wgu2, wo2,
                   FFN2_TILE, seq_lat)
    return (yp.reshape(nb_ctx, seq_ctx, D_MODEL), ys.reshape(nb_lat, seq_lat, D_MODEL),
            new_k, new_v, h_fin[:, None])
```

```python
import functools

import numpy as np
import jax
import jax.numpy as jnp
from jax import lax
from jax.experimental import pallas as pl
from jax.experimental.pallas import tpu as pltpu

F32 = jnp.float32
BF16 = jnp.bfloat16

D_MODEL = 1024
N_HEADS = 8
HEAD_DIM = 64
D_ATTN = N_HEADS * HEAD_DIM
GRID_W = 64
WIN_H = 8
WIN_W = 16
D_LRU = 1024
LRU_BLOCKS = 16
LRU_BLOCK = D_LRU // LRU_BLOCKS
CONV_W = 4
LRU_C = 8.0
D_FF = 2816
N_MOD = 9
EPS = 1e-6
NEG_INF = -1e30
F32_TINY = float(np.finfo(np.float32).tiny)
ATTN_SCALE = HEAD_DIM ** -0.5

LANES = 128
SUBLANES = 8
MXU_DIM = 256
VMEM_LIMIT_BYTES = 56 * 1024 * 1024

FF_CHUNK = MXU_DIM
N_FF_CHUNKS = D_FF // FF_CHUNK
LRU_TILE = MXU_DIM
N_LRU_TILES = D_LRU // LRU_TILE
N_SLABS = D_LRU // LANES
N_SEG = SUBLANES
SEG_PAD = 4
SCAN_UNROLL = 12
ROWS_PER_GROUP = 4
FFN2_TILE = 2 * MXU_DIM
CTX_SEQS_PER_STEP = 4
NA_SEQS_PER_STEP = 4
LRU_ROWS_PER_STEP = 1024


def _sigmoid(x):
    return 0.5 * jnp.tanh(0.5 * x) + 0.5


def _rms_mod(x, g, shift, scale):
    y = x * lax.rsqrt(jnp.mean(x * x, axis=-1, keepdims=True) + EPS)
    y = y * g
    return y * (1 + scale) + shift


def _dot(a, b):
    return jnp.dot(a, b, preferred_element_type=F32)


def _dot_nt(a, b):
    return lax.dot_general(a, b, (((1,), (1,)), ((), ())), preferred_element_type=F32)


def _params(n_axes=1):
    return pltpu.CompilerParams(dimension_semantics=("arbitrary",) * n_axes,
                                vmem_limit_bytes=VMEM_LIMIT_BYTES)


def _const_spec(shape):
    nd = len(shape)
    return pl.BlockSpec(shape, lambda *_: (0,) * nd, pipeline_mode=pl.Buffered(1))


MODS_K_CHUNK = 128


def _mods_kernel(cond_ref, w_ref, b_ref, *rest):
    n_cast = (len(rest) - 1) // 2
    cast_f32, o_ref, cast_bf16 = rest[:n_cast], rest[n_cast], rest[n_cast + 1:]

    for src, dst in zip(cast_f32, cast_bf16):
        dst[...] = src[...].astype(BF16)

    @pl.when(pl.program_id(0) == 0)
    def _():
        o_ref[...] = jnp.broadcast_to(b_ref[...], o_ref.shape)

    c = cond_ref[...]
    s = (c * _sigmoid(c)).astype(BF16)
    o_ref[...] += _dot(s, w_ref[...].astype(BF16))


def _mods(cond, w_mod, b_mod, cast_weights):
    k, n = w_mod.shape
    n_steps = k // MODS_K_CHUNK
    cast_specs = [_cast_spec(w, n_steps) for w in cast_weights]
    return pl.pallas_call(
        _mods_kernel,
        grid=(n_steps,),
        in_specs=[pl.BlockSpec((cond.shape[0], MODS_K_CHUNK), lambda j: (0, j)),
                  pl.BlockSpec((MODS_K_CHUNK, n), lambda j: (j, 0)),
                  pl.BlockSpec((1, n), lambda j: (0, 0))] + cast_specs,
        out_specs=[pl.BlockSpec((cond.shape[0], n), lambda j: (0, 0))] + cast_specs,
        out_shape=[jax.ShapeDtypeStruct((cond.shape[0], n), F32)]
        + [jax.ShapeDtypeStruct(w.shape, BF16) for w in cast_weights],
        compiler_params=_params(),
        name="mods",
    )(cond, w_mod, b_mod.reshape(1, n), *cast_weights)


def _swiglu(h_ref, a_ref, wgu_ref, wo_ref):
    h = h_ref[...]
    for j in range(N_FF_CHUNKS):
        cols = slice(j * FF_CHUNK, (j + 1) * FF_CHUNK)
        g = _dot(h, wgu_ref[:, cols])
        u = _dot(h, wgu_ref[:, D_FF + j * FF_CHUNK:D_FF + (j + 1) * FF_CHUNK])
        a_ref[:, cols] = ((g * _sigmoid(g)) * u).astype(BF16)
    return _dot(a_ref[...], wo_ref[...])


BF16_ROWS = 2 * SUBLANES


def _cast_spec(w, n_steps):
    rows = BF16_ROWS * pl.cdiv(w.shape[0], BF16_ROWS * n_steps)
    n_chunks = w.shape[0] // rows
    assert n_chunks * rows == w.shape[0] and n_chunks <= n_steps
    return pl.BlockSpec((rows, w.shape[1]), lambda i: (jnp.minimum(i, n_chunks - 1), 0))


def _two_stream(n_first, tm_rows):
    first = lambda w: pl.BlockSpec((tm_rows, w), lambda i: (jnp.minimum(i, n_first - 1), 0))
    second = lambda w: pl.BlockSpec((tm_rows, w), lambda i: (jnp.maximum(i - n_first, 0), 0))
    return first, second


def _mod_rows(mods_ref, i, n_ctx_tiles, tiles_per_seq):
    row = jnp.where(i < n_ctx_tiles, 0, 1 + jnp.maximum(i - n_ctx_tiles, 0) // tiles_per_seq)
    m = mods_ref[pl.ds(row, 1), :]
    return [m[:, k * D_MODEL:(k + 1) * D_MODEL] for k in range(N_MOD)]


N_LATE_WEIGHTS = 5


def _ffn1_kernel(xp_ref, xs_ref, mod_ref, ng_ref, wgu_ref, wo_ref, win_ref, *rest, n_ctx_tiles, tiles_per_seq):
    late_f32 = rest[:N_LATE_WEIGHTS]
    x1_ref, qkv_ref, xl_ref, gl_ref, gates_ref, newk_ref, newv_ref = rest[N_LATE_WEIGHTS:N_LATE_WEIGHTS + 7]
    late_bf16 = rest[N_LATE_WEIGHTS + 7:2 * N_LATE_WEIGHTS + 7]
    a_ref, h_ref = rest[2 * N_LATE_WEIGHTS + 7:]

    for src, dst in zip(late_f32, late_bf16):
        dst[...] = src[...].astype(BF16)

    is_ctx = pl.program_id(0) < n_ctx_tiles
    x = jnp.where(is_ctx, xp_ref[...], xs_ref[...])
    m = _mod_rows(mod_ref, pl.program_id(0), n_ctx_tiles, tiles_per_seq)
    h_ref[...] = _rms_mod(x, ng_ref[0:1], m[0], m[1]).astype(BF16)
    x1 = x + (0.5 * m[2]) * _swiglu(h_ref, a_ref, wgu_ref, wo_ref)
    x1_ref[...] = x1
    h2 = _rms_mod(x1, ng_ref[1:2], m[3], m[4]).astype(BF16)
    c0, c1, c2, c3 = 3 * D_ATTN, 3 * D_ATTN + D_LRU, 3 * D_ATTN + 2 * D_LRU, 3 * D_ATTN + 2 * D_LRU + 2 * D_MODEL
    qkv = _dot(h2, win_ref[:, 0:c0])
    qkv_ref[:, :D_ATTN] = (qkv[:, :D_ATTN] * ATTN_SCALE).astype(BF16)
    qkv_ref[:, D_ATTN:] = qkv[:, D_ATTN:].astype(BF16)
    ctx_block = jnp.broadcast_to(is_ctx, newk_ref.shape)
    for ref, off in ((newk_ref, D_ATTN), (newv_ref, 2 * D_ATTN)):
        heads = [qkv[:, off + h * HEAD_DIM:off + (h + 1) * HEAD_DIM] for h in range(N_HEADS)]
        pltpu.store(ref, jnp.stack(heads, axis=0)[None, None], mask=ctx_block)
    xl_ref[...] = _dot(h2, win_ref[:, c0:c1])
    gl_ref[...] = _dot(h2, win_ref[:, c1:c2])
    gates_ref[...] = _dot(h2, win_ref[:, c2:c3])


def _ffn1(xp, xs, mods, norm_g, wgu, wo, win, late_weights, nb_ctx, seq_ctx, seq_lat):
    tm = seq_ctx
    n_ctx_tiles = xp.shape[0] // tm
    t = xp.shape[0] + xs.shape[0]
    n_steps = t // tm
    assert len(late_weights) == N_LATE_WEIGHTS
    row = lambda w: pl.BlockSpec((tm, w), lambda i: (i, 0))
    first, second = _two_stream(n_ctx_tiles, tm)
    cache_spec = pl.BlockSpec((1, 1, N_HEADS, seq_ctx, HEAD_DIM),
                              lambda i: (jnp.minimum(i, n_ctx_tiles - 1), 0, 0, 0, 0))
    cache_shape = jax.ShapeDtypeStruct((nb_ctx, 1, N_HEADS, seq_ctx, HEAD_DIM), F32)
    cast_specs = [_cast_spec(w, n_steps) for w in late_weights]
    return pl.pallas_call(
        functools.partial(_ffn1_kernel, n_ctx_tiles=n_ctx_tiles, tiles_per_seq=seq_lat // tm),
        grid=(n_steps,),
        in_specs=[first(D_MODEL), second(D_MODEL), _const_spec(mods.shape),
                  _const_spec(norm_g.shape), _const_spec(wgu.shape), _const_spec(wo.shape),
                  _const_spec(win.shape)] + cast_specs,
        out_specs=[row(D_MODEL), row(3 * D_ATTN), row(D_LRU), row(D_LRU), row(2 * D_MODEL),
                   cache_spec, cache_spec] + cast_specs,
        out_shape=[jax.ShapeDtypeStruct((t, D_MODEL), F32),
                   jax.ShapeDtypeStruct((t, 3 * D_ATTN), BF16),
                   jax.ShapeDtypeStruct((t, D_LRU), F32),
                   jax.ShapeDtypeStruct((t, D_LRU), F32),
                   jax.ShapeDtypeStruct((t, 2 * D_MODEL), F32),
                   cache_shape, cache_shape] + [jax.ShapeDtypeStruct(w.shape, BF16) for w in late_weights],
        scratch_shapes=[pltpu.VMEM((tm, D_FF), BF16), pltpu.VMEM((tm, D_MODEL), BF16)],
        compiler_params=_params(),
        name="ffn1_inproj",
    )(xp, xs, mods, norm_g, wgu, wo, win, *late_weights)


def _ffn2_kernel(x1_ref, attnp_ref, attns_ref, lrup_ref, lrus_ref, gates_ref, mod_ref, ng_ref, fg_ref,
                 wba_ref, wbl_ref, wout_ref, wgu_ref, wo_ref, yp_ref, ys_ref, a_ref, h_ref,
                 *, n_ctx_tiles, tiles_per_seq):
    is_ctx = pl.program_id(0) < n_ctx_tiles
    m = _mod_rows(mod_ref, pl.program_id(0), n_ctx_tiles, tiles_per_seq)
    attn = jnp.where(is_ctx, attnp_ref[...], attns_ref[...])
    lru = jnp.where(is_ctx, lrup_ref[...], lrus_ref[...])
    gt = gates_ref[...]
    mm = (_sigmoid(gt[:, :D_MODEL]) * _dot(attn, wba_ref[...])
          + _sigmoid(gt[:, D_MODEL:]) * _dot(lru, wbl_ref[...]))
    x2 = x1_ref[...] + m[5] * _dot(mm.astype(BF16), wout_ref[...])
    h_ref[...] = _rms_mod(x2, ng_ref[2:3], m[6], m[7]).astype(BF16)
    x3 = x2 + (0.5 * m[8]) * _swiglu(h_ref, a_ref, wgu_ref, wo_ref)
    y = x3 * lax.rsqrt(jnp.mean(x3 * x3, axis=-1, keepdims=True) + EPS)
    y = y * fg_ref[...]
    ctx_tile = jnp.broadcast_to(is_ctx, y.shape)
    pltpu.store(yp_ref, y, mask=ctx_tile)
    pltpu.store(ys_ref, y, mask=jnp.logical_not(ctx_tile))


def _ffn2(x1, attnp, attns, lrup, lrus, gates, mods, norm_g, final_g, wba, wbl, wout, wgu, wo, tm, seq_lat):
    t = x1.shape[0]
    t_ctx = attnp.shape[0]
    n_ctx_tiles = t_ctx // tm
    row = lambda w: pl.BlockSpec((tm, w), lambda i: (i, 0))
    first, second = _two_stream(n_ctx_tiles, tm)
    return pl.pallas_call(
        functools.partial(_ffn2_kernel, n_ctx_tiles=n_ctx_tiles, tiles_per_seq=seq_lat // tm),
        grid=(t // tm,),
        in_specs=[row(D_MODEL), first(D_ATTN), second(D_ATTN), first(D_LRU), second(D_LRU), row(2 * D_MODEL),
                  _const_spec(mods.shape),
                  _const_spec(norm_g.shape), _const_spec(final_g.shape),
                  _const_spec(wba.shape), _const_spec(wbl.shape), _const_spec(wout.shape),
                  _const_spec(wgu.shape), _const_spec(wo.shape)],
        out_specs=[first(D_MODEL), second(D_MODEL)],
        out_shape=[jax.ShapeDtypeStruct((t_ctx, D_MODEL), F32), jax.ShapeDtypeStruct((t - t_ctx, D_MODEL), F32)],
        scratch_shapes=[pltpu.VMEM((tm, D_FF), BF16), pltpu.VMEM((tm, D_MODEL), BF16)],
        compiler_params=_params(),
        name="merge_ffn2",
    )(x1, attnp, attns, lrup, lrus, gates, mods, norm_g, final_g, wba, wbl, wout, wgu, wo)


def _softmax_pv(scores, values, value_dots=None):
    value_dots = value_dots or [_dot] * len(values)
    m = scores[0].max(axis=-1, keepdims=True)
    for s in scores[1:]:
        m = jnp.maximum(m, s.max(axis=-1, keepdims=True))
    l = None
    o = None
    for s, v, dot in zip(scores, values, value_dots):
        p = jnp.exp(s - m)
        ls = p.sum(axis=-1, keepdims=True)
        os_ = dot(p.astype(BF16), v)
        l = ls if l is None else l + ls
        o = os_ if o is None else o + os_
    return o / l


def _ctx_attn_kernel(qkv_ref, o_ref):
    seq = qkv_ref.shape[0] // CTX_SEQS_PER_STEP
    lane = lax.broadcasted_iota(jnp.int32, (seq, LANES), 1)
    first = lane < HEAD_DIM
    for sq in range(CTX_SEQS_PER_STEP):
        rows = slice(sq * seq, (sq + 1) * seq)
        for hp in range(D_ATTN // LANES):
            q = qkv_ref[rows, hp * LANES:(hp + 1) * LANES]
            k = qkv_ref[rows, D_ATTN + hp * LANES:D_ATTN + (hp + 1) * LANES]
            v = qkv_ref[rows, 2 * D_ATTN + hp * LANES:2 * D_ATTN + (hp + 1) * LANES]
            qm = jnp.concatenate([jnp.where(first, q, jnp.zeros_like(q)), jnp.where(first, jnp.zeros_like(q), q)],
                                 axis=0)
            o = _softmax_pv([_dot_nt(qm, k)], [v])
            o_ref[rows, hp * LANES:(hp + 1) * LANES] = jnp.where(first, o[:seq], o[seq:]).astype(BF16)


def _ctx_attention(qkv, nb, seq):
    assert nb % CTX_SEQS_PER_STEP == 0
    blk = seq * CTX_SEQS_PER_STEP
    return pl.pallas_call(
        _ctx_attn_kernel,
        grid=(nb // CTX_SEQS_PER_STEP,),
        in_specs=[pl.BlockSpec((blk, 3 * D_ATTN), lambda b: (b, 0))],
        out_specs=pl.BlockSpec((blk, D_ATTN), lambda b: (b, 0)),
        out_shape=jax.ShapeDtypeStruct((nb * seq, D_ATTN), BF16),
        compiler_params=_params(),
        name="ctx_attention",
    )(qkv)


def _na_groups(rows):
    kh = min(WIN_H, rows)
    r = np.arange(rows)
    rs = np.clip(r - kh // 2, 0, rows - kh)
    groups = []
    for g in range(rows // ROWS_PER_GROUP):
        qr = r[g * ROWS_PER_GROUP:(g + 1) * ROWS_PER_GROUP]
        k0, k1 = int(rs[qr].min()), int(rs[qr].max()) + kh
        if (k1 - k0) % 2:
            if k1 < rows:
                k1 += 1
            else:
                k0 -= 1
        groups.append((k0, k1))
    return groups, rs, kh


def _na_rel_rows(rpb):
    edge = GRID_W - WIN_W
    v = jnp.concatenate([rpb[..., WIN_W - 1:],
                         jnp.repeat(rpb[..., -1:], edge, axis=-1),
                         jnp.zeros(rpb.shape[:-1] + (1,), F32),
                         jnp.repeat(rpb[..., :1], edge, axis=-1),
                         rpb[..., :WIN_W - 1]], axis=-1).astype(F32)
    return jnp.pad(v, ((0, 0), (0, 1), (0, 0)))


def _na_build_bias(rel_ref, bias_ref, groups, rs, kh):
    shape = (GRID_W, LANES)
    qc = lax.broadcasted_iota(jnp.int32, shape, 0)
    lane = lax.broadcasted_iota(jnp.int32, shape, 1)
    left = lane < GRID_W
    kc = jnp.where(left, lane, lane - GRID_W)
    cs = jnp.clip(qc - WIN_W // 2, 0, GRID_W - WIN_W)
    in_win = (kc >= cs) & (kc < cs + WIN_W)
    neg = jnp.full(shape, NEG_INF, F32)
    for hh in range(2):
        @functools.cache
        def pair_tile(dr, a_ok, b_ok, hh=hh):
            def rolled(d, half):
                row = jnp.broadcast_to(rel_ref[hh, d + WIN_H - 1:d + WIN_H, :], shape)
                return pltpu.roll(row, half * GRID_W, 1, stride=1, stride_axis=0)

            if a_ok and b_ok:
                return jnp.where(in_win, jnp.where(left, rolled(dr, 0), rolled(dr + 1, 1)), neg)
            if a_ok:
                return jnp.where(in_win & left, rolled(dr, 0), neg)
            if b_ok:
                return jnp.where(in_win & jnp.logical_not(left), rolled(dr + 1, 1), neg)
            return neg

        boff = 0
        for g, (k0, k1) in enumerate(groups):
            for i in range(ROWS_PER_GROUP):
                r = g * ROWS_PER_GROUP + i
                for jp in range((k1 - k0) // 2):
                    j = k0 + 2 * jp
                    a_ok = rs[r] <= j < rs[r] + kh
                    b_ok = rs[r] <= j + 1 < rs[r] + kh
                    t = pair_tile(j - r if (a_ok or b_ok) else 0, a_ok, b_ok)
                    bias_ref[hh, i * GRID_W:(i + 1) * GRID_W, boff + jp * LANES:boff + (jp + 1) * LANES] = t
            boff += (k1 - k0) * GRID_W


def _na_attn_kernel(q_ref, k_ref, v_ref, kc_ref, vc_ref, rel_ref, o_ref, bias_ref, *, groups, rs, kh):
    @pl.when(pl.program_id(1) == 0)
    def _():
        _na_build_bias(rel_ref, bias_ref, groups, rs, kh)

    gq = ROWS_PER_GROUP * GRID_W
    lane = lax.broadcasted_iota(jnp.int32, (gq, LANES), 1)
    first = lane < HEAD_DIM

    past = kc_ref.shape[-1]
    seq = q_ref.shape[0] // NA_SEQS_PER_STEP
    for sq in range(NA_SEQS_PER_STEP):
        kc = kc_ref[sq, 0].reshape(LANES, past).astype(BF16)
        vc = vc_ref[sq, 0].reshape(LANES, past).astype(BF16)
        r0 = sq * seq
        for g, (k0, k1) in enumerate(groups):
            q = q_ref[r0 + g * gq:r0 + (g + 1) * gq, :]
            k = k_ref[r0 + k0 * GRID_W:r0 + k1 * GRID_W, :]
            v = v_ref[r0 + k0 * GRID_W:r0 + k1 * GRID_W, :]
            boff = sum((b - a) * GRID_W for a, b in groups[:g])
            qm = jnp.concatenate([jnp.where(first, q, jnp.zeros_like(q)), jnp.where(first, jnp.zeros_like(q), q)],
                                 axis=0)
            bias = jnp.concatenate([bias_ref[hh, :, boff:boff + (k1 - k0) * GRID_W] for hh in range(2)], axis=0)
            o = _softmax_pv([_dot_nt(qm, k) + bias, _dot(qm, kc)], [v, vc], [_dot, _dot_nt])
            o_ref[r0 + g * gq:r0 + (g + 1) * gq, :] = jnp.where(first, o[:gq], o[gq:]).astype(BF16)


def _na_attention(qkv, kc, vc, rel, first_seq, nb, seq):
    t = nb * seq
    past = kc.shape[4]
    npair = D_ATTN // LANES
    groups, rs, kh = _na_groups(seq // GRID_W)
    width = sum(k1 - k0 for k0, k1 in groups) * GRID_W
    nsq = NA_SEQS_PER_STEP
    assert nb % nsq == 0 and first_seq % nsq == 0
    tok = lambda off: pl.BlockSpec((nsq * seq, LANES), lambda hp, b: (first_seq // nsq + b, off + hp))
    ctx = pl.BlockSpec((nsq, 1, 2, HEAD_DIM, past), lambda hp, b: (b, 0, hp, 0, 0))
    return pl.pallas_call(
        functools.partial(_na_attn_kernel, groups=groups, rs=[int(x) for x in rs], kh=kh),
        grid=(npair, nb // nsq),
        in_specs=[tok(0), tok(npair), tok(2 * npair), ctx, ctx,
                  pl.BlockSpec((2,) + rel.shape[1:], lambda hp, b: (hp, 0, 0))],
        out_specs=pl.BlockSpec((nsq * seq, LANES), lambda hp, b: (b, hp)),
        out_shape=jax.ShapeDtypeStruct((t, D_ATTN), BF16),
        scratch_shapes=[pltpu.VMEM((2, ROWS_PER_GROUP * GRID_W, width), F32)],
        compiler_params=_params(2),
        name="na_attention",
    )(qkv, qkv, qkv, kc, vc, rel)


def _gelu_tanh(x):
    c = float(np.sqrt(2.0 / np.pi))
    t = jnp.tanh(x * (c + (c * 0.044715) * (x * x)))
    return x * (0.5 * t + 0.5)


def _build_blockdiag(wa_ref, wi_ref, wbd_ref):
    per = LRU_TILE // LRU_BLOCK
    wbd_ref[...] = jnp.zeros(wbd_ref.shape, BF16)
    zero = jnp.zeros((LRU_BLOCK, LRU_BLOCK), F32)
    for d in range(2):
        for t in range(N_LRU_TILES):
            for p in range(per):
                rows = slice(p * LRU_BLOCK, (p + 1) * LRU_BLOCK)
                for half, w_ref in enumerate((wa_ref, wi_ref)):
                    blk = 0.5 * w_ref[d, t * per + p]
                    piece = jnp.concatenate([blk, zero] if p % 2 == 0 else [zero, blk], axis=1)
                    c0 = half * LRU_TILE + (p // 2) * LANES
                    wbd_ref[d, t, rows, c0:c0 + LANES] = piece.astype(BF16)


def _lru_kernel(xl_ref, gl_ref, h0_ref, cw_ref, cb_ref, wa_ref, wi_ref, ba_ref, bi_ref, coef_ref,
                y_ref, hfin_ref, xpad, xc, a_buf, hf_buf, hb_buf, wbd_ref, *, seq):
    @pl.when(pl.program_id(0) == 0)
    def _():
        _build_blockdiag(wa_ref, wi_ref, wbd_ref)

    for sq in range(xl_ref.shape[0] // seq):
        rows = slice(sq * seq, (sq + 1) * seq)
        _lru_sequence(xl_ref.at[rows], gl_ref.at[rows], h0_ref.at[sq:sq + 1], cw_ref, cb_ref, ba_ref, bi_ref,
                      coef_ref, y_ref.at[rows], hfin_ref.at[sq:sq + 1], xpad, xc, a_buf, hf_buf, hb_buf, wbd_ref,
                      seq=seq)


def _lru_sequence(xl_ref, gl_ref, h0_ref, cw_ref, cb_ref, ba_ref, bi_ref, coef_ref,
                  y_ref, hfin_ref, xpad, xc, a_buf, hf_buf, hb_buf, wbd_ref, *, seq):
    pitch = seq // N_SEG + SEG_PAD
    lp = N_SEG * pitch
    halo = SUBLANES
    chunk = LANES
    n_chunks = seq // chunk
    left = CONV_W // 2

    for k in range(N_SLABS):
        cols = slice(k * LANES, (k + 1) * LANES)
        xpad[k, 0:halo, :] = jnp.zeros((halo, LANES), F32)
        xpad[k, halo + seq:2 * halo + seq, :] = jnp.zeros((halo, LANES), F32)
        xpad[k, halo:halo + seq, :] = xl_ref[:, cols]

    def conv_body(i, carry):
        r0 = pl.multiple_of(i * chunk, chunk)
        for k in range(N_SLABS):
            cols = slice(k * LANES, (k + 1) * LANES)
            out = cb_ref[:, cols]
            for j in range(CONV_W):
                out = out + xpad[k, pl.ds(r0 + halo + j - left, chunk), :] * cw_ref[j:j + 1, cols]
            xc[k, pl.ds(r0, chunk), :] = out
        return carry

    lax.fori_loop(0, n_chunks, conv_body, 0)

    a_buf[:, seq:lp, :] = jnp.ones((N_SLABS, lp - seq, LANES), F32)
    hf_buf[:, seq:lp, :] = jnp.zeros((N_SLABS, lp - seq, LANES), F32)
    hb_buf[:, seq:lp, :] = jnp.zeros((N_SLABS, lp - seq, LANES), F32)

    row = lax.broadcasted_iota(jnp.int32, (N_SEG, LANES), 0)
    for d, h_buf in enumerate((hf_buf, hb_buf)):
        def gates_body(i, carry, d=d, h_buf=h_buf):
            rows = pl.ds(pl.multiple_of(i * chunk, chunk), chunk)
            per = LRU_TILE // LANES
            for c in range(N_LRU_TILES):
                xs = [xc[c * per + half, rows, :] for half in range(per)]
                pre = _dot(jnp.concatenate(xs, axis=1).astype(BF16), wbd_ref[d, c])
                for half in range(per):
                    k = c * per + half
                    cols = slice(k * LANES, (k + 1) * LANES)
                    tr = jnp.tanh(pre[:, half * LANES:(half + 1) * LANES] + ba_ref[d:d + 1, cols])
                    ti = jnp.tanh(pre[:, LRU_TILE + half * LANES:LRU_TILE + (half + 1) * LANES]
                                  + bi_ref[d:d + 1, cols])
                    a = jnp.exp2(coef_ref[d:d + 1, cols] * tr + coef_ref[d:d + 1, cols])
                    y = (1.0 - a) * (1.0 + a)
                    mult = y * lax.rsqrt(jnp.maximum(y, F32_TINY))
                    a_buf[k, rows, :] = a
                    h_buf[k, rows, :] = mult * ((0.5 * ti + 0.5) * xs[half])
            return carry

        lax.fori_loop(0, n_chunks, gates_body, 0, unroll=min(8, n_chunks))

        base0 = jnp.int32(0 if d == 0 else pitch - SCAN_UNROLL)
        base_step = SCAN_UNROLL if d == 0 else -SCAN_UNROLL

        def seg_rows(base, u, d=d):
            return pl.ds(base + (u if d == 0 else SCAN_UNROLL - 1 - u), N_SEG, stride=pitch)

        def totals_body(_, carry, h_buf=h_buf):
            base, hs, ps = carry[0], list(carry[1]), list(carry[2])
            for u in range(SCAN_UNROLL):
                idx = seg_rows(base, u)
                for k in range(N_SLABS):
                    a = a_buf[k, idx, :]
                    hs[k] = a * hs[k] + h_buf[k, idx, :]
                    ps[k] = ps[k] * a
            return base + base_step, tuple(hs), tuple(ps)

        zeros = tuple(jnp.zeros((N_SEG, LANES), F32) for _ in range(N_SLABS))
        ones = tuple(jnp.ones((N_SEG, LANES), F32) for _ in range(N_SLABS))
        _, h_end, p_end = lax.fori_loop(0, pitch // SCAN_UNROLL, totals_body, (base0, zeros, ones))

        order = range(N_SEG) if d == 0 else range(N_SEG - 1, -1, -1)
        starts = []
        for k in range(N_SLABS):
            c = h0_ref[0, d:d + 1, k * LANES:(k + 1) * LANES]
            cm = jnp.zeros((N_SEG, LANES), F32)
            for s in order:
                cm = jnp.where(row == s, c, cm)
                c = p_end[k][s:s + 1, :] * c + h_end[k][s:s + 1, :]
            starts.append(cm)
            hfin_ref[0, d:d + 1, k * LANES:(k + 1) * LANES] = c

        def scan_body(_, carry, h_buf=h_buf):
            base, hs = carry[0], list(carry[1])
            for u in range(SCAN_UNROLL):
                idx = seg_rows(base, u)
                for k in range(N_SLABS):
                    hs[k] = a_buf[k, idx, :] * hs[k] + h_buf[k, idx, :]
                    h_buf[k, idx, :] = hs[k]
            return base + base_step, tuple(hs)

        lax.fori_loop(0, pitch // SCAN_UNROLL, scan_body, (base0, tuple(starts)))

    def out_body(i, carry):
        r0 = pl.multiple_of(i * chunk, chunk)
        for k in range(N_SLABS):
            cols = slice(k * LANES, (k + 1) * LANES)
            h = hf_buf[k, pl.ds(r0, chunk), :] + hb_buf[k, pl.ds(r0, chunk), :]
            y_ref[pl.ds(r0, chunk), cols] = (h * _gelu_tanh(gl_ref[pl.ds(r0, chunk), cols])).astype(BF16)
        return carry

    lax.fori_loop(0, n_chunks, out_body, 0)


def _lru(xl, gl, h0, conv_w, conv_b, wa, wi, ba, bi, coef, first_seq, nb, seq):
    t = nb * seq
    lp = N_SEG * (seq // N_SEG + SEG_PAD)
    nsq = max(1, LRU_ROWS_PER_STEP // seq)
    assert nb % nsq == 0 and first_seq % nsq == 0
    tok_in = pl.BlockSpec((nsq * seq, D_LRU), lambda b: (first_seq // nsq + b, 0))
    tok = pl.BlockSpec((nsq * seq, D_LRU), lambda b: (b, 0))
    state = pl.BlockSpec((nsq, 2, D_LRU), lambda b: (b, 0, 0))
    scan_buf = pltpu.VMEM((N_SLABS, lp, LANES), F32)
    return pl.pallas_call(
        functools.partial(_lru_kernel, seq=seq),
        grid=(nb // nsq,),
        in_specs=[tok_in, tok_in, state, _const_spec(conv_w.shape), _const_spec(conv_b.shape),
                  _const_spec(wa.shape), _const_spec(wi.shape), _const_spec(ba.shape), _const_spec(bi.shape),
                  _const_spec(coef.shape)],
        out_specs=[tok, state],
        out_shape=[jax.ShapeDtypeStruct((t, D_LRU), BF16), jax.ShapeDtypeStruct((nb, 2, D_LRU), F32)],
        scratch_shapes=[pltpu.VMEM((N_SLABS, seq + 2 * SUBLANES, LANES), F32),
                        pltpu.VMEM((N_SLABS, seq, LANES), F32), scan_buf, scan_buf, scan_buf,
                        pltpu.VMEM((2, N_LRU_TILES, LRU_TILE, 2 * LRU_TILE), BF16)],
        compiler_params=_params(),
        name="rglru",
    )(xl, gl, h0, conv_w, conv_b, wa, wi, ba, bi, coef)


def kernel(x_prompt, x_sample, cache_k, cache_v, state_lru, c, c_ctx, w_mod, b_mod, norm_g, ffn1_w_in, ffn1_w_out,
           w_in, rpb, conv_w, conv_b, lru_wa, lru_ba, lru_wi, lru_bi, lru_lambda, w_br_attn, w_br_lru, w_out,
           ffn2_w_in, ffn2_w_out, final_g):
    assert w_mod.shape[0] == 1, "single trunk layer"
    nb_ctx, seq_ctx, _ = x_prompt.shape
    nb_lat, seq_lat, _ = x_sample.shape

    n_cond = 1 + nb_lat
    cond = jnp.concatenate([c_ctx[None, :], c, jnp.zeros((2 * SUBLANES - n_cond, D_MODEL), F32)], axis=0)
    mods, wgu1, wo1, win = _mods(cond, w_mod[0], b_mod[0], (ffn1_w_in[0], ffn1_w_out[0], w_in[0]))

    late_weights = (w_br_attn[0], w_br_lru[0], w_out[0], ffn2_w_in[0], ffn2_w_out[0])
    coef = (0.5 * float(np.log2(np.e))) * (-LRU_C * jax.nn.softplus(-lru_lambda[0]))
    ba = 0.5 * lru_ba[0]
    bi = 0.5 * lru_bi[0]
    ng = norm_g[0]
    fg = final_g.reshape(1, D_MODEL)
    cb = conv_b[0].reshape(1, D_LRU)

    t_ctx = nb_ctx * seq_ctx
    assert t_ctx % seq_lat == 0, "latent sequences must start on a latent-sequence-sized row block"
    xp = x_prompt.reshape(t_ctx, D_MODEL)
    xs = x_sample.reshape(nb_lat * seq_lat, D_MODEL)

    x1, qkv, xl, gl, gates, new_k, new_v, wba, wbl, wout, wgu2, wo2 = _ffn1(
        xp, xs, mods, ng, wgu1, wo1, win, late_weights, nb_ctx, seq_ctx, seq_lat)

    attnp = _ctx_attention(qkv, nb_ctx, seq_ctx)
    lrup, h_fin = _lru(xl, gl, jnp.zeros((nb_ctx, 2, D_LRU), F32), conv_w[0], cb, lru_wa[0], lru_wi[0],
                       ba, bi, coef, 0, nb_ctx, seq_ctx)

    rel = _na_rel_rows(rpb[0])
    first_lat = t_ctx // seq_lat
    attns = _na_attention(qkv, cache_k.swapaxes(3, 4), cache_v.swapaxes(3, 4), rel, first_lat, nb_lat, seq_lat)
    lrus, _ = _lru(xl, gl, state_lru[:, 0], conv_w[0], cb, lru_wa[0], lru_wi[0], ba, bi, coef,
                   first_lat, nb_lat, seq_lat)

    yp, ys = _ffn2(x1, attnp, attns, lrup, lrus, gates, mods, ng, fg, wba, wbl, wout, wgu2, wo2,
                   FFN2_TILE, seq_lat)
    return (yp.reshape(nb_ctx, seq_ctx, D_MODEL), ys.reshape(nb_lat, seq_lat, D_MODEL),
            new_k, new_v, h_fin[:, None])
```

```python
import functools

import numpy as np
import jax
import jax.numpy as jnp
from jax import lax
from jax.experimental import pallas as pl
from jax.experimental.pallas import tpu as pltpu

F32 = jnp.float32
BF16 = jnp.bfloat16

D_MODEL = 1024
N_HEADS = 8
HEAD_DIM = 64
D_ATTN = N_HEADS * HEAD_DIM
GRID_W = 64
WIN_H = 8
WIN_W = 16
D_LRU = 1024
LRU_BLOCKS = 16
LRU_BLOCK = D_LRU // LRU_BLOCKS
CONV_W = 4
LRU_C = 8.0
D_FF = 2816
N_MOD = 9
EPS = 1e-6
NEG_INF = -1e30
F32_TINY = float(np.finfo(np.float32).tiny)
ATTN_SCALE = HEAD_DIM ** -0.5

LANES = 128
SUBLANES = 8
MXU_DIM = 256
VMEM_LIMIT_BYTES = 56 * 1024 * 1024

FF_CHUNK = MXU_DIM
N_FF_CHUNKS = D_FF // FF_CHUNK
LRU_TILE = MXU_DIM
N_LRU_TILES = D_LRU // LRU_TILE
N_SLABS = D_LRU // LANES
N_SEG = SUBLANES
SEG_PAD = 4
SCAN_UNROLL = 12
ROWS_PER_GROUP = 4
FFN2_TILE = 2 * MXU_DIM
CTX_SEQS_PER_STEP = 4
NA_SEQS_PER_STEP = 2


def _sigmoid(x):
    return 0.5 * jnp.tanh(0.5 * x) + 0.5


def _rms_mod(x, g, shift, scale):
    y = x * lax.rsqrt(jnp.mean(x * x, axis=-1, keepdims=True) + EPS)
    y = y * g
    return y * (1 + scale) + shift


def _dot(a, b):
    return jnp.dot(a, b, preferred_element_type=F32)


def _dot_nt(a, b):
    return lax.dot_general(a, b, (((1,), (1,)), ((), ())), preferred_element_type=F32)


def _params(n_axes=1):
    return pltpu.CompilerParams(dimension_semantics=("arbitrary",) * n_axes,
                                vmem_limit_bytes=VMEM_LIMIT_BYTES)


def _const_spec(shape):
    nd = len(shape)
    return pl.BlockSpec(shape, lambda *_: (0,) * nd, pipeline_mode=pl.Buffered(1))


MODS_K_CHUNK = 128


def _mods_kernel(cond_ref, w_ref, b_ref, *rest):
    n_cast = (len(rest) - 1) // 2
    cast_f32, o_ref, cast_bf16 = rest[:n_cast], rest[n_cast], rest[n_cast + 1:]

    for src, dst in zip(cast_f32, cast_bf16):
        dst[...] = src[...].astype(BF16)

    @pl.when(pl.program_id(0) == 0)
    def _():
        o_ref[...] = jnp.broadcast_to(b_ref[...], o_ref.shape)

    c = cond_ref[...]
    s = (c * _sigmoid(c)).astype(BF16)
    o_ref[...] += _dot(s, w_ref[...].astype(BF16))


def _mods(cond, w_mod, b_mod, cast_weights):
    k, n = w_mod.shape
    n_steps = k // MODS_K_CHUNK
    cast_specs = [_cast_spec(w, n_steps) for w in cast_weights]
    return pl.pallas_call(
        _mods_kernel,
        grid=(n_steps,),
        in_specs=[pl.BlockSpec((cond.shape[0], MODS_K_CHUNK), lambda j: (0, j)),
                  pl.BlockSpec((MODS_K_CHUNK, n), lambda j: (j, 0)),
                  pl.BlockSpec((1, n), lambda j: (0, 0))] + cast_specs,
        out_specs=[pl.BlockSpec((cond.shape[0], n), lambda j: (0, 0))] + cast_specs,
        out_shape=[jax.ShapeDtypeStruct((cond.shape[0], n), F32)]
        + [jax.ShapeDtypeStruct(w.shape, BF16) for w in cast_weights],
        compiler_params=_params(),
        name="mods",
    )(cond, w_mod, b_mod.reshape(1, n), *cast_weights)


def _swiglu(h_ref, a_ref, wgu_ref, wo_ref):
    h = h_ref[...]
    for j in range(N_FF_CHUNKS):
        cols = slice(j * FF_CHUNK, (j + 1) * FF_CHUNK)
        g = _dot(h, wgu_ref[:, cols])
        u = _dot(h, wgu_ref[:, D_FF + j * FF_CHUNK:D_FF + (j + 1) * FF_CHUNK])
        a_ref[:, cols] = ((g * _sigmoid(g)) * u).astype(BF16)
    return _dot(a_ref[...], wo_ref[...])


BF16_ROWS = 2 * SUBLANES


def _cast_spec(w, n_steps):
    rows = BF16_ROWS * pl.cdiv(w.shape[0], BF16_ROWS * n_steps)
    n_chunks = w.shape[0] // rows
    assert n_chunks * rows == w.shape[0] and n_chunks <= n_steps
    return pl.BlockSpec((rows, w.shape[1]), lambda i: (jnp.minimum(i, n_chunks - 1), 0))


def _two_stream(n_first, tm_rows):
    first = lambda w: pl.BlockSpec((tm_rows, w), lambda i: (jnp.minimum(i, n_first - 1), 0))
    second = lambda w: pl.BlockSpec((tm_rows, w), lambda i: (jnp.maximum(i - n_first, 0), 0))
    return first, second


def _mod_rows(mods_ref, i, n_ctx_tiles, tiles_per_seq):
    row = jnp.where(i < n_ctx_tiles, 0, 1 + jnp.maximum(i - n_ctx_tiles, 0) // tiles_per_seq)
    m = mods_ref[pl.ds(row, 1), :]
    return [m[:, k * D_MODEL:(k + 1) * D_MODEL] for k in range(N_MOD)]


N_LATE_WEIGHTS = 5


def _ffn1_kernel(xp_ref, xs_ref, mod_ref, ng_ref, wgu_ref, wo_ref, win_ref, *rest, n_ctx_tiles, tiles_per_seq):
    late_f32 = rest[:N_LATE_WEIGHTS]
    x1_ref, qkv_ref, xl_ref, gl_ref, gates_ref, newk_ref, newv_ref = rest[N_LATE_WEIGHTS:N_LATE_WEIGHTS + 7]
    late_bf16 = rest[N_LATE_WEIGHTS + 7:2 * N_LATE_WEIGHTS + 7]
    a_ref, h_ref = rest[2 * N_LATE_WEIGHTS + 7:]

    for src, dst in zip(late_f32, late_bf16):
        dst[...] = src[...].astype(BF16)

    is_ctx = pl.program_id(0) < n_ctx_tiles
    x = jnp.where(is_ctx, xp_ref[...], xs_ref[...])
    m = _mod_rows(mod_ref, pl.program_id(0), n_ctx_tiles, tiles_per_seq)
    h_ref[...] = _rms_mod(x, ng_ref[0:1], m[0], m[1]).astype(BF16)
    x1 = x + (0.5 * m[2]) * _swiglu(h_ref, a_ref, wgu_ref, wo_ref)
    x1_ref[...] = x1
    h2 = _rms_mod(x1, ng_ref[1:2], m[3], m[4]).astype(BF16)
    c0, c1, c2, c3 = 3 * D_ATTN, 3 * D_ATTN + D_LRU, 3 * D_ATTN + 2 * D_LRU, 3 * D_ATTN + 2 * D_LRU + 2 * D_MODEL
    qkv = _dot(h2, win_ref[:, 0:c0])
    qkv_ref[:, :D_ATTN] = (qkv[:, :D_ATTN] * ATTN_SCALE).astype(BF16)
    qkv_ref[:, D_ATTN:] = qkv[:, D_ATTN:].astype(BF16)
    ctx_block = jnp.broadcast_to(is_ctx, newk_ref.shape)
    for ref, off in ((newk_ref, D_ATTN), (newv_ref, 2 * D_ATTN)):
        heads = [qkv[:, off + h * HEAD_DIM:off + (h + 1) * HEAD_DIM] for h in range(N_HEADS)]
        pltpu.store(ref, jnp.stack(heads, axis=0)[None, None], mask=ctx_block)
    xl_ref[...] = _dot(h2, win_ref[:, c0:c1])
    gl_ref[...] = _dot(h2, win_ref[:, c1:c2])
    gates_ref[...] = _dot(h2, win_ref[:, c2:c3])


def _ffn1(xp, xs, mods, norm_g, wgu, wo, win, late_weights, nb_ctx, seq_ctx, seq_lat):
    tm = seq_ctx
    n_ctx_tiles = xp.shape[0] // tm
    t = xp.shape[0] + xs.shape[0]
    n_steps = t // tm
    assert len(late_weights) == N_LATE_WEIGHTS
    row = lambda w: pl.BlockSpec((tm, w), lambda i: (i, 0))
    first, second = _two_stream(n_ctx_tiles, tm)
    cache_spec = pl.BlockSpec((1, 1, N_HEADS, seq_ctx, HEAD_DIM),
                              lambda i: (jnp.minimum(i, n_ctx_tiles - 1), 0, 0, 0, 0))
    cache_shape = jax.ShapeDtypeStruct((nb_ctx, 1, N_HEADS, seq_ctx, HEAD_DIM), F32)
    cast_specs = [_cast_spec(w, n_steps) for w in late_weights]
    return pl.pallas_call(
        functools.partial(_ffn1_kernel, n_ctx_tiles=n_ctx_tiles, tiles_per_seq=seq_lat // tm),
        grid=(n_steps,),
        in_specs=[first(D_MODEL), second(D_MODEL), _const_spec(mods.shape),
                  _const_spec(norm_g.shape), _const_spec(wgu.shape), _const_spec(wo.shape),
                  _const_spec(win.shape)] + cast_specs,
        out_specs=[row(D_MODEL), row(3 * D_ATTN), row(D_LRU), row(D_LRU), row(2 * D_MODEL),
                   cache_spec, cache_spec] + cast_specs,
        out_shape=[jax.ShapeDtypeStruct((t, D_MODEL), F32),
                   jax.ShapeDtypeStruct((t, 3 * D_ATTN), BF16),
                   jax.ShapeDtypeStruct((t, D_LRU), F32),
                   jax.ShapeDtypeStruct((t, D_LRU), F32),
                   jax.ShapeDtypeStruct((t, 2 * D_MODEL), F32),
                   cache_shape, cache_shape] + [jax.ShapeDtypeStruct(w.shape, BF16) for w in late_weights],
        scratch_shapes=[pltpu.VMEM((tm, D_FF), BF16), pltpu.VMEM((tm, D_MODEL), BF16)],
        compiler_params=_params(),
        name="ffn1_inproj",
    )(xp, xs, mods, norm_g, wgu, wo, win, *late_weights)


def _ffn2_kernel(x1_ref, attnp_ref, attns_ref, lrup_ref, lrus_ref, gates_ref, mod_ref, ng_ref, fg_ref,
                 wba_ref, wbl_ref, wout_ref, wgu_ref, wo_ref, yp_ref, ys_ref, a_ref, h_ref,
                 *, n_ctx_tiles, tiles_per_seq):
    is_ctx = pl.program_id(0) < n_ctx_tiles
    m = _mod_rows(mod_ref, pl.program_id(0), n_ctx_tiles, tiles_per_seq)
    attn = jnp.where(is_ctx, attnp_ref[...], attns_ref[...])
    lru = jnp.where(is_ctx, lrup_ref[...], lrus_ref[...])
    gt = gates_ref[...]
    mm = (_sigmoid(gt[:, :D_MODEL]) * _dot(attn, wba_ref[...])
          + _sigmoid(gt[:, D_MODEL:]) * _dot(lru, wbl_ref[...]))
    x2 = x1_ref[...] + m[5] * _dot(mm.astype(BF16), wout_ref[...])
    h_ref[...] = _rms_mod(x2, ng_ref[2:3], m[6], m[7]).astype(BF16)
    x3 = x2 + (0.5 * m[8]) * _swiglu(h_ref, a_ref, wgu_ref, wo_ref)
    y = x3 * lax.rsqrt(jnp.mean(x3 * x3, axis=-1, keepdims=True) + EPS)
    y = y * fg_ref[...]
    ctx_tile = jnp.broadcast_to(is_ctx, y.shape)
    pltpu.store(yp_ref, y, mask=ctx_tile)
    pltpu.store(ys_ref, y, mask=jnp.logical_not(ctx_tile))


def _ffn2(x1, attnp, attns, lrup, lrus, gates, mods, norm_g, final_g, wba, wbl, wout, wgu, wo, tm, seq_lat):
    t = x1.shape[0]
    t_ctx = attnp.shape[0]
    n_ctx_tiles = t_ctx // tm
    row = lambda w: pl.BlockSpec((tm, w), lambda i: (i, 0))
    first, second = _two_stream(n_ctx_tiles, tm)
    return pl.pallas_call(
        functools.partial(_ffn2_kernel, n_ctx_tiles=n_ctx_tiles, tiles_per_seq=seq_lat // tm),
        grid=(t // tm,),
        in_specs=[row(D_MODEL), first(D_ATTN), second(D_ATTN), first(D_LRU), second(D_LRU), row(2 * D_MODEL),
                  _const_spec(mods.shape),
                  _const_spec(norm_g.shape), _const_spec(final_g.shape),
                  _const_spec(wba.shape), _const_spec(wbl.shape), _const_spec(wout.shape),
                  _const_spec(wgu.shape), _const_spec(wo.shape)],
        out_specs=[first(D_MODEL), second(D_MODEL)],
        out_shape=[jax.ShapeDtypeStruct((t_ctx, D_MODEL), F32), jax.ShapeDtypeStruct((t - t_ctx, D_MODEL), F32)],
        scratch_shapes=[pltpu.VMEM((tm, D_FF), BF16), pltpu.VMEM((tm, D_MODEL), BF16)],
        compiler_params=_params(),
        name="merge_ffn2",
    )(x1, attnp, attns, lrup, lrus, gates, mods, norm_g, final_g, wba, wbl, wout, wgu, wo)


def _softmax_pv(scores, values, value_dots=None):
    value_dots = value_dots or [_dot] * len(values)
    m = scores[0].max(axis=-1, keepdims=True)
    for s in scores[1:]:
        m = jnp.maximum(m, s.max(axis=-1, keepdims=True))
    l = None
    o = None
    for s, v, dot in zip(scores, values, value_dots):
        p = jnp.exp(s - m)
        ls = p.sum(axis=-1, keepdims=True)
        os_ = dot(p.astype(BF16), v)
        l = ls if l is None else l + ls
        o = os_ if o is None else o + os_
    return o / l


def _ctx_attn_kernel(qkv_ref, o_ref):
    seq = qkv_ref.shape[0] // CTX_SEQS_PER_STEP
    lane = lax.broadcasted_iota(jnp.int32, (seq, LANES), 1)
    first = lane < HEAD_DIM
    for sq in range(CTX_SEQS_PER_STEP):
        rows = slice(sq * seq, (sq + 1) * seq)
        for hp in range(D_ATTN // LANES):
            q = qkv_ref[rows, hp * LANES:(hp + 1) * LANES]
            k = qkv_ref[rows, D_ATTN + hp * LANES:D_ATTN + (hp + 1) * LANES]
            v = qkv_ref[rows, 2 * D_ATTN + hp * LANES:2 * D_ATTN + (hp + 1) * LANES]
            qm = jnp.concatenate([jnp.where(first, q, jnp.zeros_like(q)), jnp.where(first, jnp.zeros_like(q), q)],
                                 axis=0)
            o = _softmax_pv([_dot_nt(qm, k)], [v])
            o_ref[rows, hp * LANES:(hp + 1) * LANES] = jnp.where(first, o[:seq], o[seq:]).astype(BF16)


def _ctx_attention(qkv, nb, seq):
    assert nb % CTX_SEQS_PER_STEP == 0
    blk = seq * CTX_SEQS_PER_STEP
    return pl.pallas_call(
        _ctx_attn_kernel,
        grid=(nb // CTX_SEQS_PER_STEP,),
        in_specs=[pl.BlockSpec((blk, 3 * D_ATTN), lambda b: (b, 0))],
        out_specs=pl.BlockSpec((blk, D_ATTN), lambda b: (b, 0)),
        out_shape=jax.ShapeDtypeStruct((nb * seq, D_ATTN), BF16),
        compiler_params=_params(),
        name="ctx_attention",
    )(qkv)


def _na_groups(rows):
    kh = min(WIN_H, rows)
    r = np.arange(rows)
    rs = np.clip(r - kh // 2, 0, rows - kh)
    groups = []
    for g in range(rows // ROWS_PER_GROUP):
        qr = r[g * ROWS_PER_GROUP:(g + 1) * ROWS_PER_GROUP]
        k0, k1 = int(rs[qr].min()), int(rs[qr].max()) + kh
        if (k1 - k0) % 2:
            if k1 < rows:
                k1 += 1
            else:
                k0 -= 1
        groups.append((k0, k1))
    return groups, rs, kh


def _na_rel_rows(rpb):
    edge = GRID_W - WIN_W
    v = jnp.concatenate([rpb[..., WIN_W - 1:],
                         jnp.repeat(rpb[..., -1:], edge, axis=-1),
                         jnp.zeros(rpb.shape[:-1] + (1,), F32),
                         jnp.repeat(rpb[..., :1], edge, axis=-1),
                         rpb[..., :WIN_W - 1]], axis=-1).astype(F32)
    return jnp.pad(v, ((0, 0), (0, 1), (0, 0)))


def _na_build_bias(rel_ref, bias_ref, groups, rs, kh):
    shape = (GRID_W, LANES)
    qc = lax.broadcasted_iota(jnp.int32, shape, 0)
    lane = lax.broadcasted_iota(jnp.int32, shape, 1)
    left = lane < GRID_W
    kc = jnp.where(left, lane, lane - GRID_W)
    cs = jnp.clip(qc - WIN_W // 2, 0, GRID_W - WIN_W)
    in_win = (kc >= cs) & (kc < cs + WIN_W)
    neg = jnp.full(shape, NEG_INF, F32)
    for hh in range(2):
        @functools.cache
        def pair_tile(dr, a_ok, b_ok, hh=hh):
            def rolled(d, half):
                row = jnp.broadcast_to(rel_ref[hh, d + WIN_H - 1:d + WIN_H, :], shape)
                return pltpu.roll(row, half * GRID_W, 1, stride=1, stride_axis=0)

            if a_ok and b_ok:
                return jnp.where(in_win, jnp.where(left, rolled(dr, 0), rolled(dr + 1, 1)), neg)
            if a_ok:
                return jnp.where(in_win & left, rolled(dr, 0), neg)
            if b_ok:
                return jnp.where(in_win & jnp.logical_not(left), rolled(dr + 1, 1), neg)
            return neg

        boff = 0
        for g, (k0, k1) in enumerate(groups):
            for i in range(ROWS_PER_GROUP):
                r = g * ROWS_PER_GROUP + i
                for jp in range((k1 - k0) // 2):
                    j = k0 + 2 * jp
                    a_ok = rs[r] <= j < rs[r] + kh
                    b_ok = rs[r] <= j + 1 < rs[r] + kh
                    t = pair_tile(j - r if (a_ok or b_ok) else 0, a_ok, b_ok)
                    bias_ref[hh, i * GRID_W:(i + 1) * GRID_W, boff + jp * LANES:boff + (jp + 1) * LANES] = t
            boff += (k1 - k0) * GRID_W


def _na_attn_kernel(q_ref, k_ref, v_ref, kc_ref, vc_ref, rel_ref, o_ref, bias_ref, *, groups, rs, kh):
    @pl.when(pl.program_id(1) == 0)
    def _():
        _na_build_bias(rel_ref, bias_ref, groups, rs, kh)

    gq = ROWS_PER_GROUP * GRID_W
    lane = lax.broadcasted_iota(jnp.int32, (gq, LANES), 1)
    first = lane < HEAD_DIM

    past = kc_ref.shape[-1]
    seq = q_ref.shape[0] // NA_SEQS_PER_STEP
    for sq in range(NA_SEQS_PER_STEP):
        kc = kc_ref[sq, 0].reshape(LANES, past).astype(BF16)
        vc = vc_ref[sq, 0].reshape(LANES, past).astype(BF16)
        r0 = sq * seq
        for g, (k0, k1) in enumerate(groups):
            q = q_ref[r0 + g * gq:r0 + (g + 1) * gq, :]
            k = k_ref[r0 + k0 * GRID_W:r0 + k1 * GRID_W, :]
            v = v_ref[r0 + k0 * GRID_W:r0 + k1 * GRID_W, :]
            boff = sum((b - a) * GRID_W for a, b in groups[:g])
            qm = jnp.concatenate([jnp.where(first, q, jnp.zeros_like(q)), jnp.where(first, jnp.zeros_like(q), q)],
                                 axis=0)
            bias = jnp.concatenate([bias_ref[hh, :, boff:boff + (k1 - k0) * GRID_W] for hh in range(2)], axis=0)
            o = _softmax_pv([_dot_nt(qm, k) + bias, _dot(qm, kc)], [v, vc], [_dot, _dot_nt])
            o_ref[r0 + g * gq:r0 + (g + 1) * gq, :] = jnp.where(first, o[:gq], o[gq:]).astype(BF16)


def _na_attention(qkv, kc, vc, rel, first_seq, nb, seq):
    t = nb * seq
    past = kc.shape[4]
    npair = D_ATTN // LANES
    groups, rs, kh = _na_groups(seq // GRID_W)
    width = sum(k1 - k0 for k0, k1 in groups) * GRID_W
    nsq = NA_SEQS_PER_STEP
    assert nb % nsq == 0 and first_seq % nsq == 0
    tok = lambda off: pl.BlockSpec((nsq * seq, LANES), lambda hp, b: (first_seq // nsq + b, off + hp))
    ctx = pl.BlockSpec((nsq, 1, 2, HEAD_DIM, past), lambda hp, b: (b, 0, hp, 0, 0))
    return pl.pallas_call(
        functools.partial(_na_attn_kernel, groups=groups, rs=[int(x) for x in rs], kh=kh),
        grid=(npair, nb // nsq),
        in_specs=[tok(0), tok(npair), tok(2 * npair), ctx, ctx,
                  pl.BlockSpec((2,) + rel.shape[1:], lambda hp, b: (hp, 0, 0))],
        out_specs=pl.BlockSpec((nsq * seq, LANES), lambda hp, b: (b, hp)),
        out_shape=jax.ShapeDtypeStruct((t, D_ATTN), BF16),
        scratch_shapes=[pltpu.VMEM((2, ROWS_PER_GROUP * GRID_W, width), F32)],
        compiler_params=_params(2),
        name="na_attention",
    )(qkv, qkv, qkv, kc, vc, rel)


def _gelu_tanh(x):
    c = float(np.sqrt(2.0 / np.pi))
    t = jnp.tanh(x * (c + (c * 0.044715) * (x * x)))
    return x * (0.5 * t + 0.5)


def _build_blockdiag(wa_ref, wi_ref, wbd_ref):
    per = LRU_TILE // LRU_BLOCK
    wbd_ref[...] = jnp.zeros(wbd_ref.shape, BF16)
    zero = jnp.zeros((LRU_BLOCK, LRU_BLOCK), F32)
    for d in range(2):
        for t in range(N_LRU_TILES):
            for p in range(per):
                rows = slice(p * LRU_BLOCK, (p + 1) * LRU_BLOCK)
                for half, w_ref in enumerate((wa_ref, wi_ref)):
                    blk = 0.5 * w_ref[d, t * per + p]
                    piece = jnp.concatenate([blk, zero] if p % 2 == 0 else [zero, blk], axis=1)
                    c0 = half * LRU_TILE + (p // 2) * LANES
                    wbd_ref[d, t, rows, c0:c0 + LANES] = piece.astype(BF16)


def _lru_kernel(xl_ref, gl_ref, h0_ref, cw_ref, cb_ref, wa_ref, wi_ref, ba_ref, bi_ref, coef_ref,
                y_ref, hfin_ref, xpad, xc, a_buf, hf_buf, hb_buf, wbd_ref, *, seq):
    pitch = seq // N_SEG + SEG_PAD
    lp = N_SEG * pitch
    halo = SUBLANES
    chunk = LANES
    n_chunks = seq // chunk
    left = CONV_W // 2

    @pl.when(pl.program_id(0) == 0)
    def _():
        _build_blockdiag(wa_ref, wi_ref, wbd_ref)

    for k in range(N_SLABS):
        cols = slice(k * LANES, (k + 1) * LANES)
        xpad[k, 0:halo, :] = jnp.zeros((halo, LANES), F32)
        xpad[k, halo + seq:2 * halo + seq, :] = jnp.zeros((halo, LANES), F32)
        xpad[k, halo:halo + seq, :] = xl_ref[:, cols]

    def conv_body(i, carry):
        r0 = pl.multiple_of(i * chunk, chunk)
        for k in range(N_SLABS):
            cols = slice(k * LANES, (k + 1) * LANES)
            out = cb_ref[:, cols]
            for j in range(CONV_W):
                out = out + xpad[k, pl.ds(r0 + halo + j - left, chunk), :] * cw_ref[j:j + 1, cols]
            xc[k, pl.ds(r0, chunk), :] = out
        return carry

    lax.fori_loop(0, n_chunks, conv_body, 0)

    a_buf[:, seq:lp, :] = jnp.ones((N_SLABS, lp - seq, LANES), F32)
    hf_buf[:, seq:lp, :] = jnp.zeros((N_SLABS, lp - seq, LANES), F32)
    hb_buf[:, seq:lp, :] = jnp.zeros((N_SLABS, lp - seq, LANES), F32)

    row = lax.broadcasted_iota(jnp.int32, (N_SEG, LANES), 0)
    for d, h_buf in enumerate((hf_buf, hb_buf)):
        def gates_body(i, carry, d=d, h_buf=h_buf):
            rows = pl.ds(pl.multiple_of(i * chunk, chunk), chunk)
            per = LRU_TILE // LANES
            for c in range(N_LRU_TILES):
                xs = [xc[c * per + half, rows, :] for half in range(per)]
                pre = _dot(jnp.concatenate(xs, axis=1).astype(BF16), wbd_ref[d, c])
                for half in range(per):
                    k = c * per + half
                    cols = slice(k * LANES, (k + 1) * LANES)
                    tr = jnp.tanh(pre[:, half * LANES:(half + 1) * LANES] + ba_ref[d:d + 1, cols])
                    ti = jnp.tanh(pre[:, LRU_TILE + half * LANES:LRU_TILE + (half + 1) * LANES]
                                  + bi_ref[d:d + 1, cols])
                    a = jnp.exp2(coef_ref[d:d + 1, cols] * tr + coef_ref[d:d + 1, cols])
                    y = (1.0 - a) * (1.0 + a)
                    mult = y * lax.rsqrt(jnp.maximum(y, F32_TINY))
                    a_buf[k, rows, :] = a
                    h_buf[k, rows, :] = mult * ((0.5 * ti + 0.5) * xs[half])
            return carry

        lax.fori_loop(0, n_chunks, gates_body, 0, unroll=min(8, n_chunks))

        base0 = jnp.int32(0 if d == 0 else pitch - SCAN_UNROLL)
        base_step = SCAN_UNROLL if d == 0 else -SCAN_UNROLL

        def seg_rows(base, u, d=d):
            return pl.ds(base + (u if d == 0 else SCAN_UNROLL - 1 - u), N_SEG, stride=pitch)

        def totals_body(_, carry, h_buf=h_buf):
            base, hs, ps = carry[0], list(carry[1]), list(carry[2])
            for u in range(SCAN_UNROLL):
                idx = seg_rows(base, u)
                for k in range(N_SLABS):
                    a = a_buf[k, idx, :]
                    hs[k] = a * hs[k] + h_buf[k, idx, :]
                    ps[k] = ps[k] * a
            return base + base_step, tuple(hs), tuple(ps)

        zeros = tuple(jnp.zeros((N_SEG, LANES), F32) for _ in range(N_SLABS))
        ones = tuple(jnp.ones((N_SEG, LANES), F32) for _ in range(N_SLABS))
        _, h_end, p_end = lax.fori_loop(0, pitch // SCAN_UNROLL, totals_body, (base0, zeros, ones))

        order = range(N_SEG) if d == 0 else range(N_SEG - 1, -1, -1)
        starts = []
        for k in range(N_SLABS):
            c = h0_ref[0, d:d + 1, k * LANES:(k + 1) * LANES]
            cm = jnp.zeros((N_SEG, LANES), F32)
            for s in order:
                cm = jnp.where(row == s, c, cm)
                c = p_end[k][s:s + 1, :] * c + h_end[k][s:s + 1, :]
            starts.append(cm)
            hfin_ref[0, d:d + 1, k * LANES:(k + 1) * LANES] = c

        def scan_body(_, carry, h_buf=h_buf):
            base, hs = carry[0], list(carry[1])
            for u in range(SCAN_UNROLL):
                idx = seg_rows(base, u)
                for k in range(N_SLABS):
                    hs[k] = a_buf[k, idx, :] * hs[k] + h_buf[k, idx, :]
                    h_buf[k, idx, :] = hs[k]
            return base + base_step, tuple(hs)

        lax.fori_loop(0, pitch // SCAN_UNROLL, scan_body, (base0, tuple(starts)))

    def out_body(i, carry):
        r0 = pl.multiple_of(i * chunk, chunk)
        for k in range(N_SLABS):
            cols = slice(k * LANES, (k + 1) * LANES)
            h = hf_buf[k, pl.ds(r0, chunk), :] + hb_buf[k, pl.ds(r0, chunk), :]
            y_ref[pl.ds(r0, chunk), cols] = (h * _gelu_tanh(gl_ref[pl.ds(r0, chunk), cols])).astype(BF16)
        return carry

    lax.fori_loop(0, n_chunks, out_body, 0)


def _lru(xl, gl, h0, conv_w, conv_b, wa, wi, ba, bi, coef, first_seq, nb, seq):
    t = nb * seq
    lp = N_SEG * (seq // N_SEG + SEG_PAD)
    tok_in = pl.BlockSpec((seq, D_LRU), lambda b: (first_seq + b, 0))
    tok = pl.BlockSpec((seq, D_LRU), lambda b: (b, 0))
    state = pl.BlockSpec((1, 2, D_LRU), lambda b: (b, 0, 0))
    scan_buf = pltpu.VMEM((N_SLABS, lp, LANES), F32)
    return pl.pallas_call(
        functools.partial(_lru_kernel, seq=seq),
        grid=(nb,),
        in_specs=[tok_in, tok_in, state, _const_spec(conv_w.shape), _const_spec(conv_b.shape),
                  _const_spec(wa.shape), _const_spec(wi.shape), _const_spec(ba.shape), _const_spec(bi.shape),
                  _const_spec(coef.shape)],
        out_specs=[tok, state],
        out_shape=[jax.ShapeDtypeStruct((t, D_LRU), BF16), jax.ShapeDtypeStruct((nb, 2, D_LRU), F32)],
        scratch_shapes=[pltpu.VMEM((N_SLABS, seq + 2 * SUBLANES, LANES), F32),
                        pltpu.VMEM((N_SLABS, seq, LANES), F32), scan_buf, scan_buf, scan_buf,
                        pltpu.VMEM((2, N_LRU_TILES, LRU_TILE, 2 * LRU_TILE), BF16)],
        compiler_params=_params(),
        name="rglru",
    )(xl, gl, h0, conv_w, conv_b, wa, wi, ba, bi, coef)


def kernel(x_prompt, x_sample, cache_k, cache_v, state_lru, c, c_ctx, w_mod, b_mod, norm_g, ffn1_w_in, ffn1_w_out,
           w_in, rpb, conv_w, conv_b, lru_wa, lru_ba, lru_wi, lru_bi, lru_lambda, w_br_attn, w_br_lru, w_out,
           ffn2_w_in, ffn2_w_out, final_g):
    assert w_mod.shape[0] == 1, "single trunk layer"
    nb_ctx, seq_ctx, _ = x_prompt.shape
    nb_lat, seq_lat, _ = x_sample.shape

    n_cond = 1 + nb_lat
    cond = jnp.concatenate([c_ctx[None, :], c, jnp.zeros((2 * SUBLANES - n_cond, D_MODEL), F32)], axis=0)
    mods, wgu1, wo1, win = _mods(cond, w_mod[0], b_mod[0], (ffn1_w_in[0], ffn1_w_out[0], w_in[0]))

    late_weights = (w_br_attn[0], w_br_lru[0], w_out[0], ffn2_w_in[0], ffn2_w_out[0])
    coef = (0.5 * float(np.log2(np.e))) * (-LRU_C * jax.nn.softplus(-lru_lambda[0]))
    ba = 0.5 * lru_ba[0]
    bi = 0.5 * lru_bi[0]
    ng = norm_g[0]
    fg = final_g.reshape(1, D_MODEL)
    cb = conv_b[0].reshape(1, D_LRU)

    t_ctx = nb_ctx * seq_ctx
    assert t_ctx % seq_lat == 0, "latent sequences must start on a latent-sequence-sized row block"
    xp = x_prompt.reshape(t_ctx, D_MODEL)
    xs = x_sample.reshape(nb_lat * seq_lat, D_MODEL)

    x1, qkv, xl, gl, gates, new_k, new_v, wba, wbl, wout, wgu2, wo2 = _ffn1(
        xp, xs, mods, ng, wgu1, wo1, win, late_weights, nb_ctx, seq_ctx, seq_lat)

    attnp = _ctx_attention(qkv, nb_ctx, seq_ctx)
    lrup, h_fin = _lru(xl, gl, jnp.zeros((nb_ctx, 2, D_LRU), F32), conv_w[0], cb, lru_wa[0], lru_wi[0],
                       ba, bi, coef, 0, nb_ctx, seq_ctx)

    rel = _na_rel_rows(rpb[0])
    first_lat = t_ctx // seq_lat
    attns = _na_attention(qkv, cache_k.swapaxes(3, 4), cache_v.swapaxes(3, 4), rel, first_lat, nb_lat, seq_lat)
    lrus, _ = _lru(xl, gl, state_lru[:, 0], conv_w[0], cb, lru_wa[0], lru_wi[0], ba, bi, coef,
                   first_lat, nb_lat, seq_lat)

    yp, ys = _ffn2(x1, attnp, attns, lrup, lrus, gates, mods, ng, fg, wba, wbl, wout, wgu2, wo2,
                   FFN2_TILE, seq_lat)
    return (yp.reshape(nb_ctx, seq_ctx, D_MODEL), ys.reshape(nb_lat, seq_lat, D_MODEL),
            new_k, new_v, h_fin[:, None])
```

```python
import functools

import numpy as np
import jax
import jax.numpy as jnp
from jax import lax
from jax.experimental import pallas as pl
from jax.experimental.pallas import tpu as pltpu

F32 = jnp.float32
BF16 = jnp.bfloat16

D_MODEL = 1024
N_HEADS = 8
HEAD_DIM = 64
D_ATTN = N_HEADS * HEAD_DIM
GRID_W = 64
WIN_H = 8
WIN_W = 16
D_LRU = 1024
LRU_BLOCKS = 16
LRU_BLOCK = D_LRU // LRU_BLOCKS
CONV_W = 4
LRU_C = 8.0
D_FF = 2816
N_MOD = 9
EPS = 1e-6
NEG_INF = -1e30
F32_TINY = float(np.finfo(np.float32).tiny)
ATTN_SCALE = HEAD_DIM ** -0.5

LANES = 128
SUBLANES = 8
MXU_DIM = 256
VMEM_LIMIT_BYTES = 56 * 1024 * 1024

FF_CHUNK = MXU_DIM
N_FF_CHUNKS = D_FF // FF_CHUNK
LRU_TILE = MXU_DIM
N_LRU_TILES = D_LRU // LRU_TILE
N_SLABS = D_LRU // LANES
N_SEG = SUBLANES
SEG_PAD = 4
SCAN_UNROLL = 12
ROWS_PER_GROUP = 4
FFN2_TILE = 2 * MXU_DIM
CTX_SEQS_PER_STEP = 8
NA_SEQS_PER_STEP = 2


def _sigmoid(x):
    return 0.5 * jnp.tanh(0.5 * x) + 0.5


def _rms_mod(x, g, shift, scale):
    y = x * lax.rsqrt(jnp.mean(x * x, axis=-1, keepdims=True) + EPS)
    y = y * g
    return y * (1 + scale) + shift


def _dot(a, b):
    return jnp.dot(a, b, preferred_element_type=F32)


def _dot_nt(a, b):
    return lax.dot_general(a, b, (((1,), (1,)), ((), ())), preferred_element_type=F32)


def _params(n_axes=1):
    return pltpu.CompilerParams(dimension_semantics=("arbitrary",) * n_axes,
                                vmem_limit_bytes=VMEM_LIMIT_BYTES)


def _const_spec(shape):
    nd = len(shape)
    return pl.BlockSpec(shape, lambda *_: (0,) * nd, pipeline_mode=pl.Buffered(1))


MODS_K_CHUNK = 128


def _mods_kernel(cond_ref, w_ref, b_ref, *rest):
    n_cast = (len(rest) - 1) // 2
    cast_f32, o_ref, cast_bf16 = rest[:n_cast], rest[n_cast], rest[n_cast + 1:]

    for src, dst in zip(cast_f32, cast_bf16):
        dst[...] = src[...].astype(BF16)

    @pl.when(pl.program_id(0) == 0)
    def _():
        o_ref[...] = jnp.broadcast_to(b_ref[...], o_ref.shape)

    c = cond_ref[...]
    s = (c * _sigmoid(c)).astype(BF16)
    o_ref[...] += _dot(s, w_ref[...].astype(BF16))


def _mods(cond, w_mod, b_mod, cast_weights):
    k, n = w_mod.shape
    n_steps = k // MODS_K_CHUNK
    cast_specs = [_cast_spec(w, n_steps) for w in cast_weights]
    return pl.pallas_call(
        _mods_kernel,
        grid=(n_steps,),
        in_specs=[pl.BlockSpec((cond.shape[0], MODS_K_CHUNK), lambda j: (0, j)),
                  pl.BlockSpec((MODS_K_CHUNK, n), lambda j: (j, 0)),
                  pl.BlockSpec((1, n), lambda j: (0, 0))] + cast_specs,
        out_specs=[pl.BlockSpec((cond.shape[0], n), lambda j: (0, 0))] + cast_specs,
        out_shape=[jax.ShapeDtypeStruct((cond.shape[0], n), F32)]
        + [jax.ShapeDtypeStruct(w.shape, BF16) for w in cast_weights],
        compiler_params=_params(),
        name="mods",
    )(cond, w_mod, b_mod.reshape(1, n), *cast_weights)


def _swiglu(h_ref, a_ref, wgu_ref, wo_ref):
    h = h_ref[...]
    for j in range(N_FF_CHUNKS):
        cols = slice(j * FF_CHUNK, (j + 1) * FF_CHUNK)
        g = _dot(h, wgu_ref[:, cols])
        u = _dot(h, wgu_ref[:, D_FF + j * FF_CHUNK:D_FF + (j + 1) * FF_CHUNK])
        a_ref[:, cols] = ((g * _sigmoid(g)) * u).astype(BF16)
    return _dot(a_ref[...], wo_ref[...])


BF16_ROWS = 2 * SUBLANES


def _cast_spec(w, n_steps):
    rows = BF16_ROWS * pl.cdiv(w.shape[0], BF16_ROWS * n_steps)
    n_chunks = w.shape[0] // rows
    assert n_chunks * rows == w.shape[0] and n_chunks <= n_steps
    return pl.BlockSpec((rows, w.shape[1]), lambda i: (jnp.minimum(i, n_chunks - 1), 0))


def _two_stream(n_first, tm_rows):
    first = lambda w: pl.BlockSpec((tm_rows, w), lambda i: (jnp.minimum(i, n_first - 1), 0))
    second = lambda w: pl.BlockSpec((tm_rows, w), lambda i: (jnp.maximum(i - n_first, 0), 0))
    return first, second


def _mod_rows(mods_ref, i, n_ctx_tiles, tiles_per_seq):
    row = jnp.where(i < n_ctx_tiles, 0, 1 + jnp.maximum(i - n_ctx_tiles, 0) // tiles_per_seq)
    m = mods_ref[pl.ds(row, 1), :]
    return [m[:, k * D_MODEL:(k + 1) * D_MODEL] for k in range(N_MOD)]


N_LATE_WEIGHTS = 5


def _ffn1_kernel(xp_ref, xs_ref, mod_ref, ng_ref, wgu_ref, wo_ref, win_ref, *rest, n_ctx_tiles, tiles_per_seq):
    late_f32 = rest[:N_LATE_WEIGHTS]
    x1_ref, qkv_ref, xl_ref, gl_ref, gates_ref, newk_ref, newv_ref = rest[N_LATE_WEIGHTS:N_LATE_WEIGHTS + 7]
    late_bf16 = rest[N_LATE_WEIGHTS + 7:2 * N_LATE_WEIGHTS + 7]
    a_ref, h_ref = rest[2 * N_LATE_WEIGHTS + 7:]

    for src, dst in zip(late_f32, late_bf16):
        dst[...] = src[...].astype(BF16)

    is_ctx = pl.program_id(0) < n_ctx_tiles
    x = jnp.where(is_ctx, xp_ref[...], xs_ref[...])
    m = _mod_rows(mod_ref, pl.program_id(0), n_ctx_tiles, tiles_per_seq)
    h_ref[...] = _rms_mod(x, ng_ref[0:1], m[0], m[1]).astype(BF16)
    x1 = x + (0.5 * m[2]) * _swiglu(h_ref, a_ref, wgu_ref, wo_ref)
    x1_ref[...] = x1
    h2 = _rms_mod(x1, ng_ref[1:2], m[3], m[4]).astype(BF16)
    c0, c1, c2, c3 = 3 * D_ATTN, 3 * D_ATTN + D_LRU, 3 * D_ATTN + 2 * D_LRU, 3 * D_ATTN + 2 * D_LRU + 2 * D_MODEL
    qkv = _dot(h2, win_ref[:, 0:c0])
    qkv_ref[:, :D_ATTN] = (qkv[:, :D_ATTN] * ATTN_SCALE).astype(BF16)
    qkv_ref[:, D_ATTN:] = qkv[:, D_ATTN:].astype(BF16)
    ctx_block = jnp.broadcast_to(is_ctx, newk_ref.shape)
    for ref, off in ((newk_ref, D_ATTN), (newv_ref, 2 * D_ATTN)):
        heads = [qkv[:, off + h * HEAD_DIM:off + (h + 1) * HEAD_DIM] for h in range(N_HEADS)]
        pltpu.store(ref, jnp.stack(heads, axis=0)[None, None], mask=ctx_block)
    xl_ref[...] = _dot(h2, win_ref[:, c0:c1])
    gl_ref[...] = _dot(h2, win_ref[:, c1:c2])
    gates_ref[...] = _dot(h2, win_ref[:, c2:c3])


def _ffn1(xp, xs, mods, norm_g, wgu, wo, win, late_weights, nb_ctx, seq_ctx, seq_lat):
    tm = seq_ctx
    n_ctx_tiles = xp.shape[0] // tm
    t = xp.shape[0] + xs.shape[0]
    n_steps = t // tm
    assert len(late_weights) == N_LATE_WEIGHTS
    row = lambda w: pl.BlockSpec((tm, w), lambda i: (i, 0))
    first, second = _two_stream(n_ctx_tiles, tm)
    cache_spec = pl.BlockSpec((1, 1, N_HEADS, seq_ctx, HEAD_DIM),
                              lambda i: (jnp.minimum(i, n_ctx_tiles - 1), 0, 0, 0, 0))
    cache_shape = jax.ShapeDtypeStruct((nb_ctx, 1, N_HEADS, seq_ctx, HEAD_DIM), F32)
    cast_specs = [_cast_spec(w, n_steps) for w in late_weights]
    return pl.pallas_call(
        functools.partial(_ffn1_kernel, n_ctx_tiles=n_ctx_tiles, tiles_per_seq=seq_lat // tm),
        grid=(n_steps,),
        in_specs=[first(D_MODEL), second(D_MODEL), _const_spec(mods.shape),
                  _const_spec(norm_g.shape), _const_spec(wgu.shape), _const_spec(wo.shape),
                  _const_spec(win.shape)] + cast_specs,
        out_specs=[row(D_MODEL), row(3 * D_ATTN), row(D_LRU), row(D_LRU), row(2 * D_MODEL),
                   cache_spec, cache_spec] + cast_specs,
        out_shape=[jax.ShapeDtypeStruct((t, D_MODEL), F32),
                   jax.ShapeDtypeStruct((t, 3 * D_ATTN), BF16),
                   jax.ShapeDtypeStruct((t, D_LRU), F32),
                   jax.ShapeDtypeStruct((t, D_LRU), F32),
                   jax.ShapeDtypeStruct((t, 2 * D_MODEL), F32),
                   cache_shape, cache_shape] + [jax.ShapeDtypeStruct(w.shape, BF16) for w in late_weights],
        scratch_shapes=[pltpu.VMEM((tm, D_FF), BF16), pltpu.VMEM((tm, D_MODEL), BF16)],
        compiler_params=_params(),
        name="ffn1_inproj",
    )(xp, xs, mods, norm_g, wgu, wo, win, *late_weights)


def _ffn2_kernel(x1_ref, attnp_ref, attns_ref, lrup_ref, lrus_ref, gates_ref, mod_ref, ng_ref, fg_ref,
                 wba_ref, wbl_ref, wout_ref, wgu_ref, wo_ref, yp_ref, ys_ref, a_ref, h_ref,
                 *, n_ctx_tiles, tiles_per_seq):
    is_ctx = pl.program_id(0) < n_ctx_tiles
    m = _mod_rows(mod_ref, pl.program_id(0), n_ctx_tiles, tiles_per_seq)
    attn = jnp.where(is_ctx, attnp_ref[...], attns_ref[...])
    lru = jnp.where(is_ctx, lrup_ref[...], lrus_ref[...])
    gt = gates_ref[...]
    mm = (_sigmoid(gt[:, :D_MODEL]) * _dot(attn, wba_ref[...])
          + _sigmoid(gt[:, D_MODEL:]) * _dot(lru, wbl_ref[...]))
    x2 = x1_ref[...] + m[5] * _dot(mm.astype(BF16), wout_ref[...])
    h_ref[...] = _rms_mod(x2, ng_ref[2:3], m[6], m[7]).astype(BF16)
    x3 = x2 + (0.5 * m[8]) * _swiglu(h_ref, a_ref, wgu_ref, wo_ref)
    y = x3 * lax.rsqrt(jnp.mean(x3 * x3, axis=-1, keepdims=True) + EPS)
    y = y * fg_ref[...]
    ctx_tile = jnp.broadcast_to(is_ctx, y.shape)
    pltpu.store(yp_ref, y, mask=ctx_tile)
    pltpu.store(ys_ref, y, mask=jnp.logical_not(ctx_tile))


def _ffn2(x1, attnp, attns, lrup, lrus, gates, mods, norm_g, final_g, wba, wbl, wout, wgu, wo, tm, seq_lat):
    t = x1.shape[0]
    t_ctx = attnp.shape[0]
    n_ctx_tiles = t_ctx // tm
    row = lambda w: pl.BlockSpec((tm, w), lambda i: (i, 0))
    first, second = _two_stream(n_ctx_tiles, tm)
    return pl.pallas_call(
        functools.partial(_ffn2_kernel, n_ctx_tiles=n_ctx_tiles, tiles_per_seq=seq_lat // tm),
        grid=(t // tm,),
        in_specs=[row(D_MODEL), first(D_ATTN), second(D_ATTN), first(D_LRU), second(D_LRU), row(2 * D_MODEL),
                  _const_spec(mods.shape),
                  _const_spec(norm_g.shape), _const_spec(final_g.shape),
                  _const_spec(wba.shape), _const_spec(wbl.shape), _const_spec(wout.shape),
                  _const_spec(wgu.shape), _const_spec(wo.shape)],
        out_specs=[first(D_MODEL), second(D_MODEL)],
        out_shape=[jax.ShapeDtypeStruct((t_ctx, D_MODEL), F32), jax.ShapeDtypeStruct((t - t_ctx, D_MODEL), F32)],
        scratch_shapes=[pltpu.VMEM((tm, D_FF), BF16), pltpu.VMEM((tm, D_MODEL), BF16)],
        compiler_params=_params(),
        name="merge_ffn2",
    )(x1, attnp, attns, lrup, lrus, gates, mods, norm_g, final_g, wba, wbl, wout, wgu, wo)


def _softmax_pv(scores, values, value_dots=None):
    value_dots = value_dots or [_dot] * len(values)
    m = scores[0].max(axis=-1, keepdims=True)
    for s in scores[1:]:
        m = jnp.maximum(m, s.max(axis=-1, keepdims=True))
    l = None
    o = None
    for s, v, dot in zip(scores, values, value_dots):
        p = jnp.exp(s - m)
        ls = p.sum(axis=-1, keepdims=True)
        os_ = dot(p.astype(BF16), v)
        l = ls if l is None else l + ls
        o = os_ if o is None else o + os_
    return o / l


def _ctx_attn_kernel(qkv_ref, o_ref):
    seq = qkv_ref.shape[0] // CTX_SEQS_PER_STEP
    lane = lax.broadcasted_iota(jnp.int32, (seq, LANES), 1)
    first = lane < HEAD_DIM
    for sq in range(CTX_SEQS_PER_STEP):
        rows = slice(sq * seq, (sq + 1) * seq)
        for hp in range(D_ATTN // LANES):
            q = qkv_ref[rows, hp * LANES:(hp + 1) * LANES]
            k = qkv_ref[rows, D_ATTN + hp * LANES:D_ATTN + (hp + 1) * LANES]
            v = qkv_ref[rows, 2 * D_ATTN + hp * LANES:2 * D_ATTN + (hp + 1) * LANES]
            qm = jnp.concatenate([jnp.where(first, q, jnp.zeros_like(q)), jnp.where(first, jnp.zeros_like(q), q)],
                                 axis=0)
            o = _softmax_pv([_dot_nt(qm, k)], [v])
            o_ref[rows, hp * LANES:(hp + 1) * LANES] = jnp.where(first, o[:seq], o[seq:]).astype(BF16)


def _ctx_attention(qkv, nb, seq):
    assert nb % CTX_SEQS_PER_STEP == 0
    blk = seq * CTX_SEQS_PER_STEP
    return pl.pallas_call(
        _ctx_attn_kernel,
        grid=(nb // CTX_SEQS_PER_STEP,),
        in_specs=[pl.BlockSpec((blk, 3 * D_ATTN), lambda b: (b, 0))],
        out_specs=pl.BlockSpec((blk, D_ATTN), lambda b: (b, 0)),
        out_shape=jax.ShapeDtypeStruct((nb * seq, D_ATTN), BF16),
        compiler_params=_params(),
        name="ctx_attention",
    )(qkv)


def _na_groups(rows):
    kh = min(WIN_H, rows)
    r = np.arange(rows)
    rs = np.clip(r - kh // 2, 0, rows - kh)
    groups = []
    for g in range(rows // ROWS_PER_GROUP):
        qr = r[g * ROWS_PER_GROUP:(g + 1) * ROWS_PER_GROUP]
        k0, k1 = int(rs[qr].min()), int(rs[qr].max()) + kh
        if (k1 - k0) % 2:
            if k1 < rows:
                k1 += 1
            else:
                k0 -= 1
        groups.append((k0, k1))
    return groups, rs, kh


def _na_rel_rows(rpb):
    edge = GRID_W - WIN_W
    v = jnp.concatenate([rpb[..., WIN_W - 1:],
                         jnp.repeat(rpb[..., -1:], edge, axis=-1),
                         jnp.zeros(rpb.shape[:-1] + (1,), F32),
                         jnp.repeat(rpb[..., :1], edge, axis=-1),
                         rpb[..., :WIN_W - 1]], axis=-1).astype(F32)
    return jnp.pad(v, ((0, 0), (0, 1), (0, 0)))


def _na_build_bias(rel_ref, bias_ref, groups, rs, kh):
    shape = (GRID_W, LANES)
    qc = lax.broadcasted_iota(jnp.int32, shape, 0)
    lane = lax.broadcasted_iota(jnp.int32, shape, 1)
    left = lane < GRID_W
    kc = jnp.where(left, lane, lane - GRID_W)
    cs = jnp.clip(qc - WIN_W // 2, 0, GRID_W - WIN_W)
    in_win = (kc >= cs) & (kc < cs + WIN_W)
    neg = jnp.full(shape, NEG_INF, F32)
    for hh in range(2):
        @functools.cache
        def pair_tile(dr, a_ok, b_ok, hh=hh):
            def rolled(d, half):
                row = jnp.broadcast_to(rel_ref[hh, d + WIN_H - 1:d + WIN_H, :], shape)
                return pltpu.roll(row, half * GRID_W, 1, stride=1, stride_axis=0)

            if a_ok and b_ok:
                return jnp.where(in_win, jnp.where(left, rolled(dr, 0), rolled(dr + 1, 1)), neg)
            if a_ok:
                return jnp.where(in_win & left, rolled(dr, 0), neg)
            if b_ok:
                return jnp.where(in_win & jnp.logical_not(left), rolled(dr + 1, 1), neg)
            return neg

        boff = 0
        for g, (k0, k1) in enumerate(groups):
            for i in range(ROWS_PER_GROUP):
                r = g * ROWS_PER_GROUP + i
                for jp in range((k1 - k0) // 2):
                    j = k0 + 2 * jp
                    a_ok = rs[r] <= j < rs[r] + kh
                    b_ok = rs[r] <= j + 1 < rs[r] + kh
                    t = pair_tile(j - r if (a_ok or b_ok) else 0, a_ok, b_ok)
                    bias_ref[hh, i * GRID_W:(i + 1) * GRID_W, boff + jp * LANES:boff + (jp + 1) * LANES] = t
            boff += (k1 - k0) * GRID_W


def _na_attn_kernel(q_ref, k_ref, v_ref, kc_ref, vc_ref, rel_ref, o_ref, bias_ref, *, groups, rs, kh):
    @pl.when(pl.program_id(1) == 0)
    def _():
        _na_build_bias(rel_ref, bias_ref, groups, rs, kh)

    gq = ROWS_PER_GROUP * GRID_W
    lane = lax.broadcasted_iota(jnp.int32, (gq, LANES), 1)
    first = lane < HEAD_DIM

    past = kc_ref.shape[-1]
    seq = q_ref.shape[0] // NA_SEQS_PER_STEP
    for sq in range(NA_SEQS_PER_STEP):
        kc = kc_ref[sq, 0].reshape(LANES, past).astype(BF16)
        vc = vc_ref[sq, 0].reshape(LANES, past).astype(BF16)
        r0 = sq * seq
        for g, (k0, k1) in enumerate(groups):
            q = q_ref[r0 + g * gq:r0 + (g + 1) * gq, :]
            k = k_ref[r0 + k0 * GRID_W:r0 + k1 * GRID_W, :]
            v = v_ref[r0 + k0 * GRID_W:r0 + k1 * GRID_W, :]
            boff = sum((b - a) * GRID_W for a, b in groups[:g])
            qm = jnp.concatenate([jnp.where(first, q, jnp.zeros_like(q)), jnp.where(first, jnp.zeros_like(q), q)],
                                 axis=0)
            bias = jnp.concatenate([bias_ref[hh, :, boff:boff + (k1 - k0) * GRID_W] for hh in range(2)], axis=0)
            o = _softmax_pv([_dot_nt(qm, k) + bias, _dot(qm, kc)], [v, vc], [_dot, _dot_nt])
            o_ref[r0 + g * gq:r0 + (g + 1) * gq, :] = jnp.where(first, o[:gq], o[gq:]).astype(BF16)


def _na_attention(qkv, kc, vc, rel, first_seq, nb, seq):
    t = nb * seq
    past = kc.shape[4]
    npair = D_ATTN // LANES
    groups, rs, kh = _na_groups(seq // GRID_W)
    width = sum(k1 - k0 for k0, k1 in groups) * GRID_W
    nsq = NA_SEQS_PER_STEP
    assert nb % nsq == 0 and first_seq % nsq == 0
    tok = lambda off: pl.BlockSpec((nsq * seq, LANES), lambda hp, b: (first_seq // nsq + b, off + hp))
    ctx = pl.BlockSpec((nsq, 1, 2, HEAD_DIM, past), lambda hp, b: (b, 0, hp, 0, 0))
    return pl.pallas_call(
        functools.partial(_na_attn_kernel, groups=groups, rs=[int(x) for x in rs], kh=kh),
        grid=(npair, nb // nsq),
        in_specs=[tok(0), tok(npair), tok(2 * npair), ctx, ctx,
                  pl.BlockSpec((2,) + rel.shape[1:], lambda hp, b: (hp, 0, 0))],
        out_specs=pl.BlockSpec((nsq * seq, LANES), lambda hp, b: (b, hp)),
        out_shape=jax.ShapeDtypeStruct((t, D_ATTN), BF16),
        scratch_shapes=[pltpu.VMEM((2, ROWS_PER_GROUP * GRID_W, width), F32)],
        compiler_params=_params(2),
        name="na_attention",
    )(qkv, qkv, qkv, kc, vc, rel)


def _gelu_tanh(x):
    c = float(np.sqrt(2.0 / np.pi))
    t = jnp.tanh(x * (c + (c * 0.044715) * (x * x)))
    return x * (0.5 * t + 0.5)


def _build_blockdiag(wa_ref, wi_ref, wbd_ref):
    per = LRU_TILE // LRU_BLOCK
    wbd_ref[...] = jnp.zeros(wbd_ref.shape, BF16)
    zero = jnp.zeros((LRU_BLOCK, LRU_BLOCK), F32)
    for d in range(2):
        for t in range(N_LRU_TILES):
            for p in range(per):
                rows = slice(p * LRU_BLOCK, (p + 1) * LRU_BLOCK)
                for half, w_ref in enumerate((wa_ref, wi_ref)):
                    blk = 0.5 * w_ref[d, t * per + p]
                    piece = jnp.concatenate([blk, zero] if p % 2 == 0 else [zero, blk], axis=1)
                    c0 = half * LRU_TILE + (p // 2) * LANES
                    wbd_ref[d, t, rows, c0:c0 + LANES] = piece.astype(BF16)


def _lru_kernel(xl_ref, gl_ref, h0_ref, cw_ref, cb_ref, wa_ref, wi_ref, ba_ref, bi_ref, coef_ref,
                y_ref, hfin_ref, xpad, xc, a_buf, hf_buf, hb_buf, wbd_ref, *, seq):
    pitch = seq // N_SEG + SEG_PAD
    lp = N_SEG * pitch
    halo = SUBLANES
    chunk = LANES
    n_chunks = seq // chunk
    left = CONV_W // 2

    @pl.when(pl.program_id(0) == 0)
    def _():
        _build_blockdiag(wa_ref, wi_ref, wbd_ref)

    for k in range(N_SLABS):
        cols = slice(k * LANES, (k + 1) * LANES)
        xpad[k, 0:halo, :] = jnp.zeros((halo, LANES), F32)
        xpad[k, halo + seq:2 * halo + seq, :] = jnp.zeros((halo, LANES), F32)
        xpad[k, halo:halo + seq, :] = xl_ref[:, cols]

    def conv_body(i, carry):
        r0 = pl.multiple_of(i * chunk, chunk)
        for k in range(N_SLABS):
            cols = slice(k * LANES, (k + 1) * LANES)
            out = cb_ref[:, cols]
            for j in range(CONV_W):
                out = out + xpad[k, pl.ds(r0 + halo + j - left, chunk), :] * cw_ref[j:j + 1, cols]
            xc[k, pl.ds(r0, chunk), :] = out
        return carry

    lax.fori_loop(0, n_chunks, conv_body, 0)

    a_buf[:, seq:lp, :] = jnp.ones((N_SLABS, lp - seq, LANES), F32)
    hf_buf[:, seq:lp, :] = jnp.zeros((N_SLABS, lp - seq, LANES), F32)
    hb_buf[:, seq:lp, :] = jnp.zeros((N_SLABS, lp - seq, LANES), F32)

    row = lax.broadcasted_iota(jnp.int32, (N_SEG, LANES), 0)
    for d, h_buf in enumerate((hf_buf, hb_buf)):
        def gates_body(i, carry, d=d, h_buf=h_buf):
            rows = pl.ds(pl.multiple_of(i * chunk, chunk), chunk)
            per = LRU_TILE // LANES
            for c in range(N_LRU_TILES):
                xs = [xc[c * per + half, rows, :] for half in range(per)]
                pre = _dot(jnp.concatenate(xs, axis=1).astype(BF16), wbd_ref[d, c])
                for half in range(per):
                    k = c * per + half
                    cols = slice(k * LANES, (k + 1) * LANES)
                    tr = jnp.tanh(pre[:, half * LANES:(half + 1) * LANES] + ba_ref[d:d + 1, cols])
                    ti = jnp.tanh(pre[:, LRU_TILE + half * LANES:LRU_TILE + (half + 1) * LANES]
                                  + bi_ref[d:d + 1, cols])
                    a = jnp.exp2(coef_ref[d:d + 1, cols] * tr + coef_ref[d:d + 1, cols])
                    y = (1.0 - a) * (1.0 + a)
                    mult = y * lax.rsqrt(jnp.maximum(y, F32_TINY))
                    a_buf[k, rows, :] = a
                    h_buf[k, rows, :] = mult * ((0.5 * ti + 0.5) * xs[half])
            return carry

        lax.fori_loop(0, n_chunks, gates_body, 0, unroll=min(8, n_chunks))

        base0 = jnp.int32(0 if d == 0 else pitch - SCAN_UNROLL)
        base_step = SCAN_UNROLL if d == 0 else -SCAN_UNROLL

        def seg_rows(base, u, d=d):
            return pl.ds(base + (u if d == 0 else SCAN_UNROLL - 1 - u), N_SEG, stride=pitch)

        def totals_body(_, carry, h_buf=h_buf):
            base, hs, ps = carry[0], list(carry[1]), list(carry[2])
            for u in range(SCAN_UNROLL):
                idx = seg_rows(base, u)
                for k in range(N_SLABS):
                    a = a_buf[k, idx, :]
                    hs[k] = a * hs[k] + h_buf[k, idx, :]
                    ps[k] = ps[k] * a
            return base + base_step, tuple(hs), tuple(ps)

        zeros = tuple(jnp.zeros((N_SEG, LANES), F32) for _ in range(N_SLABS))
        ones = tuple(jnp.ones((N_SEG, LANES), F32) for _ in range(N_SLABS))
        _, h_end, p_end = lax.fori_loop(0, pitch // SCAN_UNROLL, totals_body, (base0, zeros, ones))

        order = range(N_SEG) if d == 0 else range(N_SEG - 1, -1, -1)
        starts = []
        for k in range(N_SLABS):
            c = h0_ref[0, d:d + 1, k * LANES:(k + 1) * LANES]
            cm = jnp.zeros((N_SEG, LANES), F32)
            for s in order:
                cm = jnp.where(row == s, c, cm)
                c = p_end[k][s:s + 1, :] * c + h_end[k][s:s + 1, :]
            starts.append(cm)
            hfin_ref[0, d:d + 1, k * LANES:(k + 1) * LANES] = c

        def scan_body(_, carry, h_buf=h_buf):
            base, hs = carry[0], list(carry[1])
            for u in range(SCAN_UNROLL):
                idx = seg_rows(base, u)
                for k in range(N_SLABS):
                    hs[k] = a_buf[k, idx, :] * hs[k] + h_buf[k, idx, :]
                    h_buf[k, idx, :] = hs[k]
            return base + base_step, tuple(hs)

        lax.fori_loop(0, pitch // SCAN_UNROLL, scan_body, (base0, tuple(starts)))

    def out_body(i, carry):
        r0 = pl.multiple_of(i * chunk, chunk)
        for k in range(N_SLABS):
            cols = slice(k * LANES, (k + 1) * LANES)
            h = hf_buf[k, pl.ds(r0, chunk), :] + hb_buf[k, pl.ds(r0, chunk), :]
            y_ref[pl.ds(r0, chunk), cols] = (h * _gelu_tanh(gl_ref[pl.ds(r0, chunk), cols])).astype(BF16)
        return carry

    lax.fori_loop(0, n_chunks, out_body, 0)


def _lru(xl, gl, h0, conv_w, conv_b, wa, wi, ba, bi, coef, first_seq, nb, seq):
    t = nb * seq
    lp = N_SEG * (seq // N_SEG + SEG_PAD)
    tok_in = pl.BlockSpec((seq, D_LRU), lambda b: (first_seq + b, 0))
    tok = pl.BlockSpec((seq, D_LRU), lambda b: (b, 0))
    state = pl.BlockSpec((1, 2, D_LRU), lambda b: (b, 0, 0))
    scan_buf = pltpu.VMEM((N_SLABS, lp, LANES), F32)
    return pl.pallas_call(
        functools.partial(_lru_kernel, seq=seq),
        grid=(nb,),
        in_specs=[tok_in, tok_in, state, _const_spec(conv_w.shape), _const_spec(conv_b.shape),
                  _const_spec(wa.shape), _const_spec(wi.shape), _const_spec(ba.shape), _const_spec(bi.shape),
                  _const_spec(coef.shape)],
        out_specs=[tok, state],
        out_shape=[jax.ShapeDtypeStruct((t, D_LRU), BF16), jax.ShapeDtypeStruct((nb, 2, D_LRU), F32)],
        scratch_shapes=[pltpu.VMEM((N_SLABS, seq + 2 * SUBLANES, LANES), F32),
                        pltpu.VMEM((N_SLABS, seq, LANES), F32), scan_buf, scan_buf, scan_buf,
                        pltpu.VMEM((2, N_LRU_TILES, LRU_TILE, 2 * LRU_TILE), BF16)],
        compiler_params=_params(),
        name="rglru",
    )(xl, gl, h0, conv_w, conv_b, wa, wi, ba, bi, coef)


def kernel(x_prompt, x_sample, cache_k, cache_v, state_lru, c, c_ctx, w_mod, b_mod, norm_g, ffn1_w_in, ffn1_w_out,
           w_in, rpb, conv_w, conv_b, lru_wa, lru_ba, lru_wi, lru_bi, lru_lambda, w_br_attn, w_br_lru, w_out,
           ffn2_w_in, ffn2_w_out, final_g):
    assert w_mod.shape[0] == 1, "single trunk layer"
    nb_ctx, seq_ctx, _ = x_prompt.shape
    nb_lat, seq_lat, _ = x_sample.shape

    n_cond = 1 + nb_lat
    cond = jnp.concatenate([c_ctx[None, :], c, jnp.zeros((2 * SUBLANES - n_cond, D_MODEL), F32)], axis=0)
    mods, wgu1, wo1, win = _mods(cond, w_mod[0], b_mod[0], (ffn1_w_in[0], ffn1_w_out[0], w_in[0]))

    late_weights = (w_br_attn[0], w_br_lru[0], w_out[0], ffn2_w_in[0], ffn2_w_out[0])
    coef = (0.5 * float(np.log2(np.e))) * (-LRU_C * jax.nn.softplus(-lru_lambda[0]))
    ba = 0.5 * lru_ba[0]
    bi = 0.5 * lru_bi[0]
    ng = norm_g[0]
    fg = final_g.reshape(1, D_MODEL)
    cb = conv_b[0].reshape(1, D_LRU)

    t_ctx = nb_ctx * seq_ctx
    assert t_ctx % seq_lat == 0, "latent sequences must start on a latent-sequence-sized row block"
    xp = x_prompt.reshape(t_ctx, D_MODEL)
    xs = x_sample.reshape(nb_lat * seq_lat, D_MODEL)

    x1, qkv, xl, gl, gates, new_k, new_v, wba, wbl, wout, wgu2, wo2 = _ffn1(
        xp, xs, mods, ng, wgu1, wo1, win, late_weights, nb_ctx, seq_ctx, seq_lat)

    attnp = _ctx_attention(qkv, nb_ctx, seq_ctx)
    lrup, h_fin = _lru(xl, gl, jnp.zeros((nb_ctx, 2, D_LRU), F32), conv_w[0], cb, lru_wa[0], lru_wi[0],
                       ba, bi, coef, 0, nb_ctx, seq_ctx)

    rel = _na_rel_rows(rpb[0])
    first_lat = t_ctx // seq_lat
    attns = _na_attention(qkv, cache_k.swapaxes(3, 4), cache_v.swapaxes(3, 4), rel, first_lat, nb_lat, seq_lat)
    lrus, _ = _lru(xl, gl, state_lru[:, 0], conv_w[0], cb, lru_wa[0], lru_wi[0], ba, bi, coef,
                   first_lat, nb_lat, seq_lat)

    yp, ys = _ffn2(x1, attnp, attns, lrup, lrus, gates, mods, ng, fg, wba, wbl, wout, wgu2, wo2,
                   FFN2_TILE, seq_lat)
    return (yp.reshape(nb_ctx, seq_ctx, D_MODEL), ys.reshape(nb_lat, seq_lat, D_MODEL),
            new_k, new_v, h_fin[:, None])
```
